```python
import jax, jax.numpy as jnp
from jax import lax
import numpy as np

D_MODEL = 4096
BATCH = 2
SEQ = 8192
DEPTH = 2

HEAD_DIM = 128
ROPE_THETA = 10000.0
LN_EPS = 1e-5
DIL_GROUPS = ((128, 1), (512, 4), (2048, 16))
N_DIL_GROUPS = 3
DIL_HEADS = D_MODEL // HEAD_DIM // 2
BAND_BLOCK = 128
A_WIDTH = 3 * N_DIL_GROUPS * DIL_HEADS * HEAD_DIM
DSA_HEADS = D_MODEL // HEAD_DIM
DSA_KV_HEADS = DSA_HEADS // 4
DSA_GROUP = DSA_HEADS // DSA_KV_HEADS
IDX_HEADS = 32
IDX_DIM = HEAD_DIM
TOPK_MAX = 256
QUERY_BLOCK = 128
B_Q = DSA_HEADS * HEAD_DIM
B_KV = DSA_KV_HEADS * HEAD_DIM
B_QI = IDX_HEADS * IDX_DIM
B_WIDTH = B_Q + 2 * B_KV + B_QI + IDX_DIM + IDX_HEADS
D_FF = -(-8 * D_MODEL // (3 * 256)) * 256
DEEPNORM_ALPHA = (2 * DEPTH) ** 0.25
DEEPNORM_BETA = (8 * DEPTH) ** -0.25

kernel_name = "hybrid_dilated_dsa_block"


def rope_tables(positions):
    inv = ROPE_THETA ** (-jnp.arange(0, HEAD_DIM, 2, dtype=jnp.float32) / HEAD_DIM)
    ang = positions.astype(jnp.float32)[..., None] * inv
    return jnp.cos(ang), jnp.sin(ang)


def apply_rope(x, cos, sin):
    xf = x.astype(jnp.float32)
    x1, x2 = jnp.split(xf, 2, axis=-1)
    c = cos[:, :, None, :]
    s = sin[:, :, None, :]
    return jnp.concatenate([x1 * c - x2 * s, x2 * c + x1 * s], axis=-1).astype(x.dtype)


def layer_norm(x, gain, bias):
    xf = x.astype(jnp.float32)
    mu = jnp.mean(xf, axis=-1, keepdims=True)
    var = jnp.mean(jnp.square(xf - mu), axis=-1, keepdims=True)
    y = (xf - mu) * lax.rsqrt(var + LN_EPS)
    return (y * gain.astype(jnp.float32) + bias.astype(jnp.float32)).astype(x.dtype)


def banded_window_attention(q, k, v, window):
    N, L, H, Dh = q.shape
    nb = -(-L // BAND_BLOCK)
    Lp = nb * BAND_BLOCK
    pad = Lp - L
    qb = jnp.pad(q, ((0, 0), (0, pad), (0, 0), (0, 0))).reshape(N, nb, BAND_BLOCK, H, Dh)

    def key_blocks(t):
        tp = jnp.pad(t, ((0, 0), (BAND_BLOCK, pad), (0, 0), (0, 0)))
        tp = tp.reshape(N, nb + 1, BAND_BLOCK, H, Dh)
        return jnp.concatenate([tp[:, :-1], tp[:, 1:]], axis=2)

    kb = key_blocks(k)
    vb = key_blocks(v)
    s = jnp.einsum('nbqhd,nbkhd->nbhqk', qb, kb).astype(jnp.float32) * (Dh ** -0.5)
    qi = jnp.arange(nb)[:, None] * BAND_BLOCK + jnp.arange(BAND_BLOCK)[None, :]
    ki = (jnp.arange(nb)[:, None] - 1) * BAND_BLOCK + jnp.arange(2 * BAND_BLOCK)[None, :]
    dist = qi[:, :, None] - ki[:, None, :]
    mask = (dist >= 0) & (dist <= window) & (ki[:, None, :] >= 0)
    s = jnp.where(mask[None, :, None], s, -jnp.inf)
    lse = jax.nn.logsumexp(s, axis=-1)
    p = jnp.exp(s - lse[..., None]).astype(v.dtype)
    o = jnp.einsum('nbhqk,nbkhd->nbqhd', p, vb).reshape(N, Lp, H, Dh)[:, :L]
    lse = lse.transpose(0, 1, 3, 2).reshape(N, Lp, H)[:, :L]
    return o, lse


def dilated_attention(q, k, v, window, dilation):
    B, S, H, Dh = q.shape
    Ls = S // dilation

    def to_sub(t):
        return t.reshape(B, Ls, dilation, H, Dh).transpose(0, 2, 1, 3, 4).reshape(B * dilation, Ls, H, Dh)

    o, lse = banded_window_attention(to_sub(q), to_sub(k), to_sub(v), window // dilation)
    o = o.reshape(B, dilation, Ls, H, Dh).transpose(0, 2, 1, 3, 4).reshape(B, S, H, Dh)
    lse = lse.reshape(B, dilation, Ls, H).transpose(0, 2, 1, 3).reshape(B, S, H)
    return o, lse


def mixer_dilated(x, w_in, w_out, cos, sin):
    B, S, _ = x.shape
    proj = (x @ w_in).reshape(B, S, 3, N_DIL_GROUPS, DIL_HEADS, HEAD_DIM)
    outs, lses = [], []
    for g, (window, dilation) in enumerate(DIL_GROUPS):
        q = apply_rope(proj[:, :, 0, g], cos, sin)
        k = apply_rope(proj[:, :, 1, g], cos, sin)
        v = proj[:, :, 2, g]
        o, lse = dilated_attention(q, k, v, window, dilation)
        outs.append(o)
        lses.append(lse)
    wts = jax.nn.softmax(jnp.stack(lses), axis=0)
    o = jnp.einsum('gbsh,gbshd->bshd', wts.astype(x.dtype), jnp.stack(outs))
    return o.reshape(B, S, DIL_HEADS * HEAD_DIM) @ w_out


def mixer_dsa(x, w_in, k_idx_gain, k_idx_bias, w_out, cos, sin):
    B, S, _ = x.shape
    proj = x @ w_in
    o0 = B_Q
    o1 = o0 + B_KV
    o2 = o1 + B_KV
    o3 = o2 + B_QI
    o4 = o3 + IDX_DIM
    q = apply_rope(proj[..., :o0].reshape(B, S, DSA_HEADS, HEAD_DIM), cos, sin)
    k = apply_rope(proj[..., o0:o1].reshape(B, S, DSA_KV_HEADS, HEAD_DIM), cos, sin)
    v = proj[..., o1:o2].reshape(B, S, DSA_KV_HEADS, HEAD_DIM)
    qi = apply_rope(proj[..., o2:o3].reshape(B, S, IDX_HEADS, IDX_DIM), cos, sin)
    ki = layer_norm(proj[..., o3:o4], k_idx_gain, k_idx_bias)
    ki = apply_rope(ki[:, :, None, :], cos, sin)[:, :, 0]
    wi = proj[..., o4:] * (IDX_HEADS ** -0.5 * IDX_DIM ** -0.5)

    n_sel = min(TOPK_MAX, S // 4)
    nqb = S // QUERY_BLOCK

    def to_blocks(t):
        return jnp.moveaxis(t.reshape((B, nqb, QUERY_BLOCK) + t.shape[2:]), 1, 0)

    starts = jnp.arange(nqb, dtype=jnp.int32) * QUERY_BLOCK
    key_pos = jnp.arange(S, dtype=jnp.int32)

    def block(args):
        qb, qib, wib, start = args
        t = start + jnp.arange(QUERY_BLOCK, dtype=jnp.int32)
        rel = jax.nn.relu(jnp.einsum('bqhd,bsd->bqhs', qib, ki))
        score = jnp.einsum('bqhs,bqh->bqs', rel, wib).astype(jnp.float32)
        causal = key_pos[None, :] <= t[:, None]
        score = jnp.where(causal[None], score, -jnp.inf)
        _, idx = lax.top_k(score, n_sel)
        valid = idx <= t[None, :, None]
        kg = jax.vmap(lambda kk, ii: kk[ii])(k, idx)
        vg = jax.vmap(lambda vv, ii: vv[ii])(v, idx)
        qg = qb.reshape(B, QUERY_BLOCK, DSA_KV_HEADS, DSA_GROUP, HEAD_DIM)
        s = jnp.einsum('bqkgd,bqnkd->bqkgn', qg, kg).astype(jnp.float32) * (HEAD_DIM ** -0.5)
        s = jnp.where(valid[:, :, None, None, :], s, -jnp.inf)
        p = jax.nn.softmax(s, axis=-1).astype(vg.dtype)
        o = jnp.einsum('bqkgn,bqnkd->bqkgd', p, vg)
        return o.reshape(B, QUERY_BLOCK, DSA_HEADS * HEAD_DIM)

    o = lax.map(block, (to_blocks(q), to_blocks(qi), to_blocks(wi), starts))
    o = jnp.moveaxis(o, 0, 1).reshape(B, S, DSA_HEADS * HEAD_DIM)
    return o @ w_out


def swiglu(x, w_gate, w_up, w_down):
    return (jax.nn.silu(x @ w_gate) * (x @ w_up)) @ w_down


def setup_inputs(seed: int = 0) -> dict:
    key = jax.random.key(seed)
    ks = jax.random.split(key, 32)
    f32 = jnp.float32
    sd = D_MODEL ** -0.5

    def nrm(k, shape, scale):
        return jax.random.normal(k, shape, f32) * scale

    x = jax.random.normal(ks[0], (BATCH, SEQ, D_MODEL), f32)
    start = jax.random.randint(ks[1], (BATCH, 1), 0, 4096, dtype=jnp.int32)
    positions = start + jnp.arange(SEQ, dtype=jnp.int32)[None, :]

    col_scale = jnp.array([1.0, 1.0, DEEPNORM_BETA], f32)[None, :, None]
    l0_attn_w_in = (nrm(ks[2], (D_MODEL, 3, A_WIDTH // 3), sd) * col_scale).reshape(D_MODEL, A_WIDTH)
    l0_attn_w_out = nrm(ks[3], (DIL_HEADS * HEAD_DIM, D_MODEL), (DIL_HEADS * HEAD_DIM) ** -0.5 * DEEPNORM_BETA)

    l1_attn_w_in = jnp.concatenate([
        nrm(ks[4], (D_MODEL, B_Q), sd),
        nrm(ks[5], (D_MODEL, B_KV), sd),
        nrm(ks[6], (D_MODEL, B_KV), sd * DEEPNORM_BETA),
        nrm(ks[7], (D_MODEL, B_QI), sd),
        nrm(ks[8], (D_MODEL, IDX_DIM), sd),
        nrm(ks[9], (D_MODEL, IDX_HEADS), sd),
    ], axis=1)
    l1_k_idx_gain = 1.0 + nrm(ks[10], (IDX_DIM,), 0.02)
    l1_k_idx_bias = nrm(ks[11], (IDX_DIM,), 0.02)
    l1_attn_w_out = nrm(ks[12], (B_Q, D_MODEL), B_Q ** -0.5 * DEEPNORM_BETA)

    out = {"x": x, "positions": positions,
           "l0_attn_w_in": l0_attn_w_in, "l0_attn_w_out": l0_attn_w_out,
           "l1_attn_w_in": l1_attn_w_in, "l1_k_idx_gain": l1_k_idx_gain,
           "l1_k_idx_bias": l1_k_idx_bias, "l1_attn_w_out": l1_attn_w_out}
    for i in range(2):
        b = 13 + 9 * i
        out[f"l{i}_ln_mix_gain"] = 1.0 + nrm(ks[b], (D_MODEL,), 0.02)
        out[f"l{i}_ln_mix_bias"] = nrm(ks[b + 1], (D_MODEL,), 0.02)
        out[f"l{i}_ffn_gate"] = nrm(ks[b + 2], (D_MODEL, D_FF), sd)
        out[f"l{i}_ffn_up"] = nrm(ks[b + 3], (D_MODEL, D_FF), sd * DEEPNORM_BETA)
        out[f"l{i}_ffn_down"] = nrm(ks[b + 4], (D_FF, D_MODEL), D_FF ** -0.5 * DEEPNORM_BETA)
        out[f"l{i}_ln_ffn_gain"] = 1.0 + nrm(ks[b + 5], (D_MODEL,), 0.02)
        out[f"l{i}_ln_ffn_bias"] = nrm(ks[b + 6], (D_MODEL,), 0.02)
    return out


def reference(x, positions, l0_attn_w_in, l0_attn_w_out, l1_attn_w_in, l1_k_idx_gain,
              l1_k_idx_bias, l1_attn_w_out,
              l0_ln_mix_gain, l0_ln_mix_bias, l0_ffn_gate, l0_ffn_up, l0_ffn_down,
              l0_ln_ffn_gain, l0_ln_ffn_bias,
              l1_ln_mix_gain, l1_ln_mix_bias, l1_ffn_gate, l1_ffn_up, l1_ffn_down,
              l1_ln_ffn_gain, l1_ln_ffn_bias):
    cos, sin = rope_tables(positions)
    mix_norms = ((l0_ln_mix_gain, l0_ln_mix_bias), (l1_ln_mix_gain, l1_ln_mix_bias))
    ffns = ((l0_ffn_gate, l0_ffn_up, l0_ffn_down), (l1_ffn_gate, l1_ffn_up, l1_ffn_down))
    ffn_norms = ((l0_ln_ffn_gain, l0_ln_ffn_bias), (l1_ln_ffn_gain, l1_ln_ffn_bias))
    for i in range(DEPTH):
        if i % 2 == 0:
            m = mixer_dilated(x, l0_attn_w_in, l0_attn_w_out, cos, sin)
        else:
            m = mixer_dsa(x, l1_attn_w_in, l1_k_idx_gain, l1_k_idx_bias, l1_attn_w_out, cos, sin)
        x = layer_norm(DEEPNORM_ALPHA * x + m, *mix_norms[i])
        x = layer_norm(DEEPNORM_ALPHA * x + swiglu(x, *ffns[i]), *ffn_norms[i])
    return x
```

```python
import functools

import numpy as np
import jax
import jax.numpy as jnp
from jax import lax
from jax.experimental import pallas as pl
from jax.experimental.pallas import tpu as pltpu

F32 = jnp.float32
BF16 = jnp.bfloat16
I32 = jnp.int32

HEAD_DIM = 128
HALF = HEAD_DIM // 2
ROPE_THETA = 10000.0
LN_EPS = 1e-5
DEPTH = 2
DEEPNORM_ALPHA = (2 * DEPTH) ** 0.25
ATTN_SCALE = HEAD_DIM ** -0.5

DIL_GROUPS = ((128, 1), (512, 4), (2048, 16))
BAND = 128
DIL_HEADS = 16
DIL_COLS = DIL_HEADS * HEAD_DIM

DSA_HEADS = 32
DSA_KV_HEADS = 8
DSA_GROUP = DSA_HEADS // DSA_KV_HEADS
IDX_HEADS = 32
TOPK = 256
QUERY_BLOCK = 128
KEY_CHUNK = 512

INT_MIN = np.int32(-2 ** 31)
MASK_NEG = -1e30
M_INIT = -1e29

VMEM_LIMIT = 56 * 1024 * 1024


def _params(n_axes):
    return pltpu.CompilerParams(dimension_semantics=("arbitrary",) * n_axes,
                                vmem_limit_bytes=VMEM_LIMIT)


def _dot_nt(a, b):
    return lax.dot_general(a, b, (((1,), (1,)), ((), ())), preferred_element_type=F32)


def _rope(a, c, s):
    return a * c + pltpu.roll(a, HALF, 1) * s


def _rope_table_kernel(pos_ref, inv_ref, cos_ref, sin_ref):
    ang = pos_ref[...].astype(F32) * inv_ref[...]
    lane = lax.broadcasted_iota(I32, ang.shape, 1)
    cos_ref[...] = jnp.cos(ang)
    sin_ref[...] = jnp.where(lane < HALF, -jnp.sin(ang), jnp.sin(ang))


def _rope_tables(positions):
    m = positions.size
    tm = min(m, 1024)
    inv = ROPE_THETA ** (-jnp.arange(0, HEAD_DIM, 2, dtype=F32) / HEAD_DIM)
    inv2 = jnp.concatenate([inv, inv]).reshape(1, HEAD_DIM)
    return pl.pallas_call(
        _rope_table_kernel,
        grid=(m // tm,),
        in_specs=[pl.BlockSpec((tm, 1), lambda i: (i, 0)),
                  pl.BlockSpec((1, HEAD_DIM), lambda i: (0, 0))],
        out_specs=[pl.BlockSpec((tm, HEAD_DIM), lambda i: (i, 0))] * 2,
        out_shape=[jax.ShapeDtypeStruct((m, HEAD_DIM), F32)] * 2,
        compiler_params=_params(1),
        name="rope_tables",
    )(positions.reshape(m, 1), inv2)


def _mm_kernel(x_ref, w_ref, *rest, rope):
    if rope:
        cos_ref, sin_ref, o_ref, wbf_ref = rest
    else:
        o_ref, wbf_ref = rest

    @pl.when(pl.program_id(1) == 0)
    def _():
        wbf_ref[...] = w_ref[...].astype(BF16)

    acc = jnp.dot(x_ref[...], wbf_ref[...], preferred_element_type=F32)
    if rope:
        c = cos_ref[...]
        s = sin_ref[...]
        for j in range(acc.shape[1] // HEAD_DIM):
            sl = slice(j * HEAD_DIM, (j + 1) * HEAD_DIM)
            o_ref[:, sl] = _rope(acc[:, sl], c, s).astype(o_ref.dtype)
    else:
        o_ref[...] = acc.astype(o_ref.dtype)


def _matmul(x, w, col0, n_cols, out_dtype, *, tn, rope=None, name):
    m, k = x.shape
    tm = min(m, 1024)
    assert m % tm == 0 and n_cols % tn == 0 and col0 % tn == 0
    off = col0 // tn
    in_specs = [pl.BlockSpec((tm, k), lambda n, i: (i, 0)),
                pl.BlockSpec((k, tn), lambda n, i: (0, n + off))]
    args = [x, w]
    if rope is not None:
        in_specs += [pl.BlockSpec((tm, HEAD_DIM), lambda n, i: (i, 0))] * 2
        args += list(rope)
    return pl.pallas_call(
        functools.partial(_mm_kernel, rope=rope is not None),
        grid=(n_cols // tn, m // tm),
        in_specs=in_specs,
        out_specs=pl.BlockSpec((tm, tn), lambda n, i: (i, n)),
        out_shape=jax.ShapeDtypeStruct((m, n_cols), out_dtype),
        scratch_shapes=[pltpu.VMEM((k, tn), BF16)],
        compiler_params=_params(2),
        name=name,
    )(*args)


def _kiwi_kernel(x_ref, w_ref, cos_ref, sin_ref, g_ref, b_ref, ki_ref, wi_ref, wbf_ref):
    @pl.when(pl.program_id(0) == 0)
    def _():
        wbf_ref[...] = w_ref[...].astype(BF16)

    acc = jnp.dot(x_ref[...], wbf_ref[...], preferred_element_type=F32)
    a = acc[:, :HEAD_DIM]
    mu = jnp.mean(a, axis=-1, keepdims=True)
    var = jnp.mean(jnp.square(a - mu), axis=-1, keepdims=True)
    y = (a - mu) * lax.rsqrt(var + LN_EPS) * g_ref[...] + b_ref[...]
    ki_ref[...] = _rope(y, cos_ref[...], sin_ref[...]).astype(ki_ref.dtype)
    wi_ref[...] = acc[:, HEAD_DIM:] * (IDX_HEADS ** -0.5 * HEAD_DIM ** -0.5)


def _kiwi(x, w_kw, cosf, sinf, gain, bias):
    m, k = x.shape
    tm = min(m, 1024)
    n = 2 * HEAD_DIM
    row = lambda i: (i, 0)
    fixed = lambda i: (0, 0)
    return pl.pallas_call(
        _kiwi_kernel,
        grid=(m // tm,),
        in_specs=[pl.BlockSpec((tm, k), row), pl.BlockSpec((k, n), fixed),
                  pl.BlockSpec((tm, HEAD_DIM), row), pl.BlockSpec((tm, HEAD_DIM), row),
                  pl.BlockSpec((1, HEAD_DIM), fixed), pl.BlockSpec((1, HEAD_DIM), fixed)],
        out_specs=[pl.BlockSpec((tm, HEAD_DIM), row)] * 2,
        out_shape=[jax.ShapeDtypeStruct((m, HEAD_DIM), BF16),
                   jax.ShapeDtypeStruct((m, HEAD_DIM), F32)],
        scratch_shapes=[pltpu.VMEM((k, n), BF16)],
        compiler_params=_params(1),
        name="dsa_kiwi_proj",
    )(x, w_kw, cosf, sinf, gain.reshape(1, HEAD_DIM), bias.reshape(1, HEAD_DIM))


def _ffn_up_kernel(x_ref, wg_ref, wu_ref, h_ref, wgbf_ref, wubf_ref):
    @pl.when(pl.program_id(1) == 0)
    def _():
        wgbf_ref[...] = wg_ref[...].astype(BF16)
        wubf_ref[...] = wu_ref[...].astype(BF16)

    x = x_ref[...]
    g = jnp.dot(x, wgbf_ref[...], preferred_element_type=F32)
    u = jnp.dot(x, wubf_ref[...], preferred_element_type=F32)
    h_ref[...] = (g * (1.0 / (1.0 + jnp.exp(-g))) * u).astype(h_ref.dtype)


def _ffn_up(x, w_gate, w_up, *, tn=256):
    m, k = x.shape
    n = w_gate.shape[1]
    tm = min(m, 1024)
    assert n % tn == 0
    wspec = pl.BlockSpec((k, tn), lambda j, i: (0, j))
    return pl.pallas_call(
        _ffn_up_kernel,
        grid=(n // tn, m // tm),
        in_specs=[pl.BlockSpec((tm, k), lambda j, i: (i, 0)), wspec, wspec],
        out_specs=pl.BlockSpec((tm, tn), lambda j, i: (i, j)),
        out_shape=jax.ShapeDtypeStruct((m, n), BF16),
        scratch_shapes=[pltpu.VMEM((k, tn), BF16)] * 2,
        compiler_params=_params(2),
        name="ffn_up",
    )(x, w_gate, w_up)


def _mm_acc_kernel(h_ref, w_ref, o_ref):
    @pl.when(pl.program_id(1) == 0)
    def _():
        o_ref[...] = jnp.zeros_like(o_ref)

    o_ref[...] += jnp.dot(h_ref[...], w_ref[...], preferred_element_type=F32)


def _ffn_down(h, w_down_bf, *, tk=256):
    m, k = h.shape
    n = w_down_bf.shape[1]
    tm = min(m, 1024)
    assert k % tk == 0
    return pl.pallas_call(
        _mm_acc_kernel,
        grid=(m // tm, k // tk),
        in_specs=[pl.BlockSpec((tm, tk), lambda i, j: (i, j)),
                  pl.BlockSpec((tk, n), lambda i, j: (j, 0))],
        out_specs=pl.BlockSpec((tm, n), lambda i, j: (i, 0)),
        out_shape=jax.ShapeDtypeStruct((m, n), F32),
        compiler_params=_params(2),
        name="ffn_down",
    )(h, w_down_bf)


def _res_ln_kernel(x_ref, y_ref, g_ref, b_ref, o_ref, obf_ref):
    z = DEEPNORM_ALPHA * x_ref[...] + y_ref[...]
    mu = jnp.mean(z, axis=-1, keepdims=True)
    d = z - mu
    var = jnp.mean(jnp.square(d), axis=-1, keepdims=True)
    out = d * lax.rsqrt(var + LN_EPS) * g_ref[...] + b_ref[...]
    o_ref[...] = out
    obf_ref[...] = out.astype(BF16)


def _res_ln(x, y, gain, bias):
    m, d = x.shape
    tm = min(m, 256)
    row = lambda i: (i, 0)
    fixed = lambda i: (0, 0)
    return pl.pallas_call(
        _res_ln_kernel,
        grid=(m // tm,),
        in_specs=[pl.BlockSpec((tm, d), row), pl.BlockSpec((tm, d), row),
                  pl.BlockSpec((1, d), fixed), pl.BlockSpec((1, d), fixed)],
        out_specs=[pl.BlockSpec((tm, d), row)] * 2,
        out_shape=[jax.ShapeDtypeStruct((m, d), F32), jax.ShapeDtypeStruct((m, d), BF16)],
        compiler_params=_params(1),
        name="residual_layernorm",
    )(x, y, gain.reshape(1, d), bias.reshape(1, d))


def _dil_attn_kernel(q_ref, kp_ref, kc_ref, vp_ref, vc_ref, o_ref, lse_ref):
    has_prev = pl.program_id(2) > 0
    qi = lax.broadcasted_iota(I32, (BAND, BAND), 0)
    kj = lax.broadcasted_iota(I32, (BAND, BAND), 1)
    mask_p = jnp.logical_and(kj >= qi, has_prev)
    mask_c = kj <= qi
    lse_ref[...] = jnp.zeros_like(lse_ref)
    for h in range(DIL_HEADS):
        sl = slice(h * HEAD_DIM, (h + 1) * HEAD_DIM)
        q = q_ref[0, :, sl]
        sp = jnp.where(mask_p, _dot_nt(q, kp_ref[0, :, sl]) * ATTN_SCALE, -jnp.inf)
        sc = jnp.where(mask_c, _dot_nt(q, kc_ref[0, :, sl]) * ATTN_SCALE, -jnp.inf)
        m = jnp.maximum(jnp.max(sp, axis=1, keepdims=True), jnp.max(sc, axis=1, keepdims=True))
        pp = jnp.exp(sp - m)
        pc = jnp.exp(sc - m)
        l = jnp.sum(pp, axis=1, keepdims=True) + jnp.sum(pc, axis=1, keepdims=True)
        o = (jnp.dot(pp.astype(BF16), vp_ref[0, :, sl], preferred_element_type=F32)
             + jnp.dot(pc.astype(BF16), vc_ref[0, :, sl], preferred_element_type=F32))
        o_ref[0, :, sl] = (o / l).astype(o_ref.dtype)
        lse_ref[0, :, h:h + 1] = m + jnp.log(l)


def _dilated_attention(q, k, v, g, dilation, batch, seq):
    d = dilation
    ls = seq // d
    ngrp = len(DIL_GROUPS)
    view = lambda t: t.reshape(batch, ls, d * ngrp * DIL_COLS)
    cur = lambda b, r, lb: (b, lb, r * ngrp + g)
    prev = lambda b, r, lb: (b, jnp.maximum(lb - 1, 0), r * ngrp + g)
    blk = (1, BAND, DIL_COLS)
    o, lse = pl.pallas_call(
        _dil_attn_kernel,
        grid=(batch, d, ls // BAND),
        in_specs=[pl.BlockSpec(blk, cur), pl.BlockSpec(blk, prev), pl.BlockSpec(blk, cur),
                  pl.BlockSpec(blk, prev), pl.BlockSpec(blk, cur)],
        out_specs=[pl.BlockSpec(blk, lambda b, r, lb: (b, lb, r)),
                   pl.BlockSpec((1, BAND, HEAD_DIM), lambda b, r, lb: (b, lb, r))],
        out_shape=[jax.ShapeDtypeStruct((batch, ls, d * DIL_COLS), BF16),
                   jax.ShapeDtypeStruct((batch, ls, d * HEAD_DIM), F32)],
        compiler_params=_params(3),
        name=f"dilated_attention_d{d}",
    )(view(q), view(k), view(k), view(v), view(v))
    return o.reshape(batch * seq, DIL_COLS), lse.reshape(batch * seq, HEAD_DIM)


def _dil_merge_kernel(o0_ref, o1_ref, o2_ref, l0_ref, l1_ref, l2_ref, out_ref):
    l0, l1, l2 = l0_ref[...], l1_ref[...], l2_ref[...]
    mx = jnp.maximum(jnp.maximum(l0, l1), l2)
    e0, e1, e2 = jnp.exp(l0 - mx), jnp.exp(l1 - mx), jnp.exp(l2 - mx)
    inv = 1.0 / (e0 + e1 + e2)
    w0, w1, w2 = e0 * inv, e1 * inv, e2 * inv
    for h in range(DIL_HEADS):
        sl = slice(h * HEAD_DIM, (h + 1) * HEAD_DIM)
        col = slice(h, h + 1)
        acc = (w0[:, col] * o0_ref[:, sl].astype(F32) + w1[:, col] * o1_ref[:, sl].astype(F32)
               + w2[:, col] * o2_ref[:, sl].astype(F32))
        out_ref[:, sl] = acc.astype(out_ref.dtype)


def _dil_merge(outs, lses):
    m = outs[0].shape[0]
    tm = min(m, 512)
    row = lambda i: (i, 0)
    return pl.pallas_call(
        _dil_merge_kernel,
        grid=(m // tm,),
        in_specs=[pl.BlockSpec((tm, DIL_COLS), row)] * 3 + [pl.BlockSpec((tm, HEAD_DIM), row)] * 3,
        out_specs=pl.BlockSpec((tm, DIL_COLS), row),
        out_shape=jax.ShapeDtypeStruct((m, DIL_COLS), BF16),
        compiler_params=_params(1),
        name="dilated_merge",
    )(*outs, *lses)


def _sortable(x):
    b = lax.bitcast_convert_type(x, I32)
    return b ^ ((b >> 31) & np.int32(0x7FFFFFFF))


def _dsa_select_kernel(qi_ref, wi_ref, ki_ref, bias_ref, keys_ref, *, n_chunks):
    qb = pl.program_id(1)
    n_c = (qb * QUERY_BLOCK + QUERY_BLOCK - 1) // KEY_CHUNK + 1
    w = wi_ref[...]
    t = qb * QUERY_BLOCK + lax.broadcasted_iota(I32, (QUERY_BLOCK, KEY_CHUNK), 0)
    lane = lax.broadcasted_iota(I32, (QUERY_BLOCK, KEY_CHUNK), 1)

    def score_chunk(c, carry):
        kic = ki_ref[0, pl.ds(pl.multiple_of(c * KEY_CHUNK, KEY_CHUNK), KEY_CHUNK), :]
        sc = jnp.zeros((QUERY_BLOCK, KEY_CHUNK), F32)
        for h in range(IDX_HEADS):
            a = _dot_nt(qi_ref[:, h * HEAD_DIM:(h + 1) * HEAD_DIM], kic)
            sc = sc + jnp.maximum(a, 0.0) * w[:, h:h + 1]
        causal = c * KEY_CHUNK + lane <= t
        keys_ref[c] = jnp.where(causal, _sortable(sc), INT_MIN)
        return carry

    lax.fori_loop(0, n_c, score_chunk, 0)

    def count_ge(cand):
        def body(c, acc):
            x = jnp.where(keys_ref[c] >= cand, 1, 0).astype(I32)
            for j in range(KEY_CHUNK // HEAD_DIM):
                acc = acc + x[:, j * HEAD_DIM:(j + 1) * HEAD_DIM]
            return acc
        acc = lax.fori_loop(0, n_c, body, jnp.zeros((QUERY_BLOCK, HEAD_DIM), I32))
        return jnp.sum(acc, axis=1, keepdims=True)

    zero = jnp.zeros((QUERY_BLOCK, 1), I32)
    thr = jnp.where(count_ge(zero) >= TOPK, zero, jnp.full((QUERY_BLOCK, 1), INT_MIN, I32))

    def bit_step(i, thr):
        cand = thr | jnp.left_shift(jnp.int32(1), 30 - i)
        return jnp.where(count_ge(cand) >= TOPK, cand, thr)

    thr = lax.fori_loop(0, 31, bit_step, thr)
    thr = jnp.maximum(thr, INT_MIN + 1)

    def write_bias(c, carry):
        bias_ref[0, 0, c] = jnp.where(keys_ref[c] >= thr, 0.0, MASK_NEG).astype(bias_ref.dtype)
        return carry

    def write_masked(c, carry):
        bias_ref[0, 0, c] = jnp.full((QUERY_BLOCK, KEY_CHUNK), MASK_NEG, bias_ref.dtype)
        return carry

    lax.fori_loop(0, n_c, write_bias, 0)
    lax.fori_loop(n_c, n_chunks, write_masked, 0)


def _dsa_select(qi, wi, ki, batch, seq):
    nqb = seq // QUERY_BLOCK
    nkc = seq // KEY_CHUNK
    return pl.pallas_call(
        functools.partial(_dsa_select_kernel, n_chunks=nkc),
        grid=(batch, nqb),
        in_specs=[pl.BlockSpec((QUERY_BLOCK, IDX_HEADS * HEAD_DIM), lambda b, q: (b * nqb + q, 0)),
                  pl.BlockSpec((QUERY_BLOCK, HEAD_DIM), lambda b, q: (b * nqb + q, 0)),
                  pl.BlockSpec((1, seq, HEAD_DIM), lambda b, q: (b, 0, 0))],
        out_specs=pl.BlockSpec((1, 1, nkc, QUERY_BLOCK, KEY_CHUNK), lambda b, q: (b, q, 0, 0, 0)),
        out_shape=jax.ShapeDtypeStruct((batch, nqb, nkc, QUERY_BLOCK, KEY_CHUNK), BF16),
        scratch_shapes=[pltpu.VMEM((nkc, QUERY_BLOCK, KEY_CHUNK), I32)],
        compiler_params=_params(2),
        name="dsa_select",
    )(qi, wi, ki.reshape(batch, seq, HEAD_DIM))


def _dsa_attn_kernel(q_ref, k_ref, v_ref, bias_ref, o_ref, qs_ref, m_ref, l_ref, acc_ref, *, n_chunks):
    qb = pl.program_id(1)
    c = pl.program_id(2)
    last = (qb * QUERY_BLOCK + QUERY_BLOCK - 1) // KEY_CHUNK

    @pl.when(c == 0)
    def _():
        for g in range(DSA_KV_HEADS):
            for j in range(DSA_GROUP):
                h = g * DSA_GROUP + j
                qs_ref[g, j * QUERY_BLOCK:(j + 1) * QUERY_BLOCK, :] = q_ref[:, h * HEAD_DIM:(h + 1) * HEAD_DIM]
        m_ref[...] = jnp.full(m_ref.shape, M_INIT, F32)
        l_ref[...] = jnp.zeros_like(l_ref)
        acc_ref[...] = jnp.zeros_like(acc_ref)

    @pl.when(c <= last)
    def _():
        bias = bias_ref[0, 0, 0].astype(F32)
        bias = jnp.concatenate([bias] * DSA_GROUP, axis=0)
        for g in range(DSA_KV_HEADS):
            sl = slice(g * HEAD_DIM, (g + 1) * HEAD_DIM)
            s = _dot_nt(qs_ref[g], k_ref[:, sl]) * ATTN_SCALE + bias
            m_old = m_ref[g]
            m_new = jnp.maximum(m_old, jnp.max(s, axis=1, keepdims=True))
            alpha = jnp.exp(m_old - m_new)
            p = jnp.exp(s - m_new)
            l_ref[g] = alpha * l_ref[g] + jnp.sum(p, axis=1, keepdims=True)
            acc_ref[g] = alpha * acc_ref[g] + jnp.dot(p.astype(BF16), v_ref[:, sl],
                                                      preferred_element_type=F32)
            m_ref[g] = m_new

    @pl.when(c == n_chunks - 1)
    def _():
        for g in range(DSA_KV_HEADS):
            o = acc_ref[g] / l_ref[g]
            for j in range(DSA_GROUP):
                h = g * DSA_GROUP + j
                o_ref[:, h * HEAD_DIM:(h + 1) * HEAD_DIM] = (
                    o[j * QUERY_BLOCK:(j + 1) * QUERY_BLOCK].astype(o_ref.dtype))


def _dsa_attention(q, k, v, bias, batch, seq):
    nqb = seq // QUERY_BLOCK
    nkc = seq // KEY_CHUNK
    rows = DSA_GROUP * QUERY_BLOCK

    def last_chunk(qb):
        return (qb * QUERY_BLOCK + QUERY_BLOCK - 1) // KEY_CHUNK

    qmap = lambda b, qb, c: (b * nqb + qb, 0)
    kvmap = lambda b, qb, c: (b * nkc + jnp.minimum(c, last_chunk(qb)), 0)
    return pl.pallas_call(
        functools.partial(_dsa_attn_kernel, n_chunks=nkc),
        grid=(batch, nqb, nkc),
        in_specs=[pl.BlockSpec((QUERY_BLOCK, DSA_HEADS * HEAD_DIM), qmap),
                  pl.BlockSpec((KEY_CHUNK, DSA_KV_HEADS * HEAD_DIM), kvmap),
                  pl.BlockSpec((KEY_CHUNK, DSA_KV_HEADS * HEAD_DIM), kvmap),
                  pl.BlockSpec((1, 1, 1, QUERY_BLOCK, KEY_CHUNK),
                               lambda b, qb, c: (b, qb, jnp.minimum(c, last_chunk(qb)), 0, 0))],
        out_specs=pl.BlockSpec((QUERY_BLOCK, DSA_HEADS * HEAD_DIM), qmap),
        out_shape=jax.ShapeDtypeStruct((batch * seq, DSA_HEADS * HEAD_DIM), BF16),
        scratch_shapes=[pltpu.VMEM((DSA_KV_HEADS, rows, HEAD_DIM), BF16),
                        pltpu.VMEM((DSA_KV_HEADS, rows, 1), F32),
                        pltpu.VMEM((DSA_KV_HEADS, rows, 1), F32),
                        pltpu.VMEM((DSA_KV_HEADS, rows, HEAD_DIM), F32)],
        compiler_params=_params(3),
        name="dsa_attention",
    )(q, k, v, bias)


def kernel(x, positions, l0_attn_w_in, l0_attn_w_out, l1_attn_w_in, l1_k_idx_gain, l1_k_idx_bias,
           l1_attn_w_out, l0_ln_mix_gain, l0_ln_mix_bias, l0_ffn_gate, l0_ffn_up, l0_ffn_down,
           l0_ln_ffn_gain, l0_ln_ffn_bias, l1_ln_mix_gain, l1_ln_mix_bias, l1_ffn_gate, l1_ffn_up,
           l1_ffn_down, l1_ln_ffn_gain, l1_ln_ffn_bias):
    batch, seq, d_model = x.shape
    m = batch * seq
    cosf, sinf = _rope_tables(positions)
    rope = (cosf, sinf)
    xf = x.reshape(m, d_model)
    xb = xf.astype(BF16)

    a_cols = len(DIL_GROUPS) * DIL_COLS
    q0 = _matmul(xb, l0_attn_w_in, 0, a_cols, BF16, tn=512, rope=rope, name="l0_q_proj")
    k0 = _matmul(xb, l0_attn_w_in, a_cols, a_cols, BF16, tn=512, rope=rope, name="l0_k_proj")
    v0 = _matmul(xb, l0_attn_w_in, 2 * a_cols, a_cols, BF16, tn=512, name="l0_v_proj")
    outs, lses = [], []
    for g, (_, dilation) in enumerate(DIL_GROUPS):
        o, lse = _dilated_attention(q0, k0, v0, g, dilation, batch, seq)
        outs.append(o)
        lses.append(lse)
    o0 = _dil_merge(outs, lses)
    y = _matmul(o0, l0_attn_w_out, 0, d_model, F32, tn=512, name="l0_out_proj")
    xf, xb = _res_ln(xf, y, l0_ln_mix_gain, l0_ln_mix_bias)
    h = _ffn_up(xb, l0_ffn_gate, l0_ffn_up)
    y = _ffn_down(h, l0_ffn_down.astype(BF16))
    xf, xb = _res_ln(xf, y, l0_ln_ffn_gain, l0_ln_ffn_bias)

    b_q = DSA_HEADS * HEAD_DIM
    b_kv = DSA_KV_HEADS * HEAD_DIM
    b_qi = IDX_HEADS * HEAD_DIM
    o_v = b_q + b_kv
    o_qi = o_v + b_kv
    o_ki = o_qi + b_qi
    q1 = _matmul(xb, l1_attn_w_in, 0, b_q, BF16, tn=512, rope=rope, name="l1_q_proj")
    k1 = _matmul(xb, l1_attn_w_in, b_q, b_kv, BF16, tn=512, rope=rope, name="l1_k_proj")
    v1 = _matmul(xb, l1_attn_w_in, o_v, b_kv, BF16, tn=512, name="l1_v_proj")
    qi = _matmul(xb, l1_attn_w_in, o_qi, b_qi, BF16, tn=512, rope=rope, name="l1_qidx_proj")
    w_kw = jnp.pad(l1_attn_w_in[:, o_ki:], ((0, 0), (0, 2 * HEAD_DIM - (HEAD_DIM + IDX_HEADS))))
    ki, wi = _kiwi(xb, w_kw, cosf, sinf, l1_k_idx_gain, l1_k_idx_bias)
    bias = _dsa_select(qi, wi, ki, batch, seq)
    o1 = _dsa_attention(q1, k1, v1, bias, batch, seq)
    y = _matmul(o1, l1_attn_w_out, 0, d_model, F32, tn=512, name="l1_out_proj")
    xf, xb = _res_ln(xf, y, l1_ln_mix_gain, l1_ln_mix_bias)
    h = _ffn_up(xb, l1_ffn_gate, l1_ffn_up)
    y = _ffn_down(h, l1_ffn_down.astype(BF16))
    xf, _ = _res_ln(xf, y, l1_ln_ffn_gain, l1_ln_ffn_bias)
    return xf.reshape(batch, seq, d_model)
```

```python
import functools
import math

import numpy as np
import jax
import jax.numpy as jnp
from jax import lax
from jax.experimental import pallas as pl
from jax.experimental.pallas import tpu as pltpu

F32 = jnp.float32
BF16 = jnp.bfloat16
I32 = jnp.int32

HEAD_DIM = 128
HALF = HEAD_DIM // 2
ROPE_THETA = 10000.0
LN_EPS = 1e-5
DEPTH = 2
DEEPNORM_ALPHA = (2 * DEPTH) ** 0.25
ATTN_SCALE = HEAD_DIM ** -0.5
LOG2E = math.log2(math.e)

DIL_GROUPS = ((128, 1), (512, 4), (2048, 16))
BAND = 128
DIL_HEADS = 16
DIL_COLS = DIL_HEADS * HEAD_DIM

DSA_HEADS = 32
DSA_KV_HEADS = 8
DSA_GROUP = DSA_HEADS // DSA_KV_HEADS
IDX_HEADS = 32
IDX_HEADS_PER_DOT = 4
TOPK = 256
QUERY_BLOCK = 128
KEY_CHUNK = 512

INT_MIN = np.int32(-2 ** 31)
MASK_NEG = -1e30
M_INIT = -1e29

VMEM_LIMIT = 56 * 1024 * 1024


def _params(n_axes):
    return pltpu.CompilerParams(dimension_semantics=("arbitrary",) * n_axes,
                                vmem_limit_bytes=VMEM_LIMIT)


def _dot_nt(a, b):
    return lax.dot_general(a, b, (((1,), (1,)), ((), ())), preferred_element_type=F32)


def _rope(a, c, s):
    return a * c + pltpu.roll(a, HALF, 1) * s


def _to_residue_major(t, d, batch, seq):
    if d == 1:
        return t
    c = t.shape[1]
    t = t.reshape(batch, seq // (BAND * d), BAND, d, c)
    return t.transpose(0, 1, 3, 2, 4).reshape(batch * seq, c)


def _from_residue_major(t, d, batch, seq):
    if d == 1:
        return t
    c = t.shape[1]
    t = t.reshape(batch, seq // (BAND * d), d, BAND, c)
    return t.transpose(0, 1, 3, 2, 4).reshape(batch * seq, c)


def _rope_table_kernel(pos_ref, inv_ref, cos_ref, sin_ref):
    ang = pos_ref[...].astype(F32) * inv_ref[...]
    lane = lax.broadcasted_iota(I32, ang.shape, 1)
    cos_ref[...] = jnp.cos(ang)
    sin_ref[...] = jnp.where(lane < HALF, -jnp.sin(ang), jnp.sin(ang))


def _rope_tables(positions):
    m = positions.size
    tm = min(m, 1024)
    inv = ROPE_THETA ** (-jnp.arange(0, HEAD_DIM, 2, dtype=F32) / HEAD_DIM)
    inv2 = jnp.concatenate([inv, inv]).reshape(1, HEAD_DIM)
    return pl.pallas_call(
        _rope_table_kernel,
        grid=(m // tm,),
        in_specs=[pl.BlockSpec((tm, 1), lambda i: (i, 0)),
                  pl.BlockSpec((1, HEAD_DIM), lambda i: (0, 0))],
        out_specs=[pl.BlockSpec((tm, HEAD_DIM), lambda i: (i, 0))] * 2,
        out_shape=[jax.ShapeDtypeStruct((m, HEAD_DIM), F32)] * 2,
        compiler_params=_params(1),
        name="rope_tables",
    )(positions.reshape(m, 1), inv2)


def _mm_kernel(x_ref, w_ref, *rest, rope_blocks, scale):
    if rope_blocks:
        cos_ref, sin_ref, o_ref, wbf_ref = rest
    else:
        o_ref, wbf_ref = rest

    @pl.when(pl.program_id(1) == 0)
    def _():
        wbf_ref[...] = w_ref[...].astype(BF16)

    acc = jnp.dot(x_ref[...], wbf_ref[...], preferred_element_type=F32)

    def store_plain():
        o_ref[...] = acc.astype(o_ref.dtype)

    def store_rope():
        c = cos_ref[...] * scale if scale != 1.0 else cos_ref[...]
        s = sin_ref[...] * scale if scale != 1.0 else sin_ref[...]
        for j in range(acc.shape[1] // HEAD_DIM):
            sl = slice(j * HEAD_DIM, (j + 1) * HEAD_DIM)
            o_ref[:, sl] = _rope(acc[:, sl], c, s).astype(o_ref.dtype)

    if not rope_blocks:
        store_plain()
    else:
        pl.when(pl.program_id(0) < rope_blocks)(store_rope)
        pl.when(pl.program_id(0) >= rope_blocks)(store_plain)


def _matmul(x, w, col_block, n_cols, out_dtype, *, tn, rope=None, rope_cols=None, scale=1.0, name):
    m, k = x.shape
    tm = min(m, 1024)
    assert m % tm == 0 and n_cols % tn == 0
    n_blocks = n_cols // tn
    rope_blocks = 0
    in_specs = [pl.BlockSpec((tm, k), lambda n, i: (i, 0)),
                pl.BlockSpec((k, tn), lambda n, i: (0, col_block(n)))]
    args = [x, w]
    if rope is not None:
        rope_blocks = n_blocks if rope_cols is None else rope_cols // tn
        in_specs += [pl.BlockSpec((tm, HEAD_DIM), lambda n, i: (i, 0))] * 2
        args += list(rope)
    return pl.pallas_call(
        functools.partial(_mm_kernel, rope_blocks=rope_blocks, scale=scale),
        grid=(n_blocks, m // tm),
        in_specs=in_specs,
        out_specs=pl.BlockSpec((tm, tn), lambda n, i: (i, n)),
        out_shape=jax.ShapeDtypeStruct((m, n_cols), out_dtype),
        scratch_shapes=[pltpu.VMEM((k, tn), BF16)],
        compiler_params=_params(2),
        name=name,
    )(*args)


def _kiwi_kernel(x_ref, wk_ref, wwt_ref, cos_ref, sin_ref, g_ref, b_ref, ki_ref, wit_ref, wkbf_ref, wwbf_ref):
    @pl.when(pl.program_id(0) == 0)
    def _():
        wkbf_ref[...] = wk_ref[...].astype(BF16)
        wwbf_ref[...] = wwt_ref[...].astype(BF16)

    x = x_ref[...]
    a = jnp.dot(x, wkbf_ref[...], preferred_element_type=F32)
    mu = jnp.mean(a, axis=-1, keepdims=True)
    var = jnp.mean(jnp.square(a - mu), axis=-1, keepdims=True)
    y = (a - mu) * lax.rsqrt(var + LN_EPS) * g_ref[...] + b_ref[...]
    ki_ref[...] = _rope(y, cos_ref[...], sin_ref[...]).astype(ki_ref.dtype)
    wit_ref[...] = _dot_nt(wwbf_ref[...], x) * (IDX_HEADS ** -0.5 * HEAD_DIM ** -0.5)


def _kiwi(x, w_ki, w_wi_t, cosf, sinf, gain, bias):
    m, k = x.shape
    tm = min(m, 1024)
    row = lambda i: (i, 0)
    fixed = lambda i: (0, 0)
    return pl.pallas_call(
        _kiwi_kernel,
        grid=(m // tm,),
        in_specs=[pl.BlockSpec((tm, k), row), pl.BlockSpec((k, HEAD_DIM), fixed),
                  pl.BlockSpec((HEAD_DIM, k), fixed),
                  pl.BlockSpec((tm, HEAD_DIM), row), pl.BlockSpec((tm, HEAD_DIM), row),
                  pl.BlockSpec((1, HEAD_DIM), fixed), pl.BlockSpec((1, HEAD_DIM), fixed)],
        out_specs=[pl.BlockSpec((tm, HEAD_DIM), row), pl.BlockSpec((HEAD_DIM, tm), lambda i: (0, i))],
        out_shape=[jax.ShapeDtypeStruct((m, HEAD_DIM), BF16),
                   jax.ShapeDtypeStruct((HEAD_DIM, m), F32)],
        scratch_shapes=[pltpu.VMEM((k, HEAD_DIM), BF16), pltpu.VMEM((HEAD_DIM, k), BF16)],
        compiler_params=_params(1),
        name="dsa_kiwi_proj",
    )(x, w_ki, w_wi_t, cosf, sinf, gain.reshape(1, HEAD_DIM), bias.reshape(1, HEAD_DIM))


def _ffn_up_kernel(x_ref, wg_ref, wu_ref, h_ref, wgbf_ref, wubf_ref):
    @pl.when(pl.program_id(1) == 0)
    def _():
        wgbf_ref[...] = wg_ref[...].astype(BF16)
        wubf_ref[...] = wu_ref[...].astype(BF16)

    x = x_ref[...]
    g = jnp.dot(x, wgbf_ref[...], preferred_element_type=F32)
    u = jnp.dot(x, wubf_ref[...], preferred_element_type=F32)
    h_ref[...] = (g * (1.0 / (1.0 + jnp.exp(-g))) * u).astype(h_ref.dtype)


def _ffn_up(x, w_gate, w_up, *, tn=256):
    m, k = x.shape
    n = w_gate.shape[1]
    tm = min(m, 1024)
    assert n % tn == 0
    wspec = pl.BlockSpec((k, tn), lambda j, i: (0, j))
    return pl.pallas_call(
        _ffn_up_kernel,
        grid=(n // tn, m // tm),
        in_specs=[pl.BlockSpec((tm, k), lambda j, i: (i, 0)), wspec, wspec],
        out_specs=pl.BlockSpec((tm, tn), lambda j, i: (i, j)),
        out_shape=jax.ShapeDtypeStruct((m, n), BF16),
        scratch_shapes=[pltpu.VMEM((k, tn), BF16)] * 2,
        compiler_params=_params(2),
        name="ffn_up",
    )(x, w_gate, w_up)


def _mm_acc_kernel(h_ref, w_ref, o_ref):
    @pl.when(pl.program_id(1) == 0)
    def _():
        o_ref[...] = jnp.zeros_like(o_ref)

    o_ref[...] += jnp.dot(h_ref[...], w_ref[...], preferred_element_type=F32)


def _ffn_down(h, w_down_bf, *, tk=256):
    m, k = h.shape
    n = w_down_bf.shape[1]
    tm = min(m, 1024)
    assert k % tk == 0
    return pl.pallas_call(
        _mm_acc_kernel,
        grid=(m // tm, k // tk),
        in_specs=[pl.BlockSpec((tm, tk), lambda i, j: (i, j)),
                  pl.BlockSpec((tk, n), lambda i, j: (j, 0))],
        out_specs=pl.BlockSpec((tm, n), lambda i, j: (i, 0)),
        out_shape=jax.ShapeDtypeStruct((m, n), F32),
        compiler_params=_params(2),
        name="ffn_down",
    )(h, w_down_bf)


def _res_ln_kernel(x_ref, y_ref, g_ref, b_ref, o_ref, obf_ref):
    z = DEEPNORM_ALPHA * x_ref[...] + y_ref[...]
    mu = jnp.mean(z, axis=-1, keepdims=True)
    d = z - mu
    var = jnp.mean(jnp.square(d), axis=-1, keepdims=True)
    out = d * lax.rsqrt(var + LN_EPS) * g_ref[...] + b_ref[...]
    o_ref[...] = out
    obf_ref[...] = out.astype(BF16)


def _res_ln(x, y, gain, bias):
    m, d = x.shape
    tm = min(m, 256)
    row = lambda i: (i, 0)
    fixed = lambda i: (0, 0)
    return pl.pallas_call(
        _res_ln_kernel,
        grid=(m // tm,),
        in_specs=[pl.BlockSpec((tm, d), row), pl.BlockSpec((tm, d), row),
                  pl.BlockSpec((1, d), fixed), pl.BlockSpec((1, d), fixed)],
        out_specs=[pl.BlockSpec((tm, d), row)] * 2,
        out_shape=[jax.ShapeDtypeStruct((m, d), F32), jax.ShapeDtypeStruct((m, d), BF16)],
        compiler_params=_params(1),
        name="residual_layernorm",
    )(x, y, gain.reshape(1, d), bias.reshape(1, d))


def _dil_attn_kernel(q_ref, kp_ref, kc_ref, vp_ref, vc_ref, o_ref, lse_ref):
    has_prev = pl.program_id(1) > 0
    qi = lax.broadcasted_iota(I32, (BAND, BAND), 0)
    kj = lax.broadcasted_iota(I32, (BAND, BAND), 1)
    mask_p = jnp.logical_and(kj >= qi, has_prev)
    mask_c = kj <= qi
    lse_ref[...] = jnp.zeros_like(lse_ref)
    for h in range(DIL_HEADS):
        sl = slice(h * HEAD_DIM, (h + 1) * HEAD_DIM)
        q = q_ref[:, sl]
        sp = jnp.where(mask_p, _dot_nt(q, kp_ref[:, sl]) * ATTN_SCALE, -jnp.inf)
        sc = jnp.where(mask_c, _dot_nt(q, kc_ref[:, sl]) * ATTN_SCALE, -jnp.inf)
        m = jnp.maximum(jnp.max(sp, axis=1, keepdims=True), jnp.max(sc, axis=1, keepdims=True))
        pp = jnp.exp(sp - m)
        pc = jnp.exp(sc - m)
        l = jnp.sum(pp, axis=1, keepdims=True) + jnp.sum(pc, axis=1, keepdims=True)
        o = (jnp.dot(pp.astype(BF16), vp_ref[:, sl], preferred_element_type=F32)
             + jnp.dot(pc.astype(BF16), vc_ref[:, sl], preferred_element_type=F32))
        o_ref[:, sl] = (o / l).astype(o_ref.dtype)
        lse_ref[:, h:h + 1] = m + jnp.log(l)


def _dilated_attention(qkv, dilation, batch, seq):
    d = dilation
    n_chunks = seq // (BAND * d)
    cur = lambda col: (lambda b, ch, r: ((b * n_chunks + ch) * d + r, col))
    prev = lambda col: (lambda b, ch, r: ((b * n_chunks + jnp.maximum(ch - 1, 0)) * d + r, col))
    blk = (BAND, DIL_COLS)
    return pl.pallas_call(
        _dil_attn_kernel,
        grid=(batch, n_chunks, d),
        in_specs=[pl.BlockSpec(blk, cur(0)), pl.BlockSpec(blk, prev(1)), pl.BlockSpec(blk, cur(1)),
                  pl.BlockSpec(blk, prev(2)), pl.BlockSpec(blk, cur(2))],
        out_specs=[pl.BlockSpec(blk, cur(0)), pl.BlockSpec((BAND, HEAD_DIM), cur(0))],
        out_shape=[jax.ShapeDtypeStruct((batch * seq, DIL_COLS), BF16),
                   jax.ShapeDtypeStruct((batch * seq, HEAD_DIM), F32)],
        compiler_params=_params(3),
        name=f"dilated_attention_d{d}",
    )(qkv, qkv, qkv, qkv, qkv)


def _dil_merge_kernel(o0_ref, o1_ref, o2_ref, l0_ref, l1_ref, l2_ref, out_ref):
    l0, l1, l2 = l0_ref[...], l1_ref[...], l2_ref[...]
    mx = jnp.maximum(jnp.maximum(l0, l1), l2)
    e0, e1, e2 = jnp.exp(l0 - mx), jnp.exp(l1 - mx), jnp.exp(l2 - mx)
    inv = 1.0 / (e0 + e1 + e2)
    w0, w1, w2 = e0 * inv, e1 * inv, e2 * inv
    for h in range(DIL_HEADS):
        sl = slice(h * HEAD_DIM, (h + 1) * HEAD_DIM)
        col = slice(h, h + 1)
        acc = (w0[:, col] * o0_ref[:, sl].astype(F32) + w1[:, col] * o1_ref[:, sl].astype(F32)
               + w2[:, col] * o2_ref[:, sl].astype(F32))
        out_ref[:, sl] = acc.astype(out_ref.dtype)


def _dil_merge(outs, lses):
    m = outs[0].shape[0]
    tm = min(m, 512)
    row = lambda i: (i, 0)
    return pl.pallas_call(
        _dil_merge_kernel,
        grid=(m // tm,),
        in_specs=[pl.BlockSpec((tm, DIL_COLS), row)] * 3 + [pl.BlockSpec((tm, HEAD_DIM), row)] * 3,
        out_specs=pl.BlockSpec((tm, DIL_COLS), row),
        out_shape=jax.ShapeDtypeStruct((m, DIL_COLS), BF16),
        compiler_params=_params(1),
        name="dilated_merge",
    )(*outs, *lses)


def _sortable(x):
    b = lax.bitcast_convert_type(x, I32)
    return b ^ ((b >> 31) & np.int32(0x7FFFFFFF))


def _dsa_select_kernel(qi_ref, wit_ref, ki_ref, bias_ref, qs_ref, keys_ref, *, n_chunks):
    qb = pl.program_id(1)
    n_c = (qb * QUERY_BLOCK + QUERY_BLOCK - 1) // KEY_CHUNK + 1
    for h in range(IDX_HEADS):
        qs_ref[h * QUERY_BLOCK:(h + 1) * QUERY_BLOCK, :] = qi_ref[:, h * HEAD_DIM:(h + 1) * HEAD_DIM]
    w = wit_ref[...]
    kpos = lax.broadcasted_iota(I32, (KEY_CHUNK, QUERY_BLOCK), 0)
    t = qb * QUERY_BLOCK + lax.broadcasted_iota(I32, (KEY_CHUNK, QUERY_BLOCK), 1)
    rows_per_dot = IDX_HEADS_PER_DOT * QUERY_BLOCK

    def score_chunk(c, carry):
        kic = ki_ref[0, pl.ds(pl.multiple_of(c * KEY_CHUNK, KEY_CHUNK), KEY_CHUNK), :]
        sc = jnp.zeros((KEY_CHUNK, QUERY_BLOCK), F32)
        for hg in range(IDX_HEADS // IDX_HEADS_PER_DOT):
            r = _dot_nt(kic, qs_ref[hg * rows_per_dot:(hg + 1) * rows_per_dot, :])
            for j in range(IDX_HEADS_PER_DOT):
                h = hg * IDX_HEADS_PER_DOT + j
                sc = sc + jnp.maximum(r[:, j * QUERY_BLOCK:(j + 1) * QUERY_BLOCK], 0.0) * w[h:h + 1, :]
        causal = c * KEY_CHUNK + kpos <= t
        keys_ref[c] = jnp.where(causal, _sortable(sc), INT_MIN)
        return carry

    lax.fori_loop(0, n_c, score_chunk, 0)

    def count_ge(cand):
        def body(c, acc):
            x = jnp.where(keys_ref[c] >= cand, 1, 0).astype(I32)
            return acc + jnp.sum(x, axis=0, keepdims=True)
        return lax.fori_loop(0, n_c, body, jnp.zeros((1, QUERY_BLOCK), I32))

    zero = jnp.zeros((1, QUERY_BLOCK), I32)
    thr = jnp.where(count_ge(zero) >= TOPK, zero, jnp.full((1, QUERY_BLOCK), INT_MIN, I32))

    def bit_step(i, thr):
        cand = thr | jnp.left_shift(jnp.int32(1), 30 - i)
        return jnp.where(count_ge(cand) >= TOPK, cand, thr)

    thr = lax.fori_loop(0, 31, bit_step, thr)
    thr = jnp.maximum(thr, INT_MIN + 1)

    def write_bias(c, carry):
        bias_ref[0, 0, c] = jnp.where(keys_ref[c] >= thr, 0.0, MASK_NEG).astype(bias_ref.dtype)
        return carry

    def write_masked(c, carry):
        bias_ref[0, 0, c] = jnp.full((KEY_CHUNK, QUERY_BLOCK), MASK_NEG, bias_ref.dtype)
        return carry

    lax.fori_loop(0, n_c, write_bias, 0)
    lax.fori_loop(n_c, n_chunks, write_masked, 0)


def _dsa_select(qi, wit, ki, batch, seq):
    nqb = seq // QUERY_BLOCK
    nkc = seq // KEY_CHUNK
    return pl.pallas_call(
        functools.partial(_dsa_select_kernel, n_chunks=nkc),
        grid=(batch, nqb),
        in_specs=[pl.BlockSpec((QUERY_BLOCK, IDX_HEADS * HEAD_DIM), lambda b, q: (b * nqb + q, 0)),
                  pl.BlockSpec((HEAD_DIM, QUERY_BLOCK), lambda b, q: (0, b * nqb + q)),
                  pl.BlockSpec((1, seq, HEAD_DIM), lambda b, q: (b, 0, 0))],
        out_specs=pl.BlockSpec((1, 1, nkc, KEY_CHUNK, QUERY_BLOCK), lambda b, q: (b, q, 0, 0, 0)),
        out_shape=jax.ShapeDtypeStruct((batch, nqb, nkc, KEY_CHUNK, QUERY_BLOCK), BF16),
        scratch_shapes=[pltpu.VMEM((IDX_HEADS * QUERY_BLOCK, HEAD_DIM), BF16),
                        pltpu.VMEM((nkc, KEY_CHUNK, QUERY_BLOCK), I32)],
        compiler_params=_params(2),
        name="dsa_select",
    )(qi, wit, ki.reshape(batch, seq, HEAD_DIM))


def _dsa_attn_kernel(qb_tab, c_tab, q_ref, k_ref, v_ref, bias_ref, o_ref, qa_ref, m_ref, l_ref, acc_ref):
    step = pl.program_id(1)
    qb = qb_tab[step]
    c = c_tab[step]
    last = (qb * QUERY_BLOCK + QUERY_BLOCK - 1) // KEY_CHUNK
    n_tiles = KEY_CHUNK // HEAD_DIM

    @pl.when(c == 0)
    def _():
        ri = lax.broadcasted_iota(I32, (QUERY_BLOCK, HEAD_DIM), 0)
        ci = lax.broadcasted_iota(I32, (QUERY_BLOCK, HEAD_DIM), 1)
        eye = jnp.where(ri == ci, 1.0, 0.0).astype(BF16)
        for g in range(DSA_KV_HEADS):
            for j in range(DSA_GROUP):
                h = g * DSA_GROUP + j
                rows = slice(j * QUERY_BLOCK, (j + 1) * QUERY_BLOCK)
                qa_ref[g, rows, :HEAD_DIM] = q_ref[:, h * HEAD_DIM:(h + 1) * HEAD_DIM]
                qa_ref[g, rows, HEAD_DIM:] = eye
        m_ref[...] = jnp.full(m_ref.shape, M_INIT, F32)
        l_ref[...] = jnp.zeros_like(l_ref)
        acc_ref[...] = jnp.zeros_like(acc_ref)

    bias_t = bias_ref[0, 0, 0]
    for g in range(DSA_KV_HEADS):
        sl = slice(g * HEAD_DIM, (g + 1) * HEAD_DIM)
        ka = jnp.concatenate([k_ref[:, sl], bias_t], axis=1)
        s = _dot_nt(qa_ref[g], ka)
        tiles = [s[:, j * HEAD_DIM:(j + 1) * HEAD_DIM] for j in range(n_tiles)]
        m_old = m_ref[g]
        m_new = jnp.maximum(m_old, jnp.max(functools.reduce(jnp.maximum, tiles), axis=1, keepdims=True))
        alpha = jnp.exp2(m_old - m_new)
        ps = [jnp.exp2(tl - m_new) for tl in tiles]
        l_ref[g] = alpha * l_ref[g] + jnp.sum(functools.reduce(jnp.add, ps), axis=1, keepdims=True)
        p = jnp.concatenate([tl.astype(BF16) for tl in ps], axis=1)
        acc_ref[g] = alpha * acc_ref[g] + jnp.dot(p, v_ref[:, sl], preferred_element_type=F32)
        m_ref[g] = m_new

    @pl.when(c == last)
    def _():
        for g in range(DSA_KV_HEADS):
            o = acc_ref[g] / l_ref[g]
            for j in range(DSA_GROUP):
                h = g * DSA_GROUP + j
                o_ref[:, h * HEAD_DIM:(h + 1) * HEAD_DIM] = (
                    o[j * QUERY_BLOCK:(j + 1) * QUERY_BLOCK].astype(o_ref.dtype))


def _dsa_attention(q, k, v, bias_t, batch, seq):
    nqb = seq // QUERY_BLOCK
    nkc = seq // KEY_CHUNK
    rows = DSA_GROUP * QUERY_BLOCK
    steps = [(qb, c) for qb in range(nqb)
             for c in range((qb * QUERY_BLOCK + QUERY_BLOCK - 1) // KEY_CHUNK + 1)]
    qb_tab = jnp.asarray(np.array([s[0] for s in steps], np.int32))
    c_tab = jnp.asarray(np.array([s[1] for s in steps], np.int32))
    qmap = lambda b, s, qt, ct: (b * nqb + qt[s], 0)
    kvmap = lambda b, s, qt, ct: (b * nkc + ct[s], 0)
    grid_spec = pltpu.PrefetchScalarGridSpec(
        num_scalar_prefetch=2,
        grid=(batch, len(steps)),
        in_specs=[pl.BlockSpec((QUERY_BLOCK, DSA_HEADS * HEAD_DIM), qmap),
                  pl.BlockSpec((KEY_CHUNK, DSA_KV_HEADS * HEAD_DIM), kvmap),
                  pl.BlockSpec((KEY_CHUNK, DSA_KV_HEADS * HEAD_DIM), kvmap),
                  pl.BlockSpec((1, 1, 1, KEY_CHUNK, QUERY_BLOCK),
                               lambda b, s, qt, ct: (b, qt[s], ct[s], 0, 0))],
        out_specs=pl.BlockSpec((QUERY_BLOCK, DSA_HEADS * HEAD_DIM), qmap),
        scratch_shapes=[pltpu.VMEM((DSA_KV_HEADS, rows, 2 * HEAD_DIM), BF16),
                        pltpu.VMEM((DSA_KV_HEADS, rows, HEAD_DIM), F32),
                        pltpu.VMEM((DSA_KV_HEADS, rows, HEAD_DIM), F32),
                        pltpu.VMEM((DSA_KV_HEADS, rows, HEAD_DIM), F32)])
    return pl.pallas_call(
        _dsa_attn_kernel,
        grid_spec=grid_spec,
        out_shape=jax.ShapeDtypeStruct((batch * seq, DSA_HEADS * HEAD_DIM), BF16),
        compiler_params=_params(2),
        name="dsa_attention",
    )(qb_tab, c_tab, q, k, v, bias_t)


def kernel(x, positions, l0_attn_w_in, l0_attn_w_out, l1_attn_w_in, l1_k_idx_gain, l1_k_idx_bias,
           l1_attn_w_out, l0_ln_mix_gain, l0_ln_mix_bias, l0_ffn_gate, l0_ffn_up, l0_ffn_down,
           l0_ln_ffn_gain, l0_ln_ffn_bias, l1_ln_mix_gain, l1_ln_mix_bias, l1_ffn_gate, l1_ffn_up,
           l1_ffn_down, l1_ln_ffn_gain, l1_ln_ffn_bias):
    batch, seq, d_model = x.shape
    m = batch * seq
    cosf, sinf = _rope_tables(positions)
    rope = (cosf, sinf)
    xf = x.reshape(m, d_model)
    xb = xf.astype(BF16)

    tn = 512
    n_grp = len(DIL_GROUPS)
    grp_blocks = DIL_COLS // tn
    outs, lses = [], []
    for g, (_, d) in enumerate(DIL_GROUPS):
        col_block = lambda n, g=g: (n // grp_blocks) * (n_grp * grp_blocks) + g * grp_blocks + n % grp_blocks
        rope_g = tuple(_to_residue_major(t, d, batch, seq) for t in rope)
        qkv = _matmul(_to_residue_major(xb, d, batch, seq), l0_attn_w_in, col_block, 3 * DIL_COLS, BF16,
                      tn=tn, rope=rope_g, rope_cols=2 * DIL_COLS, name=f"l0_qkv_proj_d{d}")
        o, lse = _dilated_attention(qkv, d, batch, seq)
        outs.append(_from_residue_major(o, d, batch, seq))
        lses.append(_from_residue_major(lse, d, batch, seq))
    o0 = _dil_merge(outs, lses)
    y = _matmul(o0, l0_attn_w_out, lambda n: n, d_model, F32, tn=tn, name="l0_out_proj")
    xf, xb = _res_ln(xf, y, l0_ln_mix_gain, l0_ln_mix_bias)
    h = _ffn_up(xb, l0_ffn_gate, l0_ffn_up)
    y = _ffn_down(h, l0_ffn_down.astype(BF16))
    xf, xb = _res_ln(xf, y, l0_ln_ffn_gain, l0_ln_ffn_bias)

    b_q = DSA_HEADS * HEAD_DIM
    b_kv = DSA_KV_HEADS * HEAD_DIM
    b_qi = IDX_HEADS * HEAD_DIM
    o_v = b_q + b_kv
    o_qi = o_v + b_kv
    o_ki = o_qi + b_qi
    o_wi = o_ki + HEAD_DIM
    at = lambda col0: (lambda n: n + col0 // tn)
    q1 = _matmul(xb, l1_attn_w_in, at(0), b_q, BF16, tn=tn, rope=rope, scale=ATTN_SCALE * LOG2E,
                 name="l1_q_proj")
    k1 = _matmul(xb, l1_attn_w_in, at(b_q), b_kv, BF16, tn=tn, rope=rope, name="l1_k_proj")
    v1 = _matmul(xb, l1_attn_w_in, at(o_v), b_kv, BF16, tn=tn, name="l1_v_proj")
    qi = _matmul(xb, l1_attn_w_in, at(o_qi), b_qi, BF16, tn=tn, rope=rope, name="l1_qidx_proj")
    w_ki = l1_attn_w_in[:, o_ki:o_wi]
    w_wi_t = jnp.pad(l1_attn_w_in[:, o_wi:].T, ((0, HEAD_DIM - IDX_HEADS), (0, 0)))
    ki, wit = _kiwi(xb, w_ki, w_wi_t, cosf, sinf, l1_k_idx_gain, l1_k_idx_bias)
    bias_t = _dsa_select(qi, wit, ki, batch, seq)
    o1 = _dsa_attention(q1, k1, v1, bias_t, batch, seq)
    y = _matmul(o1, l1_attn_w_out, lambda n: n, d_model, F32, tn=tn, name="l1_out_proj")
    xf, xb = _res_ln(xf, y, l1_ln_mix_gain, l1_ln_mix_bias)
    h = _ffn_up(xb, l1_ffn_gate, l1_ffn_up)
    y = _ffn_down(h, l1_ffn_down.astype(BF16))
    xf, _ = _res_ln(xf, y, l1_ln_ffn_gain, l1_ln_ffn_bias)
    return xf.reshape(batch, seq, d_model)
```

```python
import functools
import math

import numpy as np
import jax
import jax.numpy as jnp
from jax import lax
from jax.experimental import pallas as pl
from jax.experimental.pallas import tpu as pltpu

F32 = jnp.float32
BF16 = jnp.bfloat16
I32 = jnp.int32

HEAD_DIM = 128
HALF = HEAD_DIM // 2
ROPE_THETA = 10000.0
LN_EPS = 1e-5
DEPTH = 2
DEEPNORM_ALPHA = (2 * DEPTH) ** 0.25
ATTN_SCALE = HEAD_DIM ** -0.5
LOG2E = math.log2(math.e)

DIL_GROUPS = ((128, 1), (512, 4), (2048, 16))
BAND = 128
DIL_HEADS = 16
DIL_COLS = DIL_HEADS * HEAD_DIM

DSA_HEADS = 32
DSA_KV_HEADS = 8
DSA_GROUP = DSA_HEADS // DSA_KV_HEADS
IDX_HEADS = 32
IDX_HEADS_PER_DOT = 4
TOPK = 256
QUERY_BLOCK = 128
KEY_CHUNK = 512

INT_MIN = np.int32(-2 ** 31)
MASK_NEG = -1e30
M_INIT = -1e29

VMEM_LIMIT = 56 * 1024 * 1024


def _params(n_axes):
    return pltpu.CompilerParams(dimension_semantics=("arbitrary",) * n_axes,
                                vmem_limit_bytes=VMEM_LIMIT)


def _dot_nt(a, b):
    return lax.dot_general(a, b, (((1,), (1,)), ((), ())), preferred_element_type=F32)


def _rope(a, c, s):
    return a * c + pltpu.roll(a, HALF, 1) * s


def _to_residue_major(t, d, batch, seq):
    if d == 1:
        return t
    c = t.shape[1]
    t = t.reshape(batch, seq // (BAND * d), BAND, d, c)
    return t.transpose(0, 1, 3, 2, 4).reshape(batch * seq, c)


def _from_residue_major(t, d, batch, seq):
    if d == 1:
        return t
    c = t.shape[1]
    t = t.reshape(batch, seq // (BAND * d), d, BAND, c)
    return t.transpose(0, 1, 3, 2, 4).reshape(batch * seq, c)


def _rope_table_kernel(pos_ref, inv_ref, cos_ref, sin_ref):
    ang = pos_ref[...].astype(F32) * inv_ref[...]
    lane = lax.broadcasted_iota(I32, ang.shape, 1)
    cos_ref[...] = jnp.cos(ang)
    sin_ref[...] = jnp.where(lane < HALF, -jnp.sin(ang), jnp.sin(ang))


def _rope_tables(positions):
    m = positions.size
    tm = min(m, 1024)
    inv = ROPE_THETA ** (-jnp.arange(0, HEAD_DIM, 2, dtype=F32) / HEAD_DIM)
    inv2 = jnp.concatenate([inv, inv]).reshape(1, HEAD_DIM)
    return pl.pallas_call(
        _rope_table_kernel,
        grid=(m // tm,),
        in_specs=[pl.BlockSpec((tm, 1), lambda i: (i, 0)),
                  pl.BlockSpec((1, HEAD_DIM), lambda i: (0, 0))],
        out_specs=[pl.BlockSpec((tm, HEAD_DIM), lambda i: (i, 0))] * 2,
        out_shape=[jax.ShapeDtypeStruct((m, HEAD_DIM), F32)] * 2,
        compiler_params=_params(1),
        name="rope_tables",
    )(positions.reshape(m, 1), inv2)


def _mm_kernel(x_ref, w_ref, *rest, scale_blocks, rope_blocks, scale):
    if rope_blocks:
        cos_ref, sin_ref, o_ref, wbf_ref = rest
    else:
        o_ref, wbf_ref = rest

    @pl.when(pl.program_id(1) == 0)
    def _():
        wbf_ref[...] = w_ref[...].astype(BF16)

    acc = jnp.dot(x_ref[...], wbf_ref[...], preferred_element_type=F32)

    def store_plain():
        o_ref[...] = acc.astype(o_ref.dtype)

    def store_rope(factor):
        c = cos_ref[...] * factor if factor != 1.0 else cos_ref[...]
        s = sin_ref[...] * factor if factor != 1.0 else sin_ref[...]
        for j in range(acc.shape[1] // HEAD_DIM):
            sl = slice(j * HEAD_DIM, (j + 1) * HEAD_DIM)
            o_ref[:, sl] = _rope(acc[:, sl], c, s).astype(o_ref.dtype)

    n = pl.program_id(0)
    if not rope_blocks:
        store_plain()
    else:
        if scale_blocks:
            pl.when(n < scale_blocks)(functools.partial(store_rope, scale))
        pl.when(jnp.logical_and(n >= scale_blocks, n < rope_blocks))(functools.partial(store_rope, 1.0))
        pl.when(n >= rope_blocks)(store_plain)


def _matmul(x, w, col_block, n_cols, out_dtype, *, tn, rope=None, rope_cols=None, scale=1.0,
            scale_cols=0, name):
    m, k = x.shape
    tm = min(m, 1024)
    assert m % tm == 0 and n_cols % tn == 0
    n_blocks = n_cols // tn
    rope_blocks = 0
    in_specs = [pl.BlockSpec((tm, k), lambda n, i: (i, 0)),
                pl.BlockSpec((k, tn), lambda n, i: (0, col_block(n)))]
    args = [x, w]
    if rope is not None:
        rope_blocks = n_blocks if rope_cols is None else rope_cols // tn
        in_specs += [pl.BlockSpec((tm, HEAD_DIM), lambda n, i: (i, 0))] * 2
        args += list(rope)
    return pl.pallas_call(
        functools.partial(_mm_kernel, scale_blocks=scale_cols // tn, rope_blocks=rope_blocks, scale=scale),
        grid=(n_blocks, m // tm),
        in_specs=in_specs,
        out_specs=pl.BlockSpec((tm, tn), lambda n, i: (i, n)),
        out_shape=jax.ShapeDtypeStruct((m, n_cols), out_dtype),
        scratch_shapes=[pltpu.VMEM((k, tn), BF16)],
        compiler_params=_params(2),
        name=name,
    )(*args)


def _kiwi_kernel(x_ref, wk_ref, wwt_ref, cos_ref, sin_ref, g_ref, b_ref, ki_ref, wit_ref, wkbf_ref, wwbf_ref):
    @pl.when(pl.program_id(0) == 0)
    def _():
        wkbf_ref[...] = wk_ref[...].astype(BF16)
        wwbf_ref[...] = wwt_ref[...].astype(BF16)

    x = x_ref[...]
    a = jnp.dot(x, wkbf_ref[...], preferred_element_type=F32)
    mu = jnp.mean(a, axis=-1, keepdims=True)
    var = jnp.mean(jnp.square(a - mu), axis=-1, keepdims=True)
    y = (a - mu) * lax.rsqrt(var + LN_EPS) * g_ref[...] + b_ref[...]
    ki_ref[...] = _rope(y, cos_ref[...], sin_ref[...]).astype(ki_ref.dtype)
    wit_ref[...] = _dot_nt(wwbf_ref[...], x) * (IDX_HEADS ** -0.5 * HEAD_DIM ** -0.5)


def _kiwi(x, w_ki, w_wi_t, cosf, sinf, gain, bias):
    m, k = x.shape
    tm = min(m, 1024)
    row = lambda i: (i, 0)
    fixed = lambda i: (0, 0)
    return pl.pallas_call(
        _kiwi_kernel,
        grid=(m // tm,),
        in_specs=[pl.BlockSpec((tm, k), row), pl.BlockSpec((k, HEAD_DIM), fixed),
                  pl.BlockSpec((HEAD_DIM, k), fixed),
                  pl.BlockSpec((tm, HEAD_DIM), row), pl.BlockSpec((tm, HEAD_DIM), row),
                  pl.BlockSpec((1, HEAD_DIM), fixed), pl.BlockSpec((1, HEAD_DIM), fixed)],
        out_specs=[pl.BlockSpec((tm, HEAD_DIM), row), pl.BlockSpec((HEAD_DIM, tm), lambda i: (0, i))],
        out_shape=[jax.ShapeDtypeStruct((m, HEAD_DIM), BF16),
                   jax.ShapeDtypeStruct((HEAD_DIM, m), F32)],
        scratch_shapes=[pltpu.VMEM((k, HEAD_DIM), BF16), pltpu.VMEM((HEAD_DIM, k), BF16)],
        compiler_params=_params(1),
        name="dsa_kiwi_proj",
    )(x, w_ki, w_wi_t, cosf, sinf, gain.reshape(1, HEAD_DIM), bias.reshape(1, HEAD_DIM))


def _ffn_up_kernel(x_ref, wg_ref, wu_ref, h_ref, wgbf_ref, wubf_ref):
    @pl.when(pl.program_id(1) == 0)
    def _():
        wgbf_ref[...] = wg_ref[...].astype(BF16)
        wubf_ref[...] = wu_ref[...].astype(BF16)

    x = x_ref[...]
    g = jnp.dot(x, wgbf_ref[...], preferred_element_type=F32)
    u = jnp.dot(x, wubf_ref[...], preferred_element_type=F32)
    h_ref[...] = (g * (1.0 / (1.0 + jnp.exp(-g))) * u).astype(h_ref.dtype)


def _ffn_up(x, w_gate, w_up, *, tn=256):
    m, k = x.shape
    n = w_gate.shape[1]
    tm = min(m, 1024)
    assert n % tn == 0
    wspec = pl.BlockSpec((k, tn), lambda j, i: (0, j))
    return pl.pallas_call(
        _ffn_up_kernel,
        grid=(n // tn, m // tm),
        in_specs=[pl.BlockSpec((tm, k), lambda j, i: (i, 0)), wspec, wspec],
        out_specs=pl.BlockSpec((tm, tn), lambda j, i: (i, j)),
        out_shape=jax.ShapeDtypeStruct((m, n), BF16),
        scratch_shapes=[pltpu.VMEM((k, tn), BF16)] * 2,
        compiler_params=_params(2),
        name="ffn_up",
    )(x, w_gate, w_up)


def _mm_acc_kernel(h_ref, w_ref, o_ref):
    @pl.when(pl.program_id(1) == 0)
    def _():
        o_ref[...] = jnp.zeros_like(o_ref)

    o_ref[...] += jnp.dot(h_ref[...], w_ref[...], preferred_element_type=F32)


def _ffn_down(h, w_down_bf, *, tk=256):
    m, k = h.shape
    n = w_down_bf.shape[1]
    tm = min(m, 1024)
    assert k % tk == 0
    return pl.pallas_call(
        _mm_acc_kernel,
        grid=(m // tm, k // tk),
        in_specs=[pl.BlockSpec((tm, tk), lambda i, j: (i, j)),
                  pl.BlockSpec((tk, n), lambda i, j: (j, 0))],
        out_specs=pl.BlockSpec((tm, n), lambda i, j: (i, 0)),
        out_shape=jax.ShapeDtypeStruct((m, n), F32),
        compiler_params=_params(2),
        name="ffn_down",
    )(h, w_down_bf)


def _res_ln_kernel(x_ref, y_ref, g_ref, b_ref, o_ref, *maybe_obf_ref):
    z = DEEPNORM_ALPHA * x_ref[...] + y_ref[...]
    mu = jnp.mean(z, axis=-1, keepdims=True)
    d = z - mu
    var = jnp.mean(jnp.square(d), axis=-1, keepdims=True)
    out = d * lax.rsqrt(var + LN_EPS) * g_ref[...] + b_ref[...]
    o_ref[...] = out
    for obf_ref in maybe_obf_ref:
        obf_ref[...] = out.astype(BF16)


def _res_ln(x, y, gain, bias, *, with_bf16=True):
    m, d = x.shape
    tm = min(m, 256)
    row = lambda i: (i, 0)
    fixed = lambda i: (0, 0)
    n_out = 2 if with_bf16 else 1
    return pl.pallas_call(
        _res_ln_kernel,
        grid=(m // tm,),
        in_specs=[pl.BlockSpec((tm, d), row), pl.BlockSpec((tm, d), row),
                  pl.BlockSpec((1, d), fixed), pl.BlockSpec((1, d), fixed)],
        out_specs=[pl.BlockSpec((tm, d), row)] * n_out,
        out_shape=[jax.ShapeDtypeStruct((m, d), F32), jax.ShapeDtypeStruct((m, d), BF16)][:n_out],
        compiler_params=_params(1),
        name="residual_layernorm",
    )(x, y, gain.reshape(1, d), bias.reshape(1, d))


def _dil_attn_kernel(q_ref, kp_ref, kc_ref, vp_ref, vc_ref, o_ref, lse_ref):
    has_prev = pl.program_id(1) > 0
    qi = lax.broadcasted_iota(I32, (BAND, BAND), 0)
    kj = lax.broadcasted_iota(I32, (BAND, BAND), 1)
    mask_p = jnp.logical_and(kj >= qi, has_prev)
    mask_c = kj <= qi
    lse_ref[...] = jnp.zeros_like(lse_ref)
    heads = [slice(h * HEAD_DIM, (h + 1) * HEAD_DIM) for h in range(DIL_HEADS)]
    scores = [(_dot_nt(q_ref[:, sl], kp_ref[:, sl]), _dot_nt(q_ref[:, sl], kc_ref[:, sl])) for sl in heads]
    probs = []
    for h, (sp, sc) in enumerate(scores):
        sp = jnp.where(mask_p, sp, -jnp.inf)
        sc = jnp.where(mask_c, sc, -jnp.inf)
        m = jnp.max(jnp.maximum(sp, sc), axis=1, keepdims=True)
        pp = jnp.exp2(sp - m)
        pc = jnp.exp2(sc - m)
        l = jnp.sum(pp + pc, axis=1, keepdims=True)
        lse_ref[:, h:h + 1] = m + jnp.log2(l)
        probs.append((pp.astype(BF16), pc.astype(BF16), 1.0 / l))
    for sl, (pp, pc, inv_l) in zip(heads, probs):
        o = (jnp.dot(pp, vp_ref[:, sl], preferred_element_type=F32)
             + jnp.dot(pc, vc_ref[:, sl], preferred_element_type=F32))
        o_ref[:, sl] = (o * inv_l).astype(o_ref.dtype)


def _dilated_attention(qkv, dilation, batch, seq):
    d = dilation
    n_chunks = seq // (BAND * d)
    cur = lambda col: (lambda b, ch, r: ((b * n_chunks + ch) * d + r, col))
    prev = lambda col: (lambda b, ch, r: ((b * n_chunks + jnp.maximum(ch - 1, 0)) * d + r, col))
    blk = (BAND, DIL_COLS)
    return pl.pallas_call(
        _dil_attn_kernel,
        grid=(batch, n_chunks, d),
        in_specs=[pl.BlockSpec(blk, cur(0)), pl.BlockSpec(blk, prev(1)), pl.BlockSpec(blk, cur(1)),
                  pl.BlockSpec(blk, prev(2)), pl.BlockSpec(blk, cur(2))],
        out_specs=[pl.BlockSpec(blk, cur(0)), pl.BlockSpec((BAND, HEAD_DIM), cur(0))],
        out_shape=[jax.ShapeDtypeStruct((batch * seq, DIL_COLS), BF16),
                   jax.ShapeDtypeStruct((batch * seq, HEAD_DIM), F32)],
        compiler_params=_params(3),
        name=f"dilated_attention_d{d}",
    )(qkv, qkv, qkv, qkv, qkv)


def _dil_merge_kernel(o0_ref, o1_ref, o2_ref, l0_ref, l1_ref, l2_ref, out_ref):
    l0, l1, l2 = l0_ref[...], l1_ref[...], l2_ref[...]
    mx = jnp.maximum(jnp.maximum(l0, l1), l2)
    e0, e1, e2 = jnp.exp2(l0 - mx), jnp.exp2(l1 - mx), jnp.exp2(l2 - mx)
    inv = 1.0 / (e0 + e1 + e2)
    w0, w1, w2 = e0 * inv, e1 * inv, e2 * inv
    for h in range(DIL_HEADS):
        sl = slice(h * HEAD_DIM, (h + 1) * HEAD_DIM)
        col = slice(h, h + 1)
        acc = (w0[:, col] * o0_ref[:, sl].astype(F32) + w1[:, col] * o1_ref[:, sl].astype(F32)
               + w2[:, col] * o2_ref[:, sl].astype(F32))
        out_ref[:, sl] = acc.astype(out_ref.dtype)


def _dil_merge(outs, lses):
    m = outs[0].shape[0]
    tm = min(m, 512)
    row = lambda i: (i, 0)
    return pl.pallas_call(
        _dil_merge_kernel,
        grid=(m // tm,),
        in_specs=[pl.BlockSpec((tm, DIL_COLS), row)] * 3 + [pl.BlockSpec((tm, HEAD_DIM), row)] * 3,
        out_specs=pl.BlockSpec((tm, DIL_COLS), row),
        out_shape=jax.ShapeDtypeStruct((m, DIL_COLS), BF16),
        compiler_params=_params(1),
        name="dilated_merge",
    )(*outs, *lses)


def _sortable(x):
    b = lax.bitcast_convert_type(x, I32)
    return b ^ ((b >> 31) & np.int32(0x7FFFFFFF))


def _dsa_select_kernel(qi_ref, wit_ref, ki_ref, bias_ref, qs_ref, keys_ref, *, n_chunks):
    qb = pl.program_id(1)
    n_c = (qb * QUERY_BLOCK + QUERY_BLOCK - 1) // KEY_CHUNK + 1
    for h in range(IDX_HEADS):
        qs_ref[h * QUERY_BLOCK:(h + 1) * QUERY_BLOCK, :] = qi_ref[:, h * HEAD_DIM:(h + 1) * HEAD_DIM]
    w = wit_ref[...]
    kpos = lax.broadcasted_iota(I32, (KEY_CHUNK, QUERY_BLOCK), 0)
    t = qb * QUERY_BLOCK + lax.broadcasted_iota(I32, (KEY_CHUNK, QUERY_BLOCK), 1)
    rows_per_dot = IDX_HEADS_PER_DOT * QUERY_BLOCK

    def score_chunk(c, carry):
        kic = ki_ref[0, pl.ds(pl.multiple_of(c * KEY_CHUNK, KEY_CHUNK), KEY_CHUNK), :]
        sc = jnp.zeros((KEY_CHUNK, QUERY_BLOCK), F32)
        for hg in range(IDX_HEADS // IDX_HEADS_PER_DOT):
            r = _dot_nt(kic, qs_ref[hg * rows_per_dot:(hg + 1) * rows_per_dot, :])
            for j in range(IDX_HEADS_PER_DOT):
                h = hg * IDX_HEADS_PER_DOT + j
                sc = sc + jnp.maximum(r[:, j * QUERY_BLOCK:(j + 1) * QUERY_BLOCK], 0.0) * w[h:h + 1, :]
        causal = c * KEY_CHUNK + kpos <= t
        keys_ref[c] = jnp.where(causal, _sortable(sc), INT_MIN)
        return carry

    lax.fori_loop(0, n_c, score_chunk, 0)

    def count_ge(cand):
        def body(c, acc):
            x = jnp.where(keys_ref[c] >= cand, 1, 0).astype(I32)
            return acc + jnp.sum(x.reshape(KEY_CHUNK // 8, 8, QUERY_BLOCK), axis=0)
        acc = lax.fori_loop(0, n_c, body, jnp.zeros((8, QUERY_BLOCK), I32))
        return jnp.sum(acc, axis=0, keepdims=True)

    zero = jnp.zeros((1, QUERY_BLOCK), I32)
    thr = jnp.where(count_ge(zero) >= TOPK, zero, jnp.full((1, QUERY_BLOCK), INT_MIN, I32))

    def bit_step(i, thr):
        cand = thr | jnp.left_shift(jnp.int32(1), 30 - i)
        return jnp.where(count_ge(cand) >= TOPK, cand, thr)

    thr = lax.fori_loop(0, 31, bit_step, thr)
    thr = jnp.maximum(thr, INT_MIN + 1)

    def write_bias(c, carry):
        bias_ref[0, 0, c] = jnp.where(keys_ref[c] >= thr, 0.0, MASK_NEG).astype(bias_ref.dtype)
        return carry

    def write_masked(c, carry):
        bias_ref[0, 0, c] = jnp.full((KEY_CHUNK, QUERY_BLOCK), MASK_NEG, bias_ref.dtype)
        return carry

    lax.fori_loop(0, n_c, write_bias, 0)
    lax.fori_loop(n_c, n_chunks, write_masked, 0)


def _dsa_select(qi, wit, ki, batch, seq):
    nqb = seq // QUERY_BLOCK
    nkc = seq // KEY_CHUNK
    return pl.pallas_call(
        functools.partial(_dsa_select_kernel, n_chunks=nkc),
        grid=(batch, nqb),
        in_specs=[pl.BlockSpec((QUERY_BLOCK, IDX_HEADS * HEAD_DIM), lambda b, q: (b * nqb + q, 0)),
                  pl.BlockSpec((HEAD_DIM, QUERY_BLOCK), lambda b, q: (0, b * nqb + q)),
                  pl.BlockSpec((1, seq, HEAD_DIM), lambda b, q: (b, 0, 0))],
        out_specs=pl.BlockSpec((1, 1, nkc, KEY_CHUNK, QUERY_BLOCK), lambda b, q: (b, q, 0, 0, 0)),
        out_shape=jax.ShapeDtypeStruct((batch, nqb, nkc, KEY_CHUNK, QUERY_BLOCK), BF16),
        scratch_shapes=[pltpu.VMEM((IDX_HEADS * QUERY_BLOCK, HEAD_DIM), BF16),
                        pltpu.VMEM((nkc, KEY_CHUNK, QUERY_BLOCK), I32)],
        compiler_params=_params(2),
        name="dsa_select",
    )(qi, wit, ki.reshape(batch, seq, HEAD_DIM))


def _dsa_attn_kernel(qb_tab, c_tab, q_ref, k_ref, v_ref, bias_ref, o_ref, qa_ref, m_ref, l_ref, acc_ref):
    step = pl.program_id(1)
    qb = qb_tab[step]
    c = c_tab[step]
    last = (qb * QUERY_BLOCK + QUERY_BLOCK - 1) // KEY_CHUNK
    n_tiles = KEY_CHUNK // HEAD_DIM

    @pl.when(c == 0)
    def _():
        ri = lax.broadcasted_iota(I32, (QUERY_BLOCK, HEAD_DIM), 0)
        ci = lax.broadcasted_iota(I32, (QUERY_BLOCK, HEAD_DIM), 1)
        eye = jnp.where(ri == ci, 1.0, 0.0).astype(BF16)
        for g in range(DSA_KV_HEADS):
            for j in range(DSA_GROUP):
                h = g * DSA_GROUP + j
                rows = slice(j * QUERY_BLOCK, (j + 1) * QUERY_BLOCK)
                qa_ref[g, rows, :HEAD_DIM] = q_ref[:, h * HEAD_DIM:(h + 1) * HEAD_DIM]
                qa_ref[g, rows, HEAD_DIM:] = eye
        m_ref[...] = jnp.full(m_ref.shape, M_INIT, F32)
        l_ref[...] = jnp.zeros_like(l_ref)
        acc_ref[...] = jnp.zeros_like(acc_ref)

    bias_t = bias_ref[0, 0, 0]

    def scores(g):
        ka = jnp.concatenate([k_ref[:, g * HEAD_DIM:(g + 1) * HEAD_DIM], bias_t], axis=1)
        return _dot_nt(qa_ref[g], ka)

    s_next = scores(0)
    for g in range(DSA_KV_HEADS):
        sl = slice(g * HEAD_DIM, (g + 1) * HEAD_DIM)
        s = s_next
        if g + 1 < DSA_KV_HEADS:
            s_next = scores(g + 1)
        tiles = [s[:, j * HEAD_DIM:(j + 1) * HEAD_DIM] for j in range(n_tiles)]
        m_old = m_ref[g]
        m_new = jnp.maximum(m_old, jnp.max(functools.reduce(jnp.maximum, tiles), axis=1, keepdims=True))
        alpha = jnp.exp2(m_old - m_new)
        ps = [jnp.exp2(tl - m_new) for tl in tiles]
        l_ref[g] = alpha * l_ref[g] + jnp.sum(functools.reduce(jnp.add, ps), axis=1, keepdims=True)
        p = jnp.concatenate([tl.astype(BF16) for tl in ps], axis=1)
        acc_ref[g] = alpha * acc_ref[g] + jnp.dot(p, v_ref[:, sl], preferred_element_type=F32)
        m_ref[g] = m_new

    @pl.when(c == last)
    def _():
        for g in range(DSA_KV_HEADS):
            o = acc_ref[g] / l_ref[g]
            for j in range(DSA_GROUP):
                h = g * DSA_GROUP + j
                o_ref[:, h * HEAD_DIM:(h + 1) * HEAD_DIM] = (
                    o[j * QUERY_BLOCK:(j + 1) * QUERY_BLOCK].astype(o_ref.dtype))


def _dsa_attention(q, k, v, bias_t, batch, seq):
    nqb = seq // QUERY_BLOCK
    nkc = seq // KEY_CHUNK
    rows = DSA_GROUP * QUERY_BLOCK
    steps = [(qb, c) for qb in range(nqb)
             for c in range((qb * QUERY_BLOCK + QUERY_BLOCK - 1) // KEY_CHUNK + 1)]
    qb_tab = jnp.asarray(np.array([s[0] for s in steps], np.int32))
    c_tab = jnp.asarray(np.array([s[1] for s in steps], np.int32))
    qmap = lambda b, s, qt, ct: (b * nqb + qt[s], 0)
    kvmap = lambda b, s, qt, ct: (b * nkc + ct[s], 0)
    grid_spec = pltpu.PrefetchScalarGridSpec(
        num_scalar_prefetch=2,
        grid=(batch, len(steps)),
        in_specs=[pl.BlockSpec((QUERY_BLOCK, DSA_HEADS * HEAD_DIM), qmap),
                  pl.BlockSpec((KEY_CHUNK, DSA_KV_HEADS * HEAD_DIM), kvmap),
                  pl.BlockSpec((KEY_CHUNK, DSA_KV_HEADS * HEAD_DIM), kvmap),
                  pl.BlockSpec((1, 1, 1, KEY_CHUNK, QUERY_BLOCK),
                               lambda b, s, qt, ct: (b, qt[s], ct[s], 0, 0))],
        out_specs=pl.BlockSpec((QUERY_BLOCK, DSA_HEADS * HEAD_DIM), qmap),
        scratch_shapes=[pltpu.VMEM((DSA_KV_HEADS, rows, 2 * HEAD_DIM), BF16),
                        pltpu.VMEM((DSA_KV_HEADS, rows, HEAD_DIM), F32),
                        pltpu.VMEM((DSA_KV_HEADS, rows, HEAD_DIM), F32),
                        pltpu.VMEM((DSA_KV_HEADS, rows, HEAD_DIM), F32)])
    return pl.pallas_call(
        _dsa_attn_kernel,
        grid_spec=grid_spec,
        out_shape=jax.ShapeDtypeStruct((batch * seq, DSA_HEADS * HEAD_DIM), BF16),
        compiler_params=_params(2),
        name="dsa_attention",
    )(qb_tab, c_tab, q, k, v, bias_t)


def kernel(x, positions, l0_attn_w_in, l0_attn_w_out, l1_attn_w_in, l1_k_idx_gain, l1_k_idx_bias,
           l1_attn_w_out, l0_ln_mix_gain, l0_ln_mix_bias, l0_ffn_gate, l0_ffn_up, l0_ffn_down,
           l0_ln_ffn_gain, l0_ln_ffn_bias, l1_ln_mix_gain, l1_ln_mix_bias, l1_ffn_gate, l1_ffn_up,
           l1_ffn_down, l1_ln_ffn_gain, l1_ln_ffn_bias):
    batch, seq, d_model = x.shape
    m = batch * seq
    cosf, sinf = _rope_tables(positions)
    rope = (cosf, sinf)
    xf = x.reshape(m, d_model)
    xb = xf.astype(BF16)

    tn = 512
    n_grp = len(DIL_GROUPS)
    grp_blocks = DIL_COLS // tn
    outs, lses = [], []
    for g, (_, d) in enumerate(DIL_GROUPS):
        col_block = lambda n, g=g: (n // grp_blocks) * (n_grp * grp_blocks) + g * grp_blocks + n % grp_blocks
        rope_g = tuple(_to_residue_major(t, d, batch, seq) for t in rope)
        qkv = _matmul(_to_residue_major(xb, d, batch, seq), l0_attn_w_in, col_block, 3 * DIL_COLS, BF16,
                      tn=tn, rope=rope_g, rope_cols=2 * DIL_COLS, scale=ATTN_SCALE * LOG2E,
                      scale_cols=DIL_COLS, name=f"l0_qkv_proj_d{d}")
        o, lse = _dilated_attention(qkv, d, batch, seq)
        outs.append(_from_residue_major(o, d, batch, seq))
        lses.append(_from_residue_major(lse, d, batch, seq))
    o0 = _dil_merge(outs, lses)
    y = _matmul(o0, l0_attn_w_out, lambda n: n, d_model, F32, tn=tn, name="l0_out_proj")
    xf, xb = _res_ln(xf, y, l0_ln_mix_gain, l0_ln_mix_bias)
    h = _ffn_up(xb, l0_ffn_gate, l0_ffn_up)
    y = _ffn_down(h, l0_ffn_down.astype(BF16))
    xf, xb = _res_ln(xf, y, l0_ln_ffn_gain, l0_ln_ffn_bias)

    b_q = DSA_HEADS * HEAD_DIM
    b_kv = DSA_KV_HEADS * HEAD_DIM
    b_qi = IDX_HEADS * HEAD_DIM
    o_v = b_q + b_kv
    o_qi = o_v + b_kv
    o_ki = o_qi + b_qi
    o_wi = o_ki + HEAD_DIM
    at = lambda col0: (lambda n: n + col0 // tn)
    q1 = _matmul(xb, l1_attn_w_in, at(0), b_q, BF16, tn=tn, rope=rope, scale=ATTN_SCALE * LOG2E,
                 scale_cols=b_q, name="l1_q_proj")
    k1 = _matmul(xb, l1_attn_w_in, at(b_q), b_kv, BF16, tn=tn, rope=rope, name="l1_k_proj")
    v1 = _matmul(xb, l1_attn_w_in, at(o_v), b_kv, BF16, tn=tn, name="l1_v_proj")
    qi = _matmul(xb, l1_attn_w_in, at(o_qi), b_qi, BF16, tn=tn, rope=rope, name="l1_qidx_proj")
    w_ki = l1_attn_w_in[:, o_ki:o_wi]
    w_wi_t = jnp.pad(l1_attn_w_in[:, o_wi:].T, ((0, HEAD_DIM - IDX_HEADS), (0, 0)))
    ki, wit = _kiwi(xb, w_ki, w_wi_t, cosf, sinf, l1_k_idx_gain, l1_k_idx_bias)
    bias_t = _dsa_select(qi, wit, ki, batch, seq)
    o1 = _dsa_attention(q1, k1, v1, bias_t, batch, seq)
    y = _matmul(o1, l1_attn_w_out, lambda n: n, d_model, F32, tn=tn, name="l1_out_proj")
    xf, xb = _res_ln(xf, y, l1_ln_mix_gain, l1_ln_mix_bias)
    h = _ffn_up(xb, l1_ffn_gate, l1_ffn_up)
    y = _ffn_down(h, l1_ffn_down.astype(BF16))
    (xf,) = _res_ln(xf, y, l1_ln_ffn_gain, l1_ln_ffn_bias, with_bf16=False)
    return xf.reshape(batch, seq, d_model)
```

```python
import functools
import math

import numpy as np
import jax
import jax.numpy as jnp
from jax import lax
from jax.experimental import pallas as pl
from jax.experimental.pallas import tpu as pltpu

F32 = jnp.float32
BF16 = jnp.bfloat16
I32 = jnp.int32

HEAD_DIM = 128
HALF = HEAD_DIM // 2
ROPE_THETA = 10000.0
LN_EPS = 1e-5
DEPTH = 2
DEEPNORM_ALPHA = (2 * DEPTH) ** 0.25
ATTN_SCALE = HEAD_DIM ** -0.5
LOG2E = math.log2(math.e)

DIL_GROUPS = ((128, 1), (512, 4), (2048, 16))
BAND = 128
DIL_HEADS = 16
DIL_COLS = DIL_HEADS * HEAD_DIM

DSA_HEADS = 32
DSA_KV_HEADS = 8
DSA_GROUP = DSA_HEADS // DSA_KV_HEADS
IDX_HEADS = 32
IDX_HEADS_PER_DOT = 4
TOPK = 256
QUERY_BLOCK = 128
KEY_CHUNK = 512

INT_MIN = np.int32(-2 ** 31)
MASK_NEG = -1e30
M_INIT = -1e29

VMEM_LIMIT = 56 * 1024 * 1024


def _params(n_axes):
    return pltpu.CompilerParams(dimension_semantics=("arbitrary",) * n_axes,
                                vmem_limit_bytes=VMEM_LIMIT)


def _dot_nt(a, b):
    return lax.dot_general(a, b, (((1,), (1,)), ((), ())), preferred_element_type=F32)


def _rope(a, c, s):
    return a * c + pltpu.roll(a, HALF, 1) * s


def _to_residue_major(t, d, batch, seq):
    if d == 1:
        return t
    c = t.shape[1]
    t = t.reshape(batch, seq // (BAND * d), BAND, d, c)
    return t.transpose(0, 1, 3, 2, 4).reshape(batch * seq, c)


def _from_residue_major(t, d, batch, seq):
    if d == 1:
        return t
    c = t.shape[1]
    t = t.reshape(batch, seq // (BAND * d), d, BAND, c)
    return t.transpose(0, 1, 3, 2, 4).reshape(batch * seq, c)


def _rope_table_kernel(pos_ref, inv_ref, cos_ref, sin_ref):
    ang = pos_ref[...].astype(F32) * inv_ref[...]
    lane = lax.broadcasted_iota(I32, ang.shape, 1)
    cos_ref[...] = jnp.cos(ang)
    sin_ref[...] = jnp.where(lane < HALF, -jnp.sin(ang), jnp.sin(ang))


def _rope_tables(positions):
    m = positions.size
    tm = min(m, 1024)
    inv = ROPE_THETA ** (-jnp.arange(0, HEAD_DIM, 2, dtype=F32) / HEAD_DIM)
    inv2 = jnp.concatenate([inv, inv]).reshape(1, HEAD_DIM)
    return pl.pallas_call(
        _rope_table_kernel,
        grid=(m // tm,),
        in_specs=[pl.BlockSpec((tm, 1), lambda i: (i, 0)),
                  pl.BlockSpec((1, HEAD_DIM), lambda i: (0, 0))],
        out_specs=[pl.BlockSpec((tm, HEAD_DIM), lambda i: (i, 0))] * 2,
        out_shape=[jax.ShapeDtypeStruct((m, HEAD_DIM), F32)] * 2,
        compiler_params=_params(1),
        name="rope_tables",
    )(positions.reshape(m, 1), inv2)


def _mm_kernel(x_ref, w_ref, *rest, scale_blocks, rope_blocks, scale):
    if rope_blocks:
        cos_ref, sin_ref, o_ref, wbf_ref = rest
    else:
        o_ref, wbf_ref = rest

    @pl.when(pl.program_id(1) == 0)
    def _():
        wbf_ref[...] = w_ref[...].astype(BF16)

    acc = jnp.dot(x_ref[...], wbf_ref[...], preferred_element_type=F32)

    def store_plain():
        o_ref[...] = acc.astype(o_ref.dtype)

    def store_rope(factor):
        c = cos_ref[...] * factor if factor != 1.0 else cos_ref[...]
        s = sin_ref[...] * factor if factor != 1.0 else sin_ref[...]
        for j in range(acc.shape[1] // HEAD_DIM):
            sl = slice(j * HEAD_DIM, (j + 1) * HEAD_DIM)
            o_ref[:, sl] = _rope(acc[:, sl], c, s).astype(o_ref.dtype)

    n = pl.program_id(0)
    if not rope_blocks:
        store_plain()
    else:
        if scale_blocks:
            pl.when(n < scale_blocks)(functools.partial(store_rope, scale))
        pl.when(jnp.logical_and(n >= scale_blocks, n < rope_blocks))(functools.partial(store_rope, 1.0))
        pl.when(n >= rope_blocks)(store_plain)


def _matmul(x, w, col_block, n_cols, out_dtype, *, tn, rope=None, rope_cols=None, scale=1.0,
            scale_cols=0, name):
    m, k = x.shape
    tm = min(m, 1024)
    assert m % tm == 0 and n_cols % tn == 0
    n_blocks = n_cols // tn
    rope_blocks = 0
    in_specs = [pl.BlockSpec((tm, k), lambda n, i: (i, 0)),
                pl.BlockSpec((k, tn), lambda n, i: (0, col_block(n)))]
    args = [x, w]
    if rope is not None:
        rope_blocks = n_blocks if rope_cols is None else rope_cols // tn
        in_specs += [pl.BlockSpec((tm, HEAD_DIM), lambda n, i: (i, 0))] * 2
        args += list(rope)
    return pl.pallas_call(
        functools.partial(_mm_kernel, scale_blocks=scale_cols // tn, rope_blocks=rope_blocks, scale=scale),
        grid=(n_blocks, m // tm),
        in_specs=in_specs,
        out_specs=pl.BlockSpec((tm, tn), lambda n, i: (i, n)),
        out_shape=jax.ShapeDtypeStruct((m, n_cols), out_dtype),
        scratch_shapes=[pltpu.VMEM((k, tn), BF16)],
        compiler_params=_params(2),
        name=name,
    )(*args)


def _kiwi_kernel(x_ref, wk_ref, wwt_ref, cos_ref, sin_ref, g_ref, b_ref, ki_ref, wit_ref, wkbf_ref, wwbf_ref):
    @pl.when(pl.program_id(0) == 0)
    def _():
        wkbf_ref[...] = wk_ref[...].astype(BF16)
        wwbf_ref[...] = wwt_ref[...].astype(BF16)

    x = x_ref[...]
    a = jnp.dot(x, wkbf_ref[...], preferred_element_type=F32)
    mu = jnp.mean(a, axis=-1, keepdims=True)
    var = jnp.mean(jnp.square(a - mu), axis=-1, keepdims=True)
    y = (a - mu) * lax.rsqrt(var + LN_EPS) * g_ref[...] + b_ref[...]
    ki_ref[...] = _rope(y, cos_ref[...], sin_ref[...]).astype(ki_ref.dtype)
    wit_ref[...] = _dot_nt(wwbf_ref[...], x) * (IDX_HEADS ** -0.5 * HEAD_DIM ** -0.5)


def _kiwi(x, w_ki, w_wi_t, cosf, sinf, gain, bias):
    m, k = x.shape
    tm = min(m, 1024)
    row = lambda i: (i, 0)
    fixed = lambda i: (0, 0)
    return pl.pallas_call(
        _kiwi_kernel,
        grid=(m // tm,),
        in_specs=[pl.BlockSpec((tm, k), row), pl.BlockSpec((k, HEAD_DIM), fixed),
                  pl.BlockSpec((HEAD_DIM, k), fixed),
                  pl.BlockSpec((tm, HEAD_DIM), row), pl.BlockSpec((tm, HEAD_DIM), row),
                  pl.BlockSpec((1, HEAD_DIM), fixed), pl.BlockSpec((1, HEAD_DIM), fixed)],
        out_specs=[pl.BlockSpec((tm, HEAD_DIM), row), pl.BlockSpec((HEAD_DIM, tm), lambda i: (0, i))],
        out_shape=[jax.ShapeDtypeStruct((m, HEAD_DIM), BF16),
                   jax.ShapeDtypeStruct((HEAD_DIM, m), F32)],
        scratch_shapes=[pltpu.VMEM((k, HEAD_DIM), BF16), pltpu.VMEM((HEAD_DIM, k), BF16)],
        compiler_params=_params(1),
        name="dsa_kiwi_proj",
    )(x, w_ki, w_wi_t, cosf, sinf, gain.reshape(1, HEAD_DIM), bias.reshape(1, HEAD_DIM))


def _ffn_up_kernel(x_ref, wg_ref, wu_ref, h_ref, wgbf_ref, wubf_ref, *, n_valid):
    j = pl.program_id(0)

    @pl.when(jnp.logical_and(pl.program_id(1) == 0, j < n_valid))
    def _():
        wgbf_ref[...] = wg_ref[...].astype(BF16)
        wubf_ref[...] = wu_ref[...].astype(BF16)

    @pl.when(j < n_valid)
    def _():
        x = x_ref[...]
        g = jnp.dot(x, wgbf_ref[...], preferred_element_type=F32)
        u = jnp.dot(x, wubf_ref[...], preferred_element_type=F32)
        h_ref[...] = (g * (1.0 / (1.0 + jnp.exp(-g))) * u).astype(h_ref.dtype)

    @pl.when(j >= n_valid)
    def _():
        h_ref[...] = jnp.zeros_like(h_ref)


def _ffn_up(x, w_gate, w_up, n_pad, *, tn=256):
    m, k = x.shape
    n = w_gate.shape[1]
    tm = min(m, 1024)
    assert n % tn == 0 and n_pad % tn == 0
    n_valid = n // tn
    wspec = pl.BlockSpec((k, tn), lambda j, i: (0, jnp.minimum(j, n_valid - 1)))
    return pl.pallas_call(
        functools.partial(_ffn_up_kernel, n_valid=n_valid),
        grid=(n_pad // tn, m // tm),
        in_specs=[pl.BlockSpec((tm, k), lambda j, i: (i, 0)), wspec, wspec],
        out_specs=pl.BlockSpec((tm, tn), lambda j, i: (i, j)),
        out_shape=jax.ShapeDtypeStruct((m, n_pad), BF16),
        scratch_shapes=[pltpu.VMEM((k, tn), BF16)] * 2,
        compiler_params=_params(2),
        name="ffn_up",
    )(x, w_gate, w_up)


LN_SLAB = 64
RES_LN_TM = 512
RES_LN_TK = 512


def _mm_res_ln_kernel(a_ref, w_ref, x_ref, g_ref, b_ref, o_ref, *maybe_obf_ref):
    k = pl.program_id(1)

    @pl.when(k == 0)
    def _():
        o_ref[...] = jnp.dot(a_ref[...], w_ref[...], preferred_element_type=F32)

    @pl.when(k > 0)
    def _():
        o_ref[...] += jnp.dot(a_ref[...], w_ref[...], preferred_element_type=F32)

    @pl.when(k == pl.num_programs(1) - 1)
    def _():
        gain = g_ref[...]
        bias = b_ref[...]
        for r in range(0, o_ref.shape[0], LN_SLAB):
            rows = slice(r, r + LN_SLAB)
            z = DEEPNORM_ALPHA * x_ref[rows, :] + o_ref[rows, :]
            mu = jnp.mean(z, axis=-1, keepdims=True)
            d = z - mu
            var = jnp.mean(jnp.square(d), axis=-1, keepdims=True)
            out = d * lax.rsqrt(var + LN_EPS) * gain + bias
            o_ref[rows, :] = out
            for obf_ref in maybe_obf_ref:
                obf_ref[rows, :] = out.astype(BF16)


def _mm_res_ln(a, w_bf, x, gain, bias, *, with_bf16=True, name):
    m, k = a.shape
    d = w_bf.shape[1]
    tm = min(m, RES_LN_TM)
    tk = RES_LN_TK
    assert m % tm == 0 and k % tk == 0 and tm % LN_SLAB == 0
    row = lambda i, j: (i, 0)
    fixed = lambda i, j: (0, 0)
    n_out = 2 if with_bf16 else 1
    return pl.pallas_call(
        _mm_res_ln_kernel,
        grid=(m // tm, k // tk),
        in_specs=[pl.BlockSpec((tm, tk), lambda i, j: (i, j)),
                  pl.BlockSpec((tk, d), lambda i, j: (j, 0)),
                  pl.BlockSpec((tm, d), row, pipeline_mode=pl.Buffered(1)),
                  pl.BlockSpec((1, d), fixed), pl.BlockSpec((1, d), fixed)],
        out_specs=[pl.BlockSpec((tm, d), row)] * n_out,
        out_shape=[jax.ShapeDtypeStruct((m, d), F32), jax.ShapeDtypeStruct((m, d), BF16)][:n_out],
        compiler_params=_params(2),
        name=name,
    )(a, w_bf, x, gain.reshape(1, d), bias.reshape(1, d))


def _dil_attn_kernel(q_ref, kp_ref, kc_ref, vp_ref, vc_ref, o_ref, lse_ref):
    has_prev = pl.program_id(1) > 0
    qi = lax.broadcasted_iota(I32, (BAND, BAND), 0)
    kj = lax.broadcasted_iota(I32, (BAND, BAND), 1)
    mask_p = jnp.logical_and(kj >= qi, has_prev)
    mask_c = kj <= qi
    lse_ref[...] = jnp.zeros_like(lse_ref)
    heads = [slice(h * HEAD_DIM, (h + 1) * HEAD_DIM) for h in range(DIL_HEADS)]
    scores = [(_dot_nt(q_ref[:, sl], kp_ref[:, sl]), _dot_nt(q_ref[:, sl], kc_ref[:, sl])) for sl in heads]
    probs = []
    for h, (sp, sc) in enumerate(scores):
        sp = jnp.where(mask_p, sp, -jnp.inf)
        sc = jnp.where(mask_c, sc, -jnp.inf)
        m = jnp.max(jnp.maximum(sp, sc), axis=1, keepdims=True)
        pp = jnp.exp2(sp - m)
        pc = jnp.exp2(sc - m)
        l = jnp.sum(pp + pc, axis=1, keepdims=True)
        lse_ref[:, h:h + 1] = m + jnp.log2(l)
        probs.append((pp.astype(BF16), pc.astype(BF16), 1.0 / l))
    for sl, (pp, pc, inv_l) in zip(heads, probs):
        o = (jnp.dot(pp, vp_ref[:, sl], preferred_element_type=F32)
             + jnp.dot(pc, vc_ref[:, sl], preferred_element_type=F32))
        o_ref[:, sl] = (o * inv_l).astype(o_ref.dtype)


def _dilated_attention(qkv, dilation, batch, seq):
    d = dilation
    n_chunks = seq // (BAND * d)
    cur = lambda col: (lambda b, ch, r: ((b * n_chunks + ch) * d + r, col))
    prev = lambda col: (lambda b, ch, r: ((b * n_chunks + jnp.maximum(ch - 1, 0)) * d + r, col))
    blk = (BAND, DIL_COLS)
    return pl.pallas_call(
        _dil_attn_kernel,
        grid=(batch, n_chunks, d),
        in_specs=[pl.BlockSpec(blk, cur(0)), pl.BlockSpec(blk, prev(1)), pl.BlockSpec(blk, cur(1)),
                  pl.BlockSpec(blk, prev(2)), pl.BlockSpec(blk, cur(2))],
        out_specs=[pl.BlockSpec(blk, cur(0)), pl.BlockSpec((BAND, HEAD_DIM), cur(0))],
        out_shape=[jax.ShapeDtypeStruct((batch * seq, DIL_COLS), BF16),
                   jax.ShapeDtypeStruct((batch * seq, HEAD_DIM), F32)],
        compiler_params=_params(3),
        name=f"dilated_attention_d{d}",
    )(qkv, qkv, qkv, qkv, qkv)


def _dil_merge_kernel(o0_ref, o1_ref, o2_ref, l0_ref, l1_ref, l2_ref, out_ref):
    l0, l1, l2 = l0_ref[...], l1_ref[...], l2_ref[...]
    mx = jnp.maximum(jnp.maximum(l0, l1), l2)
    e0, e1, e2 = jnp.exp2(l0 - mx), jnp.exp2(l1 - mx), jnp.exp2(l2 - mx)
    inv = 1.0 / (e0 + e1 + e2)
    w0, w1, w2 = e0 * inv, e1 * inv, e2 * inv
    for h in range(DIL_HEADS):
        sl = slice(h * HEAD_DIM, (h + 1) * HEAD_DIM)
        col = slice(h, h + 1)
        acc = (w0[:, col] * o0_ref[:, sl].astype(F32) + w1[:, col] * o1_ref[:, sl].astype(F32)
               + w2[:, col] * o2_ref[:, sl].astype(F32))
        out_ref[:, sl] = acc.astype(out_ref.dtype)


def _dil_merge(outs, lses):
    m = outs[0].shape[0]
    tm = min(m, 512)
    row = lambda i: (i, 0)
    return pl.pallas_call(
        _dil_merge_kernel,
        grid=(m // tm,),
        in_specs=[pl.BlockSpec((tm, DIL_COLS), row)] * 3 + [pl.BlockSpec((tm, HEAD_DIM), row)] * 3,
        out_specs=pl.BlockSpec((tm, DIL_COLS), row),
        out_shape=jax.ShapeDtypeStruct((m, DIL_COLS), BF16),
        compiler_params=_params(1),
        name="dilated_merge",
    )(*outs, *lses)


def _sortable(x):
    b = lax.bitcast_convert_type(x, I32)
    return b ^ ((b >> 31) & np.int32(0x7FFFFFFF))


def _dsa_select_kernel(qi_ref, wit_ref, ki_ref, bias_ref, qs_ref, keys_ref, *, n_chunks):
    qb = pl.program_id(1)
    n_c = (qb * QUERY_BLOCK + QUERY_BLOCK - 1) // KEY_CHUNK + 1
    for h in range(IDX_HEADS):
        qs_ref[h * QUERY_BLOCK:(h + 1) * QUERY_BLOCK, :] = qi_ref[:, h * HEAD_DIM:(h + 1) * HEAD_DIM]
    w = wit_ref[...]
    kpos = lax.broadcasted_iota(I32, (KEY_CHUNK, QUERY_BLOCK), 0)
    t = qb * QUERY_BLOCK + lax.broadcasted_iota(I32, (KEY_CHUNK, QUERY_BLOCK), 1)
    rows_per_dot = IDX_HEADS_PER_DOT * QUERY_BLOCK

    def score_chunk(c, carry):
        kic = ki_ref[0, pl.ds(pl.multiple_of(c * KEY_CHUNK, KEY_CHUNK), KEY_CHUNK), :]
        sc = jnp.zeros((KEY_CHUNK, QUERY_BLOCK), F32)
        for hg in range(IDX_HEADS // IDX_HEADS_PER_DOT):
            r = _dot_nt(kic, qs_ref[hg * rows_per_dot:(hg + 1) * rows_per_dot, :])
            for j in range(IDX_HEADS_PER_DOT):
                h = hg * IDX_HEADS_PER_DOT + j
                sc = sc + jnp.maximum(r[:, j * QUERY_BLOCK:(j + 1) * QUERY_BLOCK], 0.0) * w[h:h + 1, :]
        causal = c * KEY_CHUNK + kpos <= t
        keys_ref[c] = jnp.where(causal, _sortable(sc), INT_MIN)
        return carry

    lax.fori_loop(0, n_c, score_chunk, 0)

    def count_ge(cand):
        def body(c, acc):
            x = jnp.where(keys_ref[c] >= cand, 1, 0).astype(I32)
            return acc + jnp.sum(x.reshape(KEY_CHUNK // 8, 8, QUERY_BLOCK), axis=0)
        acc = lax.fori_loop(0, n_c, body, jnp.zeros((8, QUERY_BLOCK), I32))
        return jnp.sum(acc, axis=0, keepdims=True)

    zero = jnp.zeros((1, QUERY_BLOCK), I32)
    thr = jnp.where(count_ge(zero) >= TOPK, zero, jnp.full((1, QUERY_BLOCK), INT_MIN, I32))

    def bit_step(i, thr):
        cand = thr | jnp.left_shift(jnp.int32(1), 30 - i)
        return jnp.where(count_ge(cand) >= TOPK, cand, thr)

    thr = lax.fori_loop(0, 31, bit_step, thr)
    thr = jnp.maximum(thr, INT_MIN + 1)

    def write_bias(c, carry):
        bias_ref[0, 0, c] = jnp.where(keys_ref[c] >= thr, 0.0, MASK_NEG).astype(bias_ref.dtype)
        return carry

    def write_masked(c, carry):
        bias_ref[0, 0, c] = jnp.full((KEY_CHUNK, QUERY_BLOCK), MASK_NEG, bias_ref.dtype)
        return carry

    lax.fori_loop(0, n_c, write_bias, 0)
    lax.fori_loop(n_c, n_chunks, write_masked, 0)


def _dsa_select(qi, wit, ki, batch, seq):
    nqb = seq // QUERY_BLOCK
    nkc = seq // KEY_CHUNK
    return pl.pallas_call(
        functools.partial(_dsa_select_kernel, n_chunks=nkc),
        grid=(batch, nqb),
        in_specs=[pl.BlockSpec((QUERY_BLOCK, IDX_HEADS * HEAD_DIM), lambda b, q: (b * nqb + q, 0)),
                  pl.BlockSpec((HEAD_DIM, QUERY_BLOCK), lambda b, q: (0, b * nqb + q)),
                  pl.BlockSpec((1, seq, HEAD_DIM), lambda b, q: (b, 0, 0))],
        out_specs=pl.BlockSpec((1, 1, nkc, KEY_CHUNK, QUERY_BLOCK), lambda b, q: (b, q, 0, 0, 0)),
        out_shape=jax.ShapeDtypeStruct((batch, nqb, nkc, KEY_CHUNK, QUERY_BLOCK), BF16),
        scratch_shapes=[pltpu.VMEM((IDX_HEADS * QUERY_BLOCK, HEAD_DIM), BF16),
                        pltpu.VMEM((nkc, KEY_CHUNK, QUERY_BLOCK), I32)],
        compiler_params=_params(2),
        name="dsa_select",
    )(qi, wit, ki.reshape(batch, seq, HEAD_DIM))


def _dsa_attn_kernel(qb_tab, c_tab, q_ref, k_ref, v_ref, bias_ref, o_ref, qa_ref, m_ref, l_ref, acc_ref):
    step = pl.program_id(1)
    qb = qb_tab[step]
    c = c_tab[step]
    last = (qb * QUERY_BLOCK + QUERY_BLOCK - 1) // KEY_CHUNK
    n_tiles = KEY_CHUNK // HEAD_DIM

    @pl.when(c == 0)
    def _():
        ri = lax.broadcasted_iota(I32, (QUERY_BLOCK, HEAD_DIM), 0)
        ci = lax.broadcasted_iota(I32, (QUERY_BLOCK, HEAD_DIM), 1)
        eye = jnp.where(ri == ci, 1.0, 0.0).astype(BF16)
        for g in range(DSA_KV_HEADS):
            for j in range(DSA_GROUP):
                h = g * DSA_GROUP + j
                rows = slice(j * QUERY_BLOCK, (j + 1) * QUERY_BLOCK)
                qa_ref[g, rows, :HEAD_DIM] = q_ref[:, h * HEAD_DIM:(h + 1) * HEAD_DIM]
                qa_ref[g, rows, HEAD_DIM:] = eye
        m_ref[...] = jnp.full(m_ref.shape, M_INIT, F32)
        l_ref[...] = jnp.zeros_like(l_ref)
        acc_ref[...] = jnp.zeros_like(acc_ref)

    bias_t = bias_ref[0, 0, 0]

    def scores(g):
        ka = jnp.concatenate([k_ref[:, g * HEAD_DIM:(g + 1) * HEAD_DIM], bias_t], axis=1)
        return _dot_nt(qa_ref[g], ka)

    s_next = scores(0)
    for g in range(DSA_KV_HEADS):
        sl = slice(g * HEAD_DIM, (g + 1) * HEAD_DIM)
        s = s_next
        if g + 1 < DSA_KV_HEADS:
            s_next = scores(g + 1)
        tiles = [s[:, j * HEAD_DIM:(j + 1) * HEAD_DIM] for j in range(n_tiles)]
        m_old = m_ref[g]
        m_new = jnp.maximum(m_old, jnp.max(functools.reduce(jnp.maximum, tiles), axis=1, keepdims=True))
        alpha = jnp.exp2(m_old - m_new)
        ps = [jnp.exp2(tl - m_new) for tl in tiles]
        l_ref[g] = alpha * l_ref[g] + jnp.sum(functools.reduce(jnp.add, ps), axis=1, keepdims=True)
        p = jnp.concatenate([tl.astype(BF16) for tl in ps], axis=1)
        acc_ref[g] = alpha * acc_ref[g] + jnp.dot(p, v_ref[:, sl], preferred_element_type=F32)
        m_ref[g] = m_new

    @pl.when(c == last)
    def _():
        for g in range(DSA_KV_HEADS):
            o = acc_ref[g] / l_ref[g]
            for j in range(DSA_GROUP):
                h = g * DSA_GROUP + j
                o_ref[:, h * HEAD_DIM:(h + 1) * HEAD_DIM] = (
                    o[j * QUERY_BLOCK:(j + 1) * QUERY_BLOCK].astype(o_ref.dtype))


def _dsa_attention(q, k, v, bias_t, batch, seq):
    nqb = seq // QUERY_BLOCK
    nkc = seq // KEY_CHUNK
    rows = DSA_GROUP * QUERY_BLOCK
    steps = [(qb, c) for qb in range(nqb)
             for c in range((qb * QUERY_BLOCK + QUERY_BLOCK - 1) // KEY_CHUNK + 1)]
    qb_tab = jnp.asarray(np.array([s[0] for s in steps], np.int32))
    c_tab = jnp.asarray(np.array([s[1] for s in steps], np.int32))
    qmap = lambda b, s, qt, ct: (b * nqb + qt[s], 0)
    kvmap = lambda b, s, qt, ct: (b * nkc + ct[s], 0)
    grid_spec = pltpu.PrefetchScalarGridSpec(
        num_scalar_prefetch=2,
        grid=(batch, len(steps)),
        in_specs=[pl.BlockSpec((QUERY_BLOCK, DSA_HEADS * HEAD_DIM), qmap),
                  pl.BlockSpec((KEY_CHUNK, DSA_KV_HEADS * HEAD_DIM), kvmap),
                  pl.BlockSpec((KEY_CHUNK, DSA_KV_HEADS * HEAD_DIM), kvmap),
                  pl.BlockSpec((1, 1, 1, KEY_CHUNK, QUERY_BLOCK),
                               lambda b, s, qt, ct: (b, qt[s], ct[s], 0, 0))],
        out_specs=pl.BlockSpec((QUERY_BLOCK, DSA_HEADS * HEAD_DIM), qmap),
        scratch_shapes=[pltpu.VMEM((DSA_KV_HEADS, rows, 2 * HEAD_DIM), BF16),
                        pltpu.VMEM((DSA_KV_HEADS, rows, HEAD_DIM), F32),
                        pltpu.VMEM((DSA_KV_HEADS, rows, HEAD_DIM), F32),
                        pltpu.VMEM((DSA_KV_HEADS, rows, HEAD_DIM), F32)])
    return pl.pallas_call(
        _dsa_attn_kernel,
        grid_spec=grid_spec,
        out_shape=jax.ShapeDtypeStruct((batch * seq, DSA_HEADS * HEAD_DIM), BF16),
        compiler_params=_params(2),
        name="dsa_attention",
    )(qb_tab, c_tab, q, k, v, bias_t)


def kernel(x, positions, l0_attn_w_in, l0_attn_w_out, l1_attn_w_in, l1_k_idx_gain, l1_k_idx_bias,
           l1_attn_w_out, l0_ln_mix_gain, l0_ln_mix_bias, l0_ffn_gate, l0_ffn_up, l0_ffn_down,
           l0_ln_ffn_gain, l0_ln_ffn_bias, l1_ln_mix_gain, l1_ln_mix_bias, l1_ffn_gate, l1_ffn_up,
           l1_ffn_down, l1_ln_ffn_gain, l1_ln_ffn_bias):
    batch, seq, d_model = x.shape
    m = batch * seq
    cosf, sinf = _rope_tables(positions)
    rope = (cosf, sinf)
    xf = x.reshape(m, d_model)
    xb = xf.astype(BF16)

    tn = 512
    n_grp = len(DIL_GROUPS)
    grp_blocks = DIL_COLS // tn
    outs, lses = [], []
    for g, (_, d) in enumerate(DIL_GROUPS):
        col_block = lambda n, g=g: (n // grp_blocks) * (n_grp * grp_blocks) + g * grp_blocks + n % grp_blocks
        rope_g = tuple(_to_residue_major(t, d, batch, seq) for t in rope)
        qkv = _matmul(_to_residue_major(xb, d, batch, seq), l0_attn_w_in, col_block, 3 * DIL_COLS, BF16,
                      tn=tn, rope=rope_g, rope_cols=2 * DIL_COLS, scale=ATTN_SCALE * LOG2E,
                      scale_cols=DIL_COLS, name=f"l0_qkv_proj_d{d}")
        o, lse = _dilated_attention(qkv, d, batch, seq)
        outs.append(_from_residue_major(o, d, batch, seq))
        lses.append(_from_residue_major(lse, d, batch, seq))
    o0 = _dil_merge(outs, lses)
    xf, xb = _mm_res_ln(o0, l0_attn_w_out.astype(BF16), xf, l0_ln_mix_gain, l0_ln_mix_bias,
                        name="l0_out_proj_ln")
    d_ff = l0_ffn_gate.shape[1]
    ff_pad = -(-d_ff // RES_LN_TK) * RES_LN_TK
    pad_rows = lambda w: jnp.pad(w.astype(BF16), ((0, ff_pad - d_ff), (0, 0)))
    h = _ffn_up(xb, l0_ffn_gate, l0_ffn_up, ff_pad)
    xf, xb = _mm_res_ln(h, pad_rows(l0_ffn_down), xf, l0_ln_ffn_gain, l0_ln_ffn_bias, name="l0_ffn_down_ln")

    b_q = DSA_HEADS * HEAD_DIM
    b_kv = DSA_KV_HEADS * HEAD_DIM
    b_qi = IDX_HEADS * HEAD_DIM
    o_v = b_q + b_kv
    o_qi = o_v + b_kv
    o_ki = o_qi + b_qi
    o_wi = o_ki + HEAD_DIM
    at = lambda col0: (lambda n: n + col0 // tn)
    q1 = _matmul(xb, l1_attn_w_in, at(0), b_q, BF16, tn=tn, rope=rope, scale=ATTN_SCALE * LOG2E,
                 scale_cols=b_q, name="l1_q_proj")
    k1 = _matmul(xb, l1_attn_w_in, at(b_q), b_kv, BF16, tn=tn, rope=rope, name="l1_k_proj")
    v1 = _matmul(xb, l1_attn_w_in, at(o_v), b_kv, BF16, tn=tn, name="l1_v_proj")
    qi = _matmul(xb, l1_attn_w_in, at(o_qi), b_qi, BF16, tn=tn, rope=rope, name="l1_qidx_proj")
    w_ki = l1_attn_w_in[:, o_ki:o_wi]
    w_wi_t = jnp.pad(l1_attn_w_in[:, o_wi:].T, ((0, HEAD_DIM - IDX_HEADS), (0, 0)))
    ki, wit = _kiwi(xb, w_ki, w_wi_t, cosf, sinf, l1_k_idx_gain, l1_k_idx_bias)
    bias_t = _dsa_select(qi, wit, ki, batch, seq)
    o1 = _dsa_attention(q1, k1, v1, bias_t, batch, seq)
    xf, xb = _mm_res_ln(o1, l1_attn_w_out.astype(BF16), xf, l1_ln_mix_gain, l1_ln_mix_bias,
                        name="l1_out_proj_ln")
    h = _ffn_up(xb, l1_ffn_gate, l1_ffn_up, ff_pad)
    (xf,) = _mm_res_ln(h, pad_rows(l1_ffn_down), xf, l1_ln_ffn_gain, l1_ln_ffn_bias, with_bf16=False,
                       name="l1_ffn_down_ln")
    return xf.reshape(batch, seq, d_model)
```

```python
import functools
import math

import numpy as np
import jax
import jax.numpy as jnp
from jax import lax
from jax.experimental import pallas as pl
from jax.experimental.pallas import tpu as pltpu

F32 = jnp.float32
BF16 = jnp.bfloat16
I32 = jnp.int32

HEAD_DIM = 128
HALF = HEAD_DIM // 2
ROPE_THETA = 10000.0
LN_EPS = 1e-5
DEPTH = 2
DEEPNORM_ALPHA = (2 * DEPTH) ** 0.25
ATTN_SCALE = HEAD_DIM ** -0.5
LOG2E = math.log2(math.e)

DIL_GROUPS = ((128, 1), (512, 4), (2048, 16))
BAND = 128
DIL_HEADS = 16
DIL_COLS = DIL_HEADS * HEAD_DIM

DSA_HEADS = 32
DSA_KV_HEADS = 8
DSA_GROUP = DSA_HEADS // DSA_KV_HEADS
IDX_HEADS = 32
IDX_HEADS_PER_DOT = 4
TOPK = 256
QUERY_BLOCK = 128
KEY_CHUNK = 512

INT_MIN = np.int32(-2 ** 31)
MASK_NEG = -1e30
M_INIT = -1e29

VMEM_LIMIT = 56 * 1024 * 1024


def _params(n_axes):
    return pltpu.CompilerParams(dimension_semantics=("arbitrary",) * n_axes,
                                vmem_limit_bytes=VMEM_LIMIT)


def _dot_nt(a, b):
    return lax.dot_general(a, b, (((1,), (1,)), ((), ())), preferred_element_type=F32)


def _rope(a, c, s):
    return a * c + pltpu.roll(a, HALF, 1) * s


def _to_residue_major(t, d, batch, seq):
    if d == 1:
        return t
    c = t.shape[1]
    t = t.reshape(batch, seq // (BAND * d), BAND, d, c)
    return t.transpose(0, 1, 3, 2, 4).reshape(batch * seq, c)


def _from_residue_major(t, d, batch, seq):
    if d == 1:
        return t
    c = t.shape[1]
    t = t.reshape(batch, seq // (BAND * d), d, BAND, c)
    return t.transpose(0, 1, 3, 2, 4).reshape(batch * seq, c)


def _rope_table_kernel(pos_ref, inv_ref, cos_ref, sin_ref):
    ang = pos_ref[...].astype(F32) * inv_ref[...]
    lane = lax.broadcasted_iota(I32, ang.shape, 1)
    cos_ref[...] = jnp.cos(ang)
    sin_ref[...] = jnp.where(lane < HALF, -jnp.sin(ang), jnp.sin(ang))


def _rope_tables(positions):
    m = positions.size
    tm = min(m, 1024)
    inv = ROPE_THETA ** (-jnp.arange(0, HEAD_DIM, 2, dtype=F32) / HEAD_DIM)
    inv2 = jnp.concatenate([inv, inv]).reshape(1, HEAD_DIM)
    return pl.pallas_call(
        _rope_table_kernel,
        grid=(m // tm,),
        in_specs=[pl.BlockSpec((tm, 1), lambda i: (i, 0)),
                  pl.BlockSpec((1, HEAD_DIM), lambda i: (0, 0))],
        out_specs=[pl.BlockSpec((tm, HEAD_DIM), lambda i: (i, 0))] * 2,
        out_shape=[jax.ShapeDtypeStruct((m, HEAD_DIM), F32)] * 2,
        compiler_params=_params(1),
        name="rope_tables",
    )(positions.reshape(m, 1), inv2)


MXU_WIDTH = 256


def _mm_kernel(x_ref, w_ref, *rest, w_is_f32, has_rope, n_blocks, scale_blocks, scale):
    rest = list(rest)
    wbf_ref = rest.pop() if w_is_f32 else w_ref
    o_ref = rest.pop()
    if w_is_f32:
        @pl.when(pl.program_id(1) == 0)
        def _():
            wbf_ref[...] = w_ref[...].astype(BF16)

    x = x_ref[...]
    strips = [slice(j, j + MXU_WIDTH) for j in range(0, o_ref.shape[1], MXU_WIDTH)]
    accs = [jnp.dot(x, wbf_ref[:, st], preferred_element_type=F32) for st in strips]
    if not has_rope:
        for st, acc in zip(strips, accs):
            o_ref[:, st] = acc.astype(o_ref.dtype)
        return
    cos_ref, sin_ref = rest
    c, s = cos_ref[...], sin_ref[...]
    if scale_blocks:
        factor = scale if scale_blocks == n_blocks else jnp.where(pl.program_id(0) < scale_blocks, scale, 1.0)
        c, s = c * factor, s * factor
    for st, acc in zip(strips, accs):
        for j in range(0, MXU_WIDTH, HEAD_DIM):
            o_ref[:, st.start + j:st.start + j + HEAD_DIM] = (
                _rope(acc[:, j:j + HEAD_DIM], c, s).astype(o_ref.dtype))


def _matmul(x, w, col_block, n_cols, out_dtype, *, tn, tm=1024, rope=None, scale=1.0, scale_cols=0, name):
    m, k = x.shape
    tm = min(m, tm)
    assert m % tm == 0 and n_cols % tn == 0 and tn % MXU_WIDTH == 0 and scale_cols % tn == 0
    n_blocks = n_cols // tn
    w_is_f32 = w.dtype == F32
    w_mode = {} if w_is_f32 else {"pipeline_mode": pl.Buffered(1)}
    in_specs = [pl.BlockSpec((tm, k), lambda n, i: (i, 0)),
                pl.BlockSpec((k, tn), lambda n, i: (0, col_block(n)), **w_mode)]
    args = [x, w]
    if rope is not None:
        in_specs += [pl.BlockSpec((tm, HEAD_DIM), lambda n, i: (i, 0))] * 2
        args += list(rope)
    return pl.pallas_call(
        functools.partial(_mm_kernel, w_is_f32=w_is_f32, has_rope=rope is not None,
                          n_blocks=n_blocks, scale_blocks=scale_cols // tn, scale=scale),
        grid=(n_blocks, m // tm),
        in_specs=in_specs,
        out_specs=pl.BlockSpec((tm, tn), lambda n, i: (i, n)),
        out_shape=jax.ShapeDtypeStruct((m, n_cols), out_dtype),
        scratch_shapes=[pltpu.VMEM((k, tn), BF16)] if w_is_f32 else [],
        compiler_params=_params(2),
        name=name,
    )(*args)


def _kiwi_kernel(x_ref, wk_ref, wwt_ref, cos_ref, sin_ref, g_ref, b_ref, ki_ref, wit_ref, wkbf_ref, wwbf_ref):
    @pl.when(pl.program_id(0) == 0)
    def _():
        wkbf_ref[...] = wk_ref[...].astype(BF16)
        wwbf_ref[...] = wwt_ref[...].astype(BF16)

    x = x_ref[...]
    a = jnp.dot(x, wkbf_ref[...], preferred_element_type=F32)
    mu = jnp.mean(a, axis=-1, keepdims=True)
    var = jnp.mean(jnp.square(a - mu), axis=-1, keepdims=True)
    y = (a - mu) * lax.rsqrt(var + LN_EPS) * g_ref[...] + b_ref[...]
    ki_ref[...] = _rope(y, cos_ref[...], sin_ref[...]).astype(ki_ref.dtype)
    wit_ref[...] = _dot_nt(wwbf_ref[...], x) * (IDX_HEADS ** -0.5 * HEAD_DIM ** -0.5)


def _kiwi(x, w_ki, w_wi_t, cosf, sinf, gain, bias):
    m, k = x.shape
    tm = min(m, 1024)
    row = lambda i: (i, 0)
    fixed = lambda i: (0, 0)
    return pl.pallas_call(
        _kiwi_kernel,
        grid=(m // tm,),
        in_specs=[pl.BlockSpec((tm, k), row), pl.BlockSpec((k, HEAD_DIM), fixed),
                  pl.BlockSpec((HEAD_DIM, k), fixed),
                  pl.BlockSpec((tm, HEAD_DIM), row), pl.BlockSpec((tm, HEAD_DIM), row),
                  pl.BlockSpec((1, HEAD_DIM), fixed), pl.BlockSpec((1, HEAD_DIM), fixed)],
        out_specs=[pl.BlockSpec((tm, HEAD_DIM), row), pl.BlockSpec((HEAD_DIM, tm), lambda i: (0, i))],
        out_shape=[jax.ShapeDtypeStruct((m, HEAD_DIM), BF16),
                   jax.ShapeDtypeStruct((HEAD_DIM, m), F32)],
        scratch_shapes=[pltpu.VMEM((k, HEAD_DIM), BF16), pltpu.VMEM((HEAD_DIM, k), BF16)],
        compiler_params=_params(1),
        name="dsa_kiwi_proj",
    )(x, w_ki, w_wi_t, cosf, sinf, gain.reshape(1, HEAD_DIM), bias.reshape(1, HEAD_DIM))


def _ffn_up_kernel(x_ref, wg_ref, wu_ref, h_ref, wgbf_ref, wubf_ref):
    @pl.when(pl.program_id(1) == 0)
    def _():
        wgbf_ref[...] = wg_ref[...].astype(BF16)
        wubf_ref[...] = wu_ref[...].astype(BF16)

    x = x_ref[...]
    g = jnp.dot(x, wgbf_ref[...], preferred_element_type=F32)
    u = jnp.dot(x, wubf_ref[...], preferred_element_type=F32)
    h_ref[...] = (g * (1.0 / (1.0 + jnp.exp(-g))) * u).astype(h_ref.dtype)


def _ffn_up(x, w_gate, w_up, *, tn=256):
    m, k = x.shape
    n = w_gate.shape[1]
    tm = min(m, 1024)
    assert n % tn == 0
    wspec = pl.BlockSpec((k, tn), lambda j, i: (0, j))
    return pl.pallas_call(
        _ffn_up_kernel,
        grid=(n // tn, m // tm),
        in_specs=[pl.BlockSpec((tm, k), lambda j, i: (i, 0)), wspec, wspec],
        out_specs=pl.BlockSpec((tm, tn), lambda j, i: (i, j)),
        out_shape=jax.ShapeDtypeStruct((m, n), BF16),
        scratch_shapes=[pltpu.VMEM((k, tn), BF16)] * 2,
        compiler_params=_params(2),
        name="ffn_up",
    )(x, w_gate, w_up)


def _res_ln_kernel(x_ref, y_ref, g_ref, b_ref, o_ref, *maybe_obf_ref):
    z = DEEPNORM_ALPHA * x_ref[...] + y_ref[...].astype(F32)
    mu = jnp.mean(z, axis=-1, keepdims=True)
    d = z - mu
    var = jnp.mean(jnp.square(d), axis=-1, keepdims=True)
    out = d * lax.rsqrt(var + LN_EPS) * g_ref[...] + b_ref[...]
    o_ref[...] = out
    for obf_ref in maybe_obf_ref:
        obf_ref[...] = out.astype(BF16)


def _res_ln(x, y, gain, bias, *, with_bf16=True):
    m, d = x.shape
    tm = min(m, 256)
    row = lambda i: (i, 0)
    fixed = lambda i: (0, 0)
    n_out = 2 if with_bf16 else 1
    return pl.pallas_call(
        _res_ln_kernel,
        grid=(m // tm,),
        in_specs=[pl.BlockSpec((tm, d), row), pl.BlockSpec((tm, d), row),
                  pl.BlockSpec((1, d), fixed), pl.BlockSpec((1, d), fixed)],
        out_specs=[pl.BlockSpec((tm, d), row)] * n_out,
        out_shape=[jax.ShapeDtypeStruct((m, d), F32), jax.ShapeDtypeStruct((m, d), BF16)][:n_out],
        compiler_params=_params(1),
        name="residual_layernorm",
    )(x, y, gain.reshape(1, d), bias.reshape(1, d))


def _dil_attn_kernel(q_ref, kp_ref, kc_ref, vp_ref, vc_ref, o_ref, lse_ref):
    has_prev = pl.program_id(1) > 0
    qi = lax.broadcasted_iota(I32, (BAND, BAND), 0)
    kj = lax.broadcasted_iota(I32, (BAND, BAND), 1)
    mask_p = jnp.logical_and(kj >= qi, has_prev)
    mask_c = kj <= qi
    lse_ref[...] = jnp.zeros_like(lse_ref)
    heads = [slice(h * HEAD_DIM, (h + 1) * HEAD_DIM) for h in range(DIL_HEADS)]
    scores = [(_dot_nt(q_ref[:, sl], kp_ref[:, sl]), _dot_nt(q_ref[:, sl], kc_ref[:, sl])) for sl in heads]
    probs = []
    for h, (sp, sc) in enumerate(scores):
        sp = jnp.where(mask_p, sp, -jnp.inf)
        sc = jnp.where(mask_c, sc, -jnp.inf)
        m = jnp.max(jnp.maximum(sp, sc), axis=1, keepdims=True)
        pp = jnp.exp2(sp - m)
        pc = jnp.exp2(sc - m)
        l = jnp.sum(pp + pc, axis=1, keepdims=True)
        lse_ref[:, h:h + 1] = m + jnp.log2(l)
        probs.append((pp.astype(BF16), pc.astype(BF16), 1.0 / l))
    for sl, (pp, pc, inv_l) in zip(heads, probs):
        o = (jnp.dot(pp, vp_ref[:, sl], preferred_element_type=F32)
             + jnp.dot(pc, vc_ref[:, sl], preferred_element_type=F32))
        o_ref[:, sl] = (o * inv_l).astype(o_ref.dtype)


def _dilated_attention(qk, v, dilation, batch, seq):
    d = dilation
    n_chunks = seq // (BAND * d)
    cur = lambda col: (lambda b, ch, r: ((b * n_chunks + ch) * d + r, col))
    prev = lambda col: (lambda b, ch, r: ((b * n_chunks + jnp.maximum(ch - 1, 0)) * d + r, col))
    blk = (BAND, DIL_COLS)
    return pl.pallas_call(
        _dil_attn_kernel,
        grid=(batch, n_chunks, d),
        in_specs=[pl.BlockSpec(blk, cur(0)), pl.BlockSpec(blk, prev(1)), pl.BlockSpec(blk, cur(1)),
                  pl.BlockSpec(blk, prev(0)), pl.BlockSpec(blk, cur(0))],
        out_specs=[pl.BlockSpec(blk, cur(0)), pl.BlockSpec((BAND, HEAD_DIM), cur(0))],
        out_shape=[jax.ShapeDtypeStruct((batch * seq, DIL_COLS), BF16),
                   jax.ShapeDtypeStruct((batch * seq, HEAD_DIM), F32)],
        compiler_params=_params(3),
        name=f"dilated_attention_d{d}",
    )(qk, qk, qk, v, v)


def _dil_merge_kernel(o0_ref, o1_ref, o2_ref, l0_ref, l1_ref, l2_ref, out_ref):
    l0, l1, l2 = l0_ref[...], l1_ref[...], l2_ref[...]
    mx = jnp.maximum(jnp.maximum(l0, l1), l2)
    e0, e1, e2 = jnp.exp2(l0 - mx), jnp.exp2(l1 - mx), jnp.exp2(l2 - mx)
    inv = 1.0 / (e0 + e1 + e2)
    w0, w1, w2 = e0 * inv, e1 * inv, e2 * inv
    for h in range(DIL_HEADS):
        sl = slice(h * HEAD_DIM, (h + 1) * HEAD_DIM)
        col = slice(h, h + 1)
        acc = (w0[:, col] * o0_ref[:, sl].astype(F32) + w1[:, col] * o1_ref[:, sl].astype(F32)
               + w2[:, col] * o2_ref[:, sl].astype(F32))
        out_ref[:, sl] = acc.astype(out_ref.dtype)


def _dil_merge(outs, lses):
    m = outs[0].shape[0]
    tm = min(m, 512)
    row = lambda i: (i, 0)
    return pl.pallas_call(
        _dil_merge_kernel,
        grid=(m // tm,),
        in_specs=[pl.BlockSpec((tm, DIL_COLS), row)] * 3 + [pl.BlockSpec((tm, HEAD_DIM), row)] * 3,
        out_specs=pl.BlockSpec((tm, DIL_COLS), row),
        out_shape=jax.ShapeDtypeStruct((m, DIL_COLS), BF16),
        compiler_params=_params(1),
        name="dilated_merge",
    )(*outs, *lses)


def _sortable(x):
    b = lax.bitcast_convert_type(x, I32)
    return b ^ ((b >> 31) & np.int32(0x7FFFFFFF))


def _dsa_select_kernel(qi_ref, wit_ref, ki_ref, bias_ref, qs_ref, keys_ref, *, n_chunks):
    qb = pl.program_id(1)
    n_c = (qb * QUERY_BLOCK + QUERY_BLOCK - 1) // KEY_CHUNK + 1
    for h in range(IDX_HEADS):
        qs_ref[h * QUERY_BLOCK:(h + 1) * QUERY_BLOCK, :] = qi_ref[:, h * HEAD_DIM:(h + 1) * HEAD_DIM]
    w = wit_ref[...]
    kpos = lax.broadcasted_iota(I32, (KEY_CHUNK, QUERY_BLOCK), 0)
    t = qb * QUERY_BLOCK + lax.broadcasted_iota(I32, (KEY_CHUNK, QUERY_BLOCK), 1)
    rows_per_dot = IDX_HEADS_PER_DOT * QUERY_BLOCK

    def score_chunk(c, carry):
        kic = ki_ref[0, pl.ds(pl.multiple_of(c * KEY_CHUNK, KEY_CHUNK), KEY_CHUNK), :]
        sc = jnp.zeros((KEY_CHUNK, QUERY_BLOCK), F32)
        for hg in range(IDX_HEADS // IDX_HEADS_PER_DOT):
            r = _dot_nt(kic, qs_ref[hg * rows_per_dot:(hg + 1) * rows_per_dot, :])
            for j in range(IDX_HEADS_PER_DOT):
                h = hg * IDX_HEADS_PER_DOT + j
                sc = sc + jnp.maximum(r[:, j * QUERY_BLOCK:(j + 1) * QUERY_BLOCK], 0.0) * w[h:h + 1, :]
        causal = c * KEY_CHUNK + kpos <= t
        keys_ref[c] = jnp.where(causal, _sortable(sc), INT_MIN)
        return carry

    lax.fori_loop(0, n_c, score_chunk, 0)

    def count_ge(cand):
        def body(c, acc):
            x = jnp.where(keys_ref[c] >= cand, 1, 0).astype(I32)
            return acc + jnp.sum(x.reshape(KEY_CHUNK // 8, 8, QUERY_BLOCK), axis=0)
        acc = lax.fori_loop(0, n_c, body, jnp.zeros((8, QUERY_BLOCK), I32))
        return jnp.sum(acc, axis=0, keepdims=True)

    zero = jnp.zeros((1, QUERY_BLOCK), I32)
    thr = jnp.where(count_ge(zero) >= TOPK, zero, jnp.full((1, QUERY_BLOCK), INT_MIN, I32))

    def bit_step(i, thr):
        cand = thr | jnp.left_shift(jnp.int32(1), 30 - i)
        return jnp.where(count_ge(cand) >= TOPK, cand, thr)

    thr = lax.fori_loop(0, 31, bit_step, thr)
    thr = jnp.maximum(thr, INT_MIN + 1)

    def write_bias(c, carry):
        bias_ref[0, 0, c] = jnp.where(keys_ref[c] >= thr, 0.0, MASK_NEG).astype(bias_ref.dtype)
        return carry

    def write_masked(c, carry):
        bias_ref[0, 0, c] = jnp.full((KEY_CHUNK, QUERY_BLOCK), MASK_NEG, bias_ref.dtype)
        return carry

    lax.fori_loop(0, n_c, write_bias, 0)
    lax.fori_loop(n_c, n_chunks, write_masked, 0)


def _dsa_select(qi, wit, ki, batch, seq):
    nqb = seq // QUERY_BLOCK
    nkc = seq // KEY_CHUNK
    return pl.pallas_call(
        functools.partial(_dsa_select_kernel, n_chunks=nkc),
        grid=(batch, nqb),
        in_specs=[pl.BlockSpec((QUERY_BLOCK, IDX_HEADS * HEAD_DIM), lambda b, q: (b * nqb + q, 0)),
                  pl.BlockSpec((HEAD_DIM, QUERY_BLOCK), lambda b, q: (0, b * nqb + q)),
                  pl.BlockSpec((1, seq, HEAD_DIM), lambda b, q: (b, 0, 0))],
        out_specs=pl.BlockSpec((1, 1, nkc, KEY_CHUNK, QUERY_BLOCK), lambda b, q: (b, q, 0, 0, 0)),
        out_shape=jax.ShapeDtypeStruct((batch, nqb, nkc, KEY_CHUNK, QUERY_BLOCK), BF16),
        scratch_shapes=[pltpu.VMEM((IDX_HEADS * QUERY_BLOCK, HEAD_DIM), BF16),
                        pltpu.VMEM((nkc, KEY_CHUNK, QUERY_BLOCK), I32)],
        compiler_params=_params(2),
        name="dsa_select",
    )(qi, wit, ki.reshape(batch, seq, HEAD_DIM))


def _dsa_attn_kernel(qb_tab, c_tab, q_ref, k_ref, v_ref, bias_ref, o_ref, qa_ref, m_ref, l_ref, acc_ref):
    step = pl.program_id(1)
    qb = qb_tab[step]
    c = c_tab[step]
    last = (qb * QUERY_BLOCK + QUERY_BLOCK - 1) // KEY_CHUNK
    n_tiles = KEY_CHUNK // HEAD_DIM

    @pl.when(c == 0)
    def _():
        ri = lax.broadcasted_iota(I32, (QUERY_BLOCK, HEAD_DIM), 0)
        ci = lax.broadcasted_iota(I32, (QUERY_BLOCK, HEAD_DIM), 1)
        eye = jnp.where(ri == ci, 1.0, 0.0).astype(BF16)
        for g in range(DSA_KV_HEADS):
            for j in range(DSA_GROUP):
                h = g * DSA_GROUP + j
                rows = slice(j * QUERY_BLOCK, (j + 1) * QUERY_BLOCK)
                qa_ref[g, rows, :HEAD_DIM] = q_ref[:, h * HEAD_DIM:(h + 1) * HEAD_DIM]
                qa_ref[g, rows, HEAD_DIM:] = eye
        m_ref[...] = jnp.full(m_ref.shape, M_INIT, F32)
        l_ref[...] = jnp.zeros_like(l_ref)
        acc_ref[...] = jnp.zeros_like(acc_ref)

    bias_t = bias_ref[0, 0, 0]

    def scores(g):
        ka = jnp.concatenate([k_ref[:, g * HEAD_DIM:(g + 1) * HEAD_DIM], bias_t], axis=1)
        return _dot_nt(qa_ref[g], ka)

    s_next = scores(0)
    for g in range(DSA_KV_HEADS):
        sl = slice(g * HEAD_DIM, (g + 1) * HEAD_DIM)
        s = s_next
        if g + 1 < DSA_KV_HEADS:
            s_next = scores(g + 1)
        tiles = [s[:, j * HEAD_DIM:(j + 1) * HEAD_DIM] for j in range(n_tiles)]
        m_old = m_ref[g]
        m_new = jnp.maximum(m_old, jnp.max(functools.reduce(jnp.maximum, tiles), axis=1, keepdims=True))
        alpha = jnp.exp2(m_old - m_new)
        ps = [jnp.exp2(tl - m_new) for tl in tiles]
        l_ref[g] = alpha * l_ref[g] + jnp.sum(functools.reduce(jnp.add, ps), axis=1, keepdims=True)
        p = jnp.concatenate([tl.astype(BF16) for tl in ps], axis=1)
        acc_ref[g] = alpha * acc_ref[g] + jnp.dot(p, v_ref[:, sl], preferred_element_type=F32)
        m_ref[g] = m_new

    @pl.when(c == last)
    def _():
        for g in range(DSA_KV_HEADS):
            o = acc_ref[g] / l_ref[g]
            for j in range(DSA_GROUP):
                h = g * DSA_GROUP + j
                o_ref[:, h * HEAD_DIM:(h + 1) * HEAD_DIM] = (
                    o[j * QUERY_BLOCK:(j + 1) * QUERY_BLOCK].astype(o_ref.dtype))


def _dsa_attention(q, k, v, bias_t, batch, seq):
    nqb = seq // QUERY_BLOCK
    nkc = seq // KEY_CHUNK
    rows = DSA_GROUP * QUERY_BLOCK
    steps = [(qb, c) for qb in range(nqb)
             for c in range((qb * QUERY_BLOCK + QUERY_BLOCK - 1) // KEY_CHUNK + 1)]
    qb_tab = jnp.asarray(np.array([s[0] for s in steps], np.int32))
    c_tab = jnp.asarray(np.array([s[1] for s in steps], np.int32))
    qmap = lambda b, s, qt, ct: (b * nqb + qt[s], 0)
    kvmap = lambda b, s, qt, ct: (b * nkc + ct[s], 0)
    grid_spec = pltpu.PrefetchScalarGridSpec(
        num_scalar_prefetch=2,
        grid=(batch, len(steps)),
        in_specs=[pl.BlockSpec((QUERY_BLOCK, DSA_HEADS * HEAD_DIM), qmap),
                  pl.BlockSpec((KEY_CHUNK, DSA_KV_HEADS * HEAD_DIM), kvmap),
                  pl.BlockSpec((KEY_CHUNK, DSA_KV_HEADS * HEAD_DIM), kvmap),
                  pl.BlockSpec((1, 1, 1, KEY_CHUNK, QUERY_BLOCK),
                               lambda b, s, qt, ct: (b, qt[s], ct[s], 0, 0))],
        out_specs=pl.BlockSpec((QUERY_BLOCK, DSA_HEADS * HEAD_DIM), qmap),
        scratch_shapes=[pltpu.VMEM((DSA_KV_HEADS, rows, 2 * HEAD_DIM), BF16),
                        pltpu.VMEM((DSA_KV_HEADS, rows, HEAD_DIM), F32),
                        pltpu.VMEM((DSA_KV_HEADS, rows, HEAD_DIM), F32),
                        pltpu.VMEM((DSA_KV_HEADS, rows, HEAD_DIM), F32)])
    return pl.pallas_call(
        _dsa_attn_kernel,
        grid_spec=grid_spec,
        out_shape=jax.ShapeDtypeStruct((batch * seq, DSA_HEADS * HEAD_DIM), BF16),
        compiler_params=_params(2),
        name="dsa_attention",
    )(qb_tab, c_tab, q, k, v, bias_t)


def kernel(x, positions, l0_attn_w_in, l0_attn_w_out, l1_attn_w_in, l1_k_idx_gain, l1_k_idx_bias,
           l1_attn_w_out, l0_ln_mix_gain, l0_ln_mix_bias, l0_ffn_gate, l0_ffn_up, l0_ffn_down,
           l0_ln_ffn_gain, l0_ln_ffn_bias, l1_ln_mix_gain, l1_ln_mix_bias, l1_ffn_gate, l1_ffn_up,
           l1_ffn_down, l1_ln_ffn_gain, l1_ln_ffn_bias):
    batch, seq, d_model = x.shape
    m = batch * seq
    cosf, sinf = _rope_tables(positions)
    rope = (cosf, sinf)
    xf = x.reshape(m, d_model)
    xb = xf.astype(BF16)

    tn = 512
    n_grp = len(DIL_GROUPS)
    grp_blocks = DIL_COLS // tn
    outs, lses = [], []
    for g, (_, d) in enumerate(DIL_GROUPS):
        col_block = lambda n, p0=0, g=g: ((p0 + n // grp_blocks) * n_grp + g) * grp_blocks + n % grp_blocks
        rope_g = tuple(_to_residue_major(t, d, batch, seq) for t in rope)
        x_g = _to_residue_major(xb, d, batch, seq)
        qk = _matmul(x_g, l0_attn_w_in, col_block, 2 * DIL_COLS, BF16, tn=tn, rope=rope_g,
                     scale=ATTN_SCALE * LOG2E, scale_cols=DIL_COLS, name=f"l0_qk_proj_d{d}")
        v = _matmul(x_g, l0_attn_w_in, functools.partial(col_block, p0=2), DIL_COLS, BF16, tn=tn,
                    name=f"l0_v_proj_d{d}")
        o, lse = _dilated_attention(qk, v, d, batch, seq)
        outs.append(_from_residue_major(o, d, batch, seq))
        lses.append(_from_residue_major(lse, d, batch, seq))
    o0 = _dil_merge(outs, lses)
    same = lambda n: n
    y = _matmul(o0, l0_attn_w_out, same, d_model, BF16, tn=tn, name="l0_out_proj")
    xf, xb = _res_ln(xf, y, l0_ln_mix_gain, l0_ln_mix_bias)
    h = _ffn_up(xb, l0_ffn_gate, l0_ffn_up)
    y = _matmul(h, l0_ffn_down.astype(BF16), same, d_model, BF16, tn=tn, tm=512, name="l0_ffn_down")
    xf, xb = _res_ln(xf, y, l0_ln_ffn_gain, l0_ln_ffn_bias)

    b_q = DSA_HEADS * HEAD_DIM
    b_kv = DSA_KV_HEADS * HEAD_DIM
    b_qi = IDX_HEADS * HEAD_DIM
    o_v = b_q + b_kv
    o_qi = o_v + b_kv
    o_ki = o_qi + b_qi
    o_wi = o_ki + HEAD_DIM
    at = lambda col0: (lambda n: n + col0 // tn)
    q1 = _matmul(xb, l1_attn_w_in, at(0), b_q, BF16, tn=tn, rope=rope, scale=ATTN_SCALE * LOG2E,
                 scale_cols=b_q, name="l1_q_proj")
    k1 = _matmul(xb, l1_attn_w_in, at(b_q), b_kv, BF16, tn=tn, rope=rope, name="l1_k_proj")
    v1 = _matmul(xb, l1_attn_w_in, at(o_v), b_kv, BF16, tn=tn, name="l1_v_proj")
    qi = _matmul(xb, l1_attn_w_in, at(o_qi), b_qi, BF16, tn=tn, rope=rope, name="l1_qidx_proj")
    w_ki = l1_attn_w_in[:, o_ki:o_wi]
    w_wi_t = jnp.pad(l1_attn_w_in[:, o_wi:].T, ((0, HEAD_DIM - IDX_HEADS), (0, 0)))
    ki, wit = _kiwi(xb, w_ki, w_wi_t, cosf, sinf, l1_k_idx_gain, l1_k_idx_bias)
    bias_t = _dsa_select(qi, wit, ki, batch, seq)
    o1 = _dsa_attention(q1, k1, v1, bias_t, batch, seq)
    y = _matmul(o1, l1_attn_w_out, same, d_model, BF16, tn=tn, name="l1_out_proj")
    xf, xb = _res_ln(xf, y, l1_ln_mix_gain, l1_ln_mix_bias)
    h = _ffn_up(xb, l1_ffn_gate, l1_ffn_up)
    y = _matmul(h, l1_ffn_down.astype(BF16), same, d_model, BF16, tn=tn, tm=512, name="l1_ffn_down")
    (xf,) = _res_ln(xf, y, l1_ln_ffn_gain, l1_ln_ffn_bias, with_bf16=False)
    return xf.reshape(batch, seq, d_model)
```

```python
import functools
import math

import numpy as np
import jax
import jax.numpy as jnp
from jax import lax
from jax.experimental import pallas as pl
from jax.experimental.pallas import tpu as pltpu

F32 = jnp.float32
BF16 = jnp.bfloat16
I32 = jnp.int32

HEAD_DIM = 128
HALF = HEAD_DIM // 2
ROPE_THETA = 10000.0
LN_EPS = 1e-5
DEPTH = 2
DEEPNORM_ALPHA = (2 * DEPTH) ** 0.25
ATTN_SCALE = HEAD_DIM ** -0.5
LOG2E = math.log2(math.e)

DIL_GROUPS = ((128, 1), (512, 4), (2048, 16))
BAND = 128
DIL_HEADS = 16
DIL_COLS = DIL_HEADS * HEAD_DIM

DSA_HEADS = 32
DSA_KV_HEADS = 8
DSA_GROUP = DSA_HEADS // DSA_KV_HEADS
IDX_HEADS = 32
IDX_HEADS_PER_DOT = 4
TOPK = 256
QUERY_BLOCK = 128
KEY_CHUNK = 512
ATT_KEY_CHUNK = 1024

INT_MIN = np.int32(-2 ** 31)
MASK_NEG = -1e30
M_INIT = -1e29

VMEM_LIMIT = 56 * 1024 * 1024


def _params(n_axes):
    return pltpu.CompilerParams(dimension_semantics=("arbitrary",) * n_axes,
                                vmem_limit_bytes=VMEM_LIMIT)


def _dot_nt(a, b):
    return lax.dot_general(a, b, (((1,), (1,)), ((), ())), preferred_element_type=F32)


def _rope(a, c, s):
    return a * c + pltpu.roll(a, HALF, 1) * s


def _to_residue_major(t, d, batch, seq):
    if d == 1:
        return t
    c = t.shape[1]
    t = t.reshape(batch, seq // (BAND * d), BAND, d, c)
    return t.transpose(0, 1, 3, 2, 4).reshape(batch * seq, c)


def _from_residue_major(t, d, batch, seq):
    if d == 1:
        return t
    c = t.shape[1]
    t = t.reshape(batch, seq // (BAND * d), d, BAND, c)
    return t.transpose(0, 1, 3, 2, 4).reshape(batch * seq, c)


def _cast_residue_major_kernel(x_ref, *o_refs, dilations):
    rows = x_ref.shape[0]
    for o_ref, d in zip(o_refs, dilations):
        if d == 1:
            o_ref[...] = x_ref[...].astype(BF16)
            continue
        span = BAND * d
        for base in range(0, rows, span):
            for r in range(d):
                o_ref[base + r * BAND:base + (r + 1) * BAND, :] = (
                    x_ref[pl.ds(base + r, BAND, stride=d), :].astype(BF16))


def _cast_residue_major(x, dilations):
    m, dm = x.shape
    cols = HEAD_DIM
    rows = BAND * max(dilations)
    rows *= 2 if m % (2 * rows) == 0 else 1
    assert m % rows == 0 and dm % cols == 0 and all(rows % (BAND * d) == 0 for d in dilations)
    blk = pl.BlockSpec((rows, cols), lambda i, j: (i, j))
    return pl.pallas_call(
        functools.partial(_cast_residue_major_kernel, dilations=tuple(dilations)),
        grid=(m // rows, dm // cols),
        in_specs=[blk],
        out_specs=[blk] * len(dilations),
        out_shape=[jax.ShapeDtypeStruct((m, dm), BF16)] * len(dilations),
        compiler_params=_params(2),
        name="cast_residue_major",
    )(x)


def _rope_table_kernel(pos_ref, inv_ref, cos_ref, sin_ref):
    ang = pos_ref[...].astype(F32) * inv_ref[...]
    lane = lax.broadcasted_iota(I32, ang.shape, 1)
    cos_ref[...] = jnp.cos(ang)
    sin_ref[...] = jnp.where(lane < HALF, -jnp.sin(ang), jnp.sin(ang))


def _rope_tables(positions):
    m = positions.size
    tm = min(m, 1024)
    inv = ROPE_THETA ** (-jnp.arange(0, HEAD_DIM, 2, dtype=F32) / HEAD_DIM)
    inv2 = jnp.concatenate([inv, inv]).reshape(1, HEAD_DIM)
    return pl.pallas_call(
        _rope_table_kernel,
        grid=(m // tm,),
        in_specs=[pl.BlockSpec((tm, 1), lambda i: (i, 0)),
                  pl.BlockSpec((1, HEAD_DIM), lambda i: (0, 0))],
        out_specs=[pl.BlockSpec((tm, HEAD_DIM), lambda i: (i, 0))] * 2,
        out_shape=[jax.ShapeDtypeStruct((m, HEAD_DIM), F32)] * 2,
        compiler_params=_params(1),
        name="rope_tables",
    )(positions.reshape(m, 1), inv2)


MXU_WIDTH = 256


def _mm_kernel(x_ref, w_ref, *rest, w_is_f32, has_rope, n_blocks, scale_blocks, scale):
    rest = list(rest)
    wbf_ref = rest.pop() if w_is_f32 else w_ref
    o_ref = rest.pop()
    if w_is_f32:
        @pl.when(pl.program_id(1) == 0)
        def _():
            wbf_ref[...] = w_ref[...].astype(BF16)

    x = x_ref[...]
    strips = [slice(j, j + MXU_WIDTH) for j in range(0, o_ref.shape[1], MXU_WIDTH)]
    accs = [jnp.dot(x, wbf_ref[:, st], preferred_element_type=F32) for st in strips]
    if not has_rope:
        for st, acc in zip(strips, accs):
            o_ref[:, st] = acc.astype(o_ref.dtype)
        return
    cos_ref, sin_ref = rest
    c, s = cos_ref[...], sin_ref[...]
    if scale_blocks:
        factor = scale if scale_blocks == n_blocks else jnp.where(pl.program_id(0) < scale_blocks, scale, 1.0)
        c, s = c * factor, s * factor
    for st, acc in zip(strips, accs):
        for j in range(0, MXU_WIDTH, HEAD_DIM):
            o_ref[:, st.start + j:st.start + j + HEAD_DIM] = (
                _rope(acc[:, j:j + HEAD_DIM], c, s).astype(o_ref.dtype))


def _matmul(x, w, col_block, n_cols, out_dtype, *, tn, tm=1024, rope=None, scale=1.0, scale_cols=0, name):
    m, k = x.shape
    tm = min(m, tm)
    assert m % tm == 0 and n_cols % tn == 0 and tn % MXU_WIDTH == 0 and scale_cols % tn == 0
    n_blocks = n_cols // tn
    w_is_f32 = w.dtype == F32
    w_mode = {} if w_is_f32 else {"pipeline_mode": pl.Buffered(1)}
    in_specs = [pl.BlockSpec((tm, k), lambda n, i: (i, 0)),
                pl.BlockSpec((k, tn), lambda n, i: (0, col_block(n)), **w_mode)]
    args = [x, w]
    if rope is not None:
        in_specs += [pl.BlockSpec((tm, HEAD_DIM), lambda n, i: (i, 0))] * 2
        args += list(rope)
    return pl.pallas_call(
        functools.partial(_mm_kernel, w_is_f32=w_is_f32, has_rope=rope is not None,
                          n_blocks=n_blocks, scale_blocks=scale_cols // tn, scale=scale),
        grid=(n_blocks, m // tm),
        in_specs=in_specs,
        out_specs=pl.BlockSpec((tm, tn), lambda n, i: (i, n)),
        out_shape=jax.ShapeDtypeStruct((m, n_cols), out_dtype),
        scratch_shapes=[pltpu.VMEM((k, tn), BF16)] if w_is_f32 else [],
        compiler_params=_params(2),
        name=name,
    )(*args)


def _kiwi_kernel(x_ref, wk_ref, wwt_ref, cos_ref, sin_ref, g_ref, b_ref, ki_ref, wit_ref, wkbf_ref, wwbf_ref):
    @pl.when(pl.program_id(0) == 0)
    def _():
        wkbf_ref[...] = wk_ref[...].astype(BF16)
        wwbf_ref[...] = wwt_ref[...].astype(BF16)

    x = x_ref[...]
    a = jnp.dot(x, wkbf_ref[...], preferred_element_type=F32)
    mu = jnp.mean(a, axis=-1, keepdims=True)
    var = jnp.mean(jnp.square(a - mu), axis=-1, keepdims=True)
    y = (a - mu) * lax.rsqrt(var + LN_EPS) * g_ref[...] + b_ref[...]
    ki_ref[...] = _rope(y, cos_ref[...], sin_ref[...]).astype(ki_ref.dtype)
    wit_ref[...] = _dot_nt(wwbf_ref[...], x) * (IDX_HEADS ** -0.5 * HEAD_DIM ** -0.5)


def _kiwi(x, w_ki, w_wi_t, cosf, sinf, gain, bias):
    m, k = x.shape
    tm = min(m, 1024)
    row = lambda i: (i, 0)
    fixed = lambda i: (0, 0)
    return pl.pallas_call(
        _kiwi_kernel,
        grid=(m // tm,),
        in_specs=[pl.BlockSpec((tm, k), row), pl.BlockSpec((k, HEAD_DIM), fixed),
                  pl.BlockSpec((HEAD_DIM, k), fixed),
                  pl.BlockSpec((tm, HEAD_DIM), row), pl.BlockSpec((tm, HEAD_DIM), row),
                  pl.BlockSpec((1, HEAD_DIM), fixed), pl.BlockSpec((1, HEAD_DIM), fixed)],
        out_specs=[pl.BlockSpec((tm, HEAD_DIM), row), pl.BlockSpec((HEAD_DIM, tm), lambda i: (0, i))],
        out_shape=[jax.ShapeDtypeStruct((m, HEAD_DIM), BF16),
                   jax.ShapeDtypeStruct((HEAD_DIM, m), F32)],
        scratch_shapes=[pltpu.VMEM((k, HEAD_DIM), BF16), pltpu.VMEM((HEAD_DIM, k), BF16)],
        compiler_params=_params(1),
        name="dsa_kiwi_proj",
    )(x, w_ki, w_wi_t, cosf, sinf, gain.reshape(1, HEAD_DIM), bias.reshape(1, HEAD_DIM))


def _ffn_up_kernel(x_ref, wg_ref, wu_ref, h_ref, wgbf_ref, wubf_ref):
    @pl.when(pl.program_id(1) == 0)
    def _():
        wgbf_ref[...] = wg_ref[...].astype(BF16)
        wubf_ref[...] = wu_ref[...].astype(BF16)

    x = x_ref[...]
    g = jnp.dot(x, wgbf_ref[...], preferred_element_type=F32)
    u = jnp.dot(x, wubf_ref[...], preferred_element_type=F32)
    h_ref[...] = (g * (1.0 / (1.0 + jnp.exp(-g))) * u).astype(h_ref.dtype)


def _ffn_up(x, w_gate, w_up, *, tn=256):
    m, k = x.shape
    n = w_gate.shape[1]
    tm = min(m, 1024)
    assert n % tn == 0
    wspec = pl.BlockSpec((k, tn), lambda j, i: (0, j))
    return pl.pallas_call(
        _ffn_up_kernel,
        grid=(n // tn, m // tm),
        in_specs=[pl.BlockSpec((tm, k), lambda j, i: (i, 0)), wspec, wspec],
        out_specs=pl.BlockSpec((tm, tn), lambda j, i: (i, j)),
        out_shape=jax.ShapeDtypeStruct((m, n), BF16),
        scratch_shapes=[pltpu.VMEM((k, tn), BF16)] * 2,
        compiler_params=_params(2),
        name="ffn_up",
    )(x, w_gate, w_up)


def _res_ln_kernel(x_ref, y_ref, g_ref, b_ref, o_ref, *maybe_obf_ref):
    z = DEEPNORM_ALPHA * x_ref[...] + y_ref[...].astype(F32)
    mu = jnp.mean(z, axis=-1, keepdims=True)
    d = z - mu
    var = jnp.mean(jnp.square(d), axis=-1, keepdims=True)
    out = d * lax.rsqrt(var + LN_EPS) * g_ref[...] + b_ref[...]
    o_ref[...] = out
    for obf_ref in maybe_obf_ref:
        obf_ref[...] = out.astype(BF16)


def _res_ln(x, y, gain, bias, *, with_bf16=True):
    m, d = x.shape
    tm = min(m, 256)
    row = lambda i: (i, 0)
    fixed = lambda i: (0, 0)
    n_out = 2 if with_bf16 else 1
    return pl.pallas_call(
        _res_ln_kernel,
        grid=(m // tm,),
        in_specs=[pl.BlockSpec((tm, d), row), pl.BlockSpec((tm, d), row),
                  pl.BlockSpec((1, d), fixed), pl.BlockSpec((1, d), fixed)],
        out_specs=[pl.BlockSpec((tm, d), row)] * n_out,
        out_shape=[jax.ShapeDtypeStruct((m, d), F32), jax.ShapeDtypeStruct((m, d), BF16)][:n_out],
        compiler_params=_params(1),
        name="residual_layernorm",
    )(x, y, gain.reshape(1, d), bias.reshape(1, d))


def _dil_attn_kernel(q_ref, kp_ref, kc_ref, vp_ref, vc_ref, o_ref, lse_ref):
    has_prev = pl.program_id(1) > 0
    qi = lax.broadcasted_iota(I32, (BAND, BAND), 0)
    kj = lax.broadcasted_iota(I32, (BAND, BAND), 1)
    mask_p = jnp.logical_and(kj >= qi, has_prev)
    mask_c = kj <= qi
    lse_ref[...] = jnp.zeros_like(lse_ref)
    heads = [slice(h * HEAD_DIM, (h + 1) * HEAD_DIM) for h in range(DIL_HEADS)]
    scores = [(_dot_nt(q_ref[:, sl], kp_ref[:, sl]), _dot_nt(q_ref[:, sl], kc_ref[:, sl])) for sl in heads]
    probs = []
    for h, (sp, sc) in enumerate(scores):
        sp = jnp.where(mask_p, sp, -jnp.inf)
        sc = jnp.where(mask_c, sc, -jnp.inf)
        m = jnp.max(jnp.maximum(sp, sc), axis=1, keepdims=True)
        pp = jnp.exp2(sp - m)
        pc = jnp.exp2(sc - m)
        l = jnp.sum(pp + pc, axis=1, keepdims=True)
        lse_ref[:, h:h + 1] = m + jnp.log2(l)
        probs.append((pp.astype(BF16), pc.astype(BF16), 1.0 / l))
    for sl, (pp, pc, inv_l) in zip(heads, probs):
        o = (jnp.dot(pp, vp_ref[:, sl], preferred_element_type=F32)
             + jnp.dot(pc, vc_ref[:, sl], preferred_element_type=F32))
        o_ref[:, sl] = (o * inv_l).astype(o_ref.dtype)


def _dilated_attention(qk, v, dilation, batch, seq):
    d = dilation
    n_chunks = seq // (BAND * d)
    cur = lambda col: (lambda b, ch, r: ((b * n_chunks + ch) * d + r, col))
    prev = lambda col: (lambda b, ch, r: ((b * n_chunks + jnp.maximum(ch - 1, 0)) * d + r, col))
    blk = (BAND, DIL_COLS)
    return pl.pallas_call(
        _dil_attn_kernel,
        grid=(batch, n_chunks, d),
        in_specs=[pl.BlockSpec(blk, cur(0)), pl.BlockSpec(blk, prev(1)), pl.BlockSpec(blk, cur(1)),
                  pl.BlockSpec(blk, prev(0)), pl.BlockSpec(blk, cur(0))],
        out_specs=[pl.BlockSpec(blk, cur(0)), pl.BlockSpec((BAND, HEAD_DIM), cur(0))],
        out_shape=[jax.ShapeDtypeStruct((batch * seq, DIL_COLS), BF16),
                   jax.ShapeDtypeStruct((batch * seq, HEAD_DIM), F32)],
        compiler_params=_params(3),
        name=f"dilated_attention_d{d}",
    )(qk, qk, qk, v, v)


def _dil_merge_kernel(o0_ref, o1_ref, o2_ref, l0_ref, l1_ref, l2_ref, out_ref):
    l0, l1, l2 = l0_ref[...], l1_ref[...], l2_ref[...]
    mx = jnp.maximum(jnp.maximum(l0, l1), l2)
    e0, e1, e2 = jnp.exp2(l0 - mx), jnp.exp2(l1 - mx), jnp.exp2(l2 - mx)
    inv = 1.0 / (e0 + e1 + e2)
    w0, w1, w2 = e0 * inv, e1 * inv, e2 * inv
    for h in range(DIL_HEADS):
        sl = slice(h * HEAD_DIM, (h + 1) * HEAD_DIM)
        col = slice(h, h + 1)
        acc = (w0[:, col] * o0_ref[:, sl].astype(F32) + w1[:, col] * o1_ref[:, sl].astype(F32)
               + w2[:, col] * o2_ref[:, sl].astype(F32))
        out_ref[:, sl] = acc.astype(out_ref.dtype)


def _dil_merge(outs, lses):
    m = outs[0].shape[0]
    tm = min(m, 512)
    row = lambda i: (i, 0)
    return pl.pallas_call(
        _dil_merge_kernel,
        grid=(m // tm,),
        in_specs=[pl.BlockSpec((tm, DIL_COLS), row)] * 3 + [pl.BlockSpec((tm, HEAD_DIM), row)] * 3,
        out_specs=pl.BlockSpec((tm, DIL_COLS), row),
        out_shape=jax.ShapeDtypeStruct((m, DIL_COLS), BF16),
        compiler_params=_params(1),
        name="dilated_merge",
    )(*outs, *lses)


def _sortable(x):
    b = lax.bitcast_convert_type(x, I32)
    return b ^ ((b >> 31) & np.int32(0x7FFFFFFF))


def _dsa_select_kernel(qi_ref, wit_ref, ki_ref, bias_ref, qs_ref, keys_ref, *, n_chunks):
    qb = pl.program_id(1)
    n_c = (qb * QUERY_BLOCK + QUERY_BLOCK - 1) // KEY_CHUNK + 1
    for h in range(IDX_HEADS):
        qs_ref[h * QUERY_BLOCK:(h + 1) * QUERY_BLOCK, :] = qi_ref[:, h * HEAD_DIM:(h + 1) * HEAD_DIM]
    w = wit_ref[...]
    kpos = lax.broadcasted_iota(I32, (KEY_CHUNK, QUERY_BLOCK), 0)
    t = qb * QUERY_BLOCK + lax.broadcasted_iota(I32, (KEY_CHUNK, QUERY_BLOCK), 1)
    rows_per_dot = IDX_HEADS_PER_DOT * QUERY_BLOCK

    def score_chunk(c, carry):
        kic = ki_ref[0, pl.ds(pl.multiple_of(c * KEY_CHUNK, KEY_CHUNK), KEY_CHUNK), :]
        sc = jnp.zeros((KEY_CHUNK, QUERY_BLOCK), F32)
        for hg in range(IDX_HEADS // IDX_HEADS_PER_DOT):
            r = _dot_nt(kic, qs_ref[hg * rows_per_dot:(hg + 1) * rows_per_dot, :])
            for j in range(IDX_HEADS_PER_DOT):
                h = hg * IDX_HEADS_PER_DOT + j
                sc = sc + jnp.maximum(r[:, j * QUERY_BLOCK:(j + 1) * QUERY_BLOCK], 0.0) * w[h:h + 1, :]
        causal = c * KEY_CHUNK + kpos <= t
        keys_ref[c] = jnp.where(causal, _sortable(sc), INT_MIN)
        return carry

    lax.fori_loop(0, n_c, score_chunk, 0)

    def count_ge(cand):
        def body(c, acc):
            x = jnp.where(keys_ref[c] >= cand, 1, 0).astype(I32)
            return acc + jnp.sum(x.reshape(KEY_CHUNK // 8, 8, QUERY_BLOCK), axis=0)
        acc = lax.fori_loop(0, n_c, body, jnp.zeros((8, QUERY_BLOCK), I32))
        return jnp.sum(acc, axis=0, keepdims=True)

    zero = jnp.zeros((1, QUERY_BLOCK), I32)
    thr = jnp.where(count_ge(zero) >= TOPK, zero, jnp.full((1, QUERY_BLOCK), INT_MIN, I32))

    def bit_step(i, thr):
        cand = thr | jnp.left_shift(jnp.int32(1), 30 - i)
        return jnp.where(count_ge(cand) >= TOPK, cand, thr)

    thr = lax.fori_loop(0, 31, bit_step, thr)
    thr = jnp.maximum(thr, INT_MIN + 1)

    def write_bias(c, carry):
        bias_ref[0, 0, c] = jnp.where(keys_ref[c] >= thr, 0.0, MASK_NEG).astype(bias_ref.dtype)
        return carry

    def write_masked(c, carry):
        bias_ref[0, 0, c] = jnp.full((KEY_CHUNK, QUERY_BLOCK), MASK_NEG, bias_ref.dtype)
        return carry

    lax.fori_loop(0, n_c, write_bias, 0)
    lax.fori_loop(n_c, n_chunks, write_masked, 0)


def _dsa_select(qi, wit, ki, batch, seq):
    nqb = seq // QUERY_BLOCK
    nkc = seq // KEY_CHUNK
    return pl.pallas_call(
        functools.partial(_dsa_select_kernel, n_chunks=nkc),
        grid=(batch, nqb),
        in_specs=[pl.BlockSpec((QUERY_BLOCK, IDX_HEADS * HEAD_DIM), lambda b, q: (b * nqb + q, 0)),
                  pl.BlockSpec((HEAD_DIM, QUERY_BLOCK), lambda b, q: (0, b * nqb + q)),
                  pl.BlockSpec((1, seq, HEAD_DIM), lambda b, q: (b, 0, 0))],
        out_specs=pl.BlockSpec((1, 1, nkc, KEY_CHUNK, QUERY_BLOCK), lambda b, q: (b, q, 0, 0, 0)),
        out_shape=jax.ShapeDtypeStruct((batch, nqb, nkc, KEY_CHUNK, QUERY_BLOCK), BF16),
        scratch_shapes=[pltpu.VMEM((IDX_HEADS * QUERY_BLOCK, HEAD_DIM), BF16),
                        pltpu.VMEM((nkc, KEY_CHUNK, QUERY_BLOCK), I32)],
        compiler_params=_params(2),
        name="dsa_select",
    )(qi, wit, ki.reshape(batch, seq, HEAD_DIM))


def _dsa_attn_kernel(qb_tab, c_tab, q_ref, k_ref, v_ref, bias_ref, o_ref, qa_ref, m_ref, l_ref, acc_ref):
    step = pl.program_id(1)
    qb = qb_tab[step]
    c = c_tab[step]
    last = (qb * QUERY_BLOCK + QUERY_BLOCK - 1) // ATT_KEY_CHUNK
    n_tiles = ATT_KEY_CHUNK // HEAD_DIM

    @pl.when(c == 0)
    def _():
        ri = lax.broadcasted_iota(I32, (QUERY_BLOCK, HEAD_DIM), 0)
        ci = lax.broadcasted_iota(I32, (QUERY_BLOCK, HEAD_DIM), 1)
        eye = jnp.where(ri == ci, 1.0, 0.0).astype(BF16)
        for g in range(DSA_KV_HEADS):
            for j in range(DSA_GROUP):
                h = g * DSA_GROUP + j
                rows = slice(j * QUERY_BLOCK, (j + 1) * QUERY_BLOCK)
                qa_ref[g, rows, :HEAD_DIM] = q_ref[:, h * HEAD_DIM:(h + 1) * HEAD_DIM]
                qa_ref[g, rows, HEAD_DIM:] = eye
        m_ref[...] = jnp.full(m_ref.shape, M_INIT, F32)
        l_ref[...] = jnp.zeros_like(l_ref)
        acc_ref[...] = jnp.zeros_like(acc_ref)

    bias_t = bias_ref[0, 0, 0]

    def scores(g):
        ka = jnp.concatenate([k_ref[:, g * HEAD_DIM:(g + 1) * HEAD_DIM], bias_t], axis=1)
        return _dot_nt(qa_ref[g], ka)

    s_next = scores(0)
    for g in range(DSA_KV_HEADS):
        sl = slice(g * HEAD_DIM, (g + 1) * HEAD_DIM)
        s = s_next
        if g + 1 < DSA_KV_HEADS:
            s_next = scores(g + 1)
        tiles = [s[:, j * HEAD_DIM:(j + 1) * HEAD_DIM] for j in range(n_tiles)]
        m_old = m_ref[g]
        m_new = jnp.maximum(m_old, jnp.max(functools.reduce(jnp.maximum, tiles), axis=1, keepdims=True))
        alpha = jnp.exp2(m_old - m_new)
        ps = [jnp.exp2(tl - m_new) for tl in tiles]
        l_ref[g] = alpha * l_ref[g] + jnp.sum(functools.reduce(jnp.add, ps), axis=1, keepdims=True)
        p = jnp.concatenate([tl.astype(BF16) for tl in ps], axis=1)
        acc_ref[g] = alpha * acc_ref[g] + jnp.dot(p, v_ref[:, sl], preferred_element_type=F32)
        m_ref[g] = m_new

    @pl.when(c == last)
    def _():
        for g in range(DSA_KV_HEADS):
            o = acc_ref[g] / l_ref[g]
            for j in range(DSA_GROUP):
                h = g * DSA_GROUP + j
                o_ref[:, h * HEAD_DIM:(h + 1) * HEAD_DIM] = (
                    o[j * QUERY_BLOCK:(j + 1) * QUERY_BLOCK].astype(o_ref.dtype))


def _dsa_attention(q, k, v, bias_t, batch, seq):
    nqb = seq // QUERY_BLOCK
    nkc = seq // ATT_KEY_CHUNK
    rows = DSA_GROUP * QUERY_BLOCK
    steps = [(qb, c) for qb in range(nqb)
             for c in range((qb * QUERY_BLOCK + QUERY_BLOCK - 1) // ATT_KEY_CHUNK + 1)]
    qb_tab = jnp.asarray(np.array([s[0] for s in steps], np.int32))
    c_tab = jnp.asarray(np.array([s[1] for s in steps], np.int32))
    qmap = lambda b, s, qt, ct: (b * nqb + qt[s], 0)
    kvmap = lambda b, s, qt, ct: (b * nkc + ct[s], 0)
    grid_spec = pltpu.PrefetchScalarGridSpec(
        num_scalar_prefetch=2,
        grid=(batch, len(steps)),
        in_specs=[pl.BlockSpec((QUERY_BLOCK, DSA_HEADS * HEAD_DIM), qmap),
                  pl.BlockSpec((ATT_KEY_CHUNK, DSA_KV_HEADS * HEAD_DIM), kvmap),
                  pl.BlockSpec((ATT_KEY_CHUNK, DSA_KV_HEADS * HEAD_DIM), kvmap),
                  pl.BlockSpec((1, 1, 1, ATT_KEY_CHUNK, QUERY_BLOCK),
                               lambda b, s, qt, ct: (b, qt[s], ct[s], 0, 0))],
        out_specs=pl.BlockSpec((QUERY_BLOCK, DSA_HEADS * HEAD_DIM), qmap),
        scratch_shapes=[pltpu.VMEM((DSA_KV_HEADS, rows, 2 * HEAD_DIM), BF16),
                        pltpu.VMEM((DSA_KV_HEADS, rows, HEAD_DIM), F32),
                        pltpu.VMEM((DSA_KV_HEADS, rows, HEAD_DIM), F32),
                        pltpu.VMEM((DSA_KV_HEADS, rows, HEAD_DIM), F32)])
    return pl.pallas_call(
        _dsa_attn_kernel,
        grid_spec=grid_spec,
        out_shape=jax.ShapeDtypeStruct((batch * seq, DSA_HEADS * HEAD_DIM), BF16),
        compiler_params=_params(2),
        name="dsa_attention",
    )(qb_tab, c_tab, q, k, v, bias_t.reshape(batch, nqb, nkc, ATT_KEY_CHUNK, QUERY_BLOCK))


def kernel(x, positions, l0_attn_w_in, l0_attn_w_out, l1_attn_w_in, l1_k_idx_gain, l1_k_idx_bias,
           l1_attn_w_out, l0_ln_mix_gain, l0_ln_mix_bias, l0_ffn_gate, l0_ffn_up, l0_ffn_down,
           l0_ln_ffn_gain, l0_ln_ffn_bias, l1_ln_mix_gain, l1_ln_mix_bias, l1_ffn_gate, l1_ffn_up,
           l1_ffn_down, l1_ln_ffn_gain, l1_ln_ffn_bias):
    batch, seq, d_model = x.shape
    m = batch * seq
    cosf, sinf = _rope_tables(positions)
    rope = (cosf, sinf)
    xf = x.reshape(m, d_model)
    x_by_dilation = _cast_residue_major(xf, [d for _, d in DIL_GROUPS])

    tn = 512
    n_grp = len(DIL_GROUPS)
    grp_blocks = DIL_COLS // tn
    outs, lses = [], []
    for g, (_, d) in enumerate(DIL_GROUPS):
        col_block = lambda n, p0=0, g=g: ((p0 + n // grp_blocks) * n_grp + g) * grp_blocks + n % grp_blocks
        rope_g = tuple(_to_residue_major(t, d, batch, seq) for t in rope)
        x_g = x_by_dilation[g]
        qk = _matmul(x_g, l0_attn_w_in, col_block, 2 * DIL_COLS, BF16, tn=tn, rope=rope_g,
                     scale=ATTN_SCALE * LOG2E, scale_cols=DIL_COLS, name=f"l0_qk_proj_d{d}")
        v = _matmul(x_g, l0_attn_w_in, functools.partial(col_block, p0=2), DIL_COLS, BF16, tn=tn,
                    name=f"l0_v_proj_d{d}")
        o, lse = _dilated_attention(qk, v, d, batch, seq)
        outs.append(_from_residue_major(o, d, batch, seq))
        lses.append(_from_residue_major(lse, d, batch, seq))
    o0 = _dil_merge(outs, lses)
    same = lambda n: n
    y = _matmul(o0, l0_attn_w_out, same, d_model, BF16, tn=tn, name="l0_out_proj")
    xf, xb = _res_ln(xf, y, l0_ln_mix_gain, l0_ln_mix_bias)
    h = _ffn_up(xb, l0_ffn_gate, l0_ffn_up)
    y = _matmul(h, l0_ffn_down.astype(BF16), same, d_model, BF16, tn=tn, tm=512, name="l0_ffn_down")
    xf, xb = _res_ln(xf, y, l0_ln_ffn_gain, l0_ln_ffn_bias)

    b_q = DSA_HEADS * HEAD_DIM
    b_kv = DSA_KV_HEADS * HEAD_DIM
    b_qi = IDX_HEADS * HEAD_DIM
    o_v = b_q + b_kv
    o_qi = o_v + b_kv
    o_ki = o_qi + b_qi
    o_wi = o_ki + HEAD_DIM
    at = lambda col0: (lambda n: n + col0 // tn)
    q1 = _matmul(xb, l1_attn_w_in, at(0), b_q, BF16, tn=tn, rope=rope, scale=ATTN_SCALE * LOG2E,
                 scale_cols=b_q, name="l1_q_proj")
    k1 = _matmul(xb, l1_attn_w_in, at(b_q), b_kv, BF16, tn=tn, rope=rope, name="l1_k_proj")
    v1 = _matmul(xb, l1_attn_w_in, at(o_v), b_kv, BF16, tn=tn, name="l1_v_proj")
    qi = _matmul(xb, l1_attn_w_in, at(o_qi), b_qi, BF16, tn=tn, rope=rope, name="l1_qidx_proj")
    w_ki = l1_attn_w_in[:, o_ki:o_wi]
    w_wi_t = jnp.pad(l1_attn_w_in[:, o_wi:].T, ((0, HEAD_DIM - IDX_HEADS), (0, 0)))
    ki, wit = _kiwi(xb, w_ki, w_wi_t, cosf, sinf, l1_k_idx_gain, l1_k_idx_bias)
    bias_t = _dsa_select(qi, wit, ki, batch, seq)
    o1 = _dsa_attention(q1, k1, v1, bias_t, batch, seq)
    y = _matmul(o1, l1_attn_w_out, same, d_model, BF16, tn=tn, name="l1_out_proj")
    xf, xb = _res_ln(xf, y, l1_ln_mix_gain, l1_ln_mix_bias)
    h = _ffn_up(xb, l1_ffn_gate, l1_ffn_up)
    y = _matmul(h, l1_ffn_down.astype(BF16), same, d_model, BF16, tn=tn, tm=512, name="l1_ffn_down")
    (xf,) = _res_ln(xf, y, l1_ln_ffn_gain, l1_ln_ffn_bias, with_bf16=False)
    return xf.reshape(batch, seq, d_model)
```

```python
import functools
import math

import numpy as np
import jax
import jax.numpy as jnp
from jax import lax
from jax.experimental import pallas as pl
from jax.experimental.pallas import tpu as pltpu

F32 = jnp.float32
BF16 = jnp.bfloat16
I32 = jnp.int32

HEAD_DIM = 128
HALF = HEAD_DIM // 2
ROPE_THETA = 10000.0
LN_EPS = 1e-5
DEPTH = 2
DEEPNORM_ALPHA = (2 * DEPTH) ** 0.25
ATTN_SCALE = HEAD_DIM ** -0.5
LOG2E = math.log2(math.e)

DIL_GROUPS = ((128, 1), (512, 4), (2048, 16))
BAND = 128
DIL_HEADS = 16
DIL_COLS = DIL_HEADS * HEAD_DIM

DSA_HEADS = 32
DSA_KV_HEADS = 8
DSA_GROUP = DSA_HEADS // DSA_KV_HEADS
IDX_HEADS = 32
IDX_HEADS_PER_DOT = 4
TOPK = 256
QUERY_BLOCK = 128
KEY_CHUNK = 512
ATT_KEY_CHUNK = 1024
SCORES_AHEAD = 1

INT_MIN = np.int32(-2 ** 31)
MASK_NEG = -1e30
M_INIT = -1e29

VMEM_LIMIT = 56 * 1024 * 1024


def _params(n_axes):
    return pltpu.CompilerParams(dimension_semantics=("arbitrary",) * n_axes,
                                vmem_limit_bytes=VMEM_LIMIT)


def _dot_nt(a, b):
    return lax.dot_general(a, b, (((1,), (1,)), ((), ())), preferred_element_type=F32)


def _rope(a, c, s):
    return a * c + pltpu.roll(a, HALF, 1) * s


def _to_residue_major(t, d, batch, seq):
    if d == 1:
        return t
    c = t.shape[1]
    t = t.reshape(batch, seq // (BAND * d), BAND, d, c)
    return t.transpose(0, 1, 3, 2, 4).reshape(batch * seq, c)


def _from_residue_major(t, d, batch, seq):
    if d == 1:
        return t
    c = t.shape[1]
    t = t.reshape(batch, seq // (BAND * d), d, BAND, c)
    return t.transpose(0, 1, 3, 2, 4).reshape(batch * seq, c)


def _cast_residue_major_kernel(x_ref, *o_refs, dilations):
    rows = x_ref.shape[0]
    for o_ref, d in zip(o_refs, dilations):
        if d == 1:
            o_ref[...] = x_ref[...].astype(BF16)
            continue
        span = BAND * d
        for base in range(0, rows, span):
            for r in range(d):
                o_ref[base + r * BAND:base + (r + 1) * BAND, :] = (
                    x_ref[pl.ds(base + r, BAND, stride=d), :].astype(BF16))


def _cast_residue_major(x, dilations):
    m, dm = x.shape
    cols = HEAD_DIM
    rows = BAND * max(dilations)
    rows *= 2 if m % (2 * rows) == 0 else 1
    assert m % rows == 0 and dm % cols == 0 and all(rows % (BAND * d) == 0 for d in dilations)
    blk = pl.BlockSpec((rows, cols), lambda i, j: (i, j))
    return pl.pallas_call(
        functools.partial(_cast_residue_major_kernel, dilations=tuple(dilations)),
        grid=(m // rows, dm // cols),
        in_specs=[blk],
        out_specs=[blk] * len(dilations),
        out_shape=[jax.ShapeDtypeStruct((m, dm), BF16)] * len(dilations),
        compiler_params=_params(2),
        name="cast_residue_major",
    )(x)


def _rope_table_kernel(pos_ref, inv_ref, cos_ref, sin_ref):
    ang = pos_ref[...].astype(F32) * inv_ref[...]
    lane = lax.broadcasted_iota(I32, ang.shape, 1)
    cos_ref[...] = jnp.cos(ang)
    sin_ref[...] = jnp.where(lane < HALF, -jnp.sin(ang), jnp.sin(ang))


def _rope_tables(positions):
    m = positions.size
    tm = min(m, 1024)
    inv = ROPE_THETA ** (-jnp.arange(0, HEAD_DIM, 2, dtype=F32) / HEAD_DIM)
    inv2 = jnp.concatenate([inv, inv]).reshape(1, HEAD_DIM)
    return pl.pallas_call(
        _rope_table_kernel,
        grid=(m // tm,),
        in_specs=[pl.BlockSpec((tm, 1), lambda i: (i, 0)),
                  pl.BlockSpec((1, HEAD_DIM), lambda i: (0, 0))],
        out_specs=[pl.BlockSpec((tm, HEAD_DIM), lambda i: (i, 0))] * 2,
        out_shape=[jax.ShapeDtypeStruct((m, HEAD_DIM), F32)] * 2,
        compiler_params=_params(1),
        name="rope_tables",
    )(positions.reshape(m, 1), inv2)


MXU_WIDTH = 256


def _mm_kernel(x_ref, w_ref, *rest, w_is_f32, w_transposed, has_rope, n_blocks, scale_blocks, scale):
    rest = list(rest)
    wbf_ref = rest.pop() if w_is_f32 else w_ref
    o_ref = rest.pop()
    if w_is_f32:
        @pl.when(pl.program_id(1) == 0)
        def _():
            w = w_ref[...]
            wbf_ref[...] = (w.T if w_transposed else w).astype(BF16)

    x = x_ref[...]
    strips = [slice(j, j + MXU_WIDTH) for j in range(0, o_ref.shape[1], MXU_WIDTH)]
    accs = [jnp.dot(x, wbf_ref[:, st], preferred_element_type=F32) for st in strips]
    if not has_rope:
        for st, acc in zip(strips, accs):
            o_ref[:, st] = acc.astype(o_ref.dtype)
        return
    cos_ref, sin_ref = rest
    c, s = cos_ref[...], sin_ref[...]
    if scale_blocks:
        factor = scale if scale_blocks == n_blocks else jnp.where(pl.program_id(0) < scale_blocks, scale, 1.0)
        c, s = c * factor, s * factor
    for st, acc in zip(strips, accs):
        for j in range(0, MXU_WIDTH, HEAD_DIM):
            o_ref[:, st.start + j:st.start + j + HEAD_DIM] = (
                _rope(acc[:, j:j + HEAD_DIM], c, s).astype(o_ref.dtype))


def _matmul(x, w, col_block, n_cols, out_dtype, *, tn, tm=1024, rope=None, scale=1.0, scale_cols=0,
            w_transposed=False, name):
    m, k = x.shape
    tm = min(m, tm)
    assert m % tm == 0 and n_cols % tn == 0 and tn % MXU_WIDTH == 0 and scale_cols % tn == 0
    n_blocks = n_cols // tn
    w_is_f32 = w.dtype == F32
    assert w_is_f32 or not w_transposed
    w_mode = {} if w_is_f32 else {"pipeline_mode": pl.Buffered(1)}
    in_specs = [pl.BlockSpec((tm, k), lambda n, i: (i, 0)),
                pl.BlockSpec((tn, k), lambda n, i: (col_block(n), 0)) if w_transposed else
                pl.BlockSpec((k, tn), lambda n, i: (0, col_block(n)), **w_mode)]
    args = [x, w]
    if rope is not None:
        in_specs += [pl.BlockSpec((tm, HEAD_DIM), lambda n, i: (i, 0))] * 2
        args += list(rope)
    return pl.pallas_call(
        functools.partial(_mm_kernel, w_is_f32=w_is_f32, w_transposed=w_transposed, has_rope=rope is not None,
                          n_blocks=n_blocks, scale_blocks=scale_cols // tn, scale=scale),
        grid=(n_blocks, m // tm),
        in_specs=in_specs,
        out_specs=pl.BlockSpec((tm, tn), lambda n, i: (i, n)),
        out_shape=jax.ShapeDtypeStruct((m, n_cols), out_dtype),
        scratch_shapes=[pltpu.VMEM((k, tn), BF16)] if w_is_f32 else [],
        compiler_params=_params(2),
        name=name,
    )(*args)


def _kiwi_kernel(x_ref, wk_ref, ww_ref, cos_ref, sin_ref, g_ref, b_ref, ki_ref, wit_ref, wkbf_ref, wwbf_ref):
    @pl.when(pl.program_id(0) == 0)
    def _():
        wkbf_ref[...] = wk_ref[...].astype(BF16)
        wwbf_ref[...] = ww_ref[...].astype(BF16)

    x = x_ref[...]
    a = jnp.dot(x, wkbf_ref[...], preferred_element_type=F32)
    mu = jnp.mean(a, axis=-1, keepdims=True)
    var = jnp.mean(jnp.square(a - mu), axis=-1, keepdims=True)
    y = (a - mu) * lax.rsqrt(var + LN_EPS) * g_ref[...] + b_ref[...]
    ki_ref[...] = _rope(y, cos_ref[...], sin_ref[...]).astype(ki_ref.dtype)
    wi = jnp.dot(x, wwbf_ref[...], preferred_element_type=F32) * (IDX_HEADS ** -0.5 * HEAD_DIM ** -0.5)
    wit_ref[...] = wi.T


def _kiwi(x, w_ki, w_wi, cosf, sinf, gain, bias):
    m, k = x.shape
    tm = min(m, 1024)
    row = lambda i: (i, 0)
    fixed = lambda i: (0, 0)
    return pl.pallas_call(
        _kiwi_kernel,
        grid=(m // tm,),
        in_specs=[pl.BlockSpec((tm, k), row), pl.BlockSpec((k, HEAD_DIM), fixed),
                  pl.BlockSpec((k, HEAD_DIM), fixed),
                  pl.BlockSpec((tm, HEAD_DIM), row), pl.BlockSpec((tm, HEAD_DIM), row),
                  pl.BlockSpec((1, HEAD_DIM), fixed), pl.BlockSpec((1, HEAD_DIM), fixed)],
        out_specs=[pl.BlockSpec((tm, HEAD_DIM), row), pl.BlockSpec((HEAD_DIM, tm), lambda i: (0, i))],
        out_shape=[jax.ShapeDtypeStruct((m, HEAD_DIM), BF16),
                   jax.ShapeDtypeStruct((HEAD_DIM, m), F32)],
        scratch_shapes=[pltpu.VMEM((k, HEAD_DIM), BF16)] * 2,
        compiler_params=_params(1),
        name="dsa_kiwi_proj",
    )(x, w_ki, w_wi, cosf, sinf, gain.reshape(1, HEAD_DIM), bias.reshape(1, HEAD_DIM))


def _ffn_up_kernel(x_ref, wg_ref, wu_ref, h_ref, wgbf_ref, wubf_ref):
    @pl.when(pl.program_id(1) == 0)
    def _():
        wgbf_ref[...] = wg_ref[...].astype(BF16)
        wubf_ref[...] = wu_ref[...].astype(BF16)

    x = x_ref[...]
    g = jnp.dot(x, wgbf_ref[...], preferred_element_type=F32)
    u = jnp.dot(x, wubf_ref[...], preferred_element_type=F32)
    h_ref[...] = (g * (1.0 / (1.0 + jnp.exp(-g))) * u).astype(h_ref.dtype)


def _ffn_up(x, w_gate, w_up, *, tn=256):
    m, k = x.shape
    n = w_gate.shape[1]
    tm = min(m, 1024)
    assert n % tn == 0
    wspec = pl.BlockSpec((k, tn), lambda j, i: (0, j))
    return pl.pallas_call(
        _ffn_up_kernel,
        grid=(n // tn, m // tm),
        in_specs=[pl.BlockSpec((tm, k), lambda j, i: (i, 0)), wspec, wspec],
        out_specs=pl.BlockSpec((tm, tn), lambda j, i: (i, j)),
        out_shape=jax.ShapeDtypeStruct((m, n), BF16),
        scratch_shapes=[pltpu.VMEM((k, tn), BF16)] * 2,
        compiler_params=_params(2),
        name="ffn_up",
    )(x, w_gate, w_up)


def _res_ln_kernel(x_ref, y_ref, g_ref, b_ref, o_ref, *maybe_obf_ref):
    z = DEEPNORM_ALPHA * x_ref[...] + y_ref[...].astype(F32)
    mu = jnp.mean(z, axis=-1, keepdims=True)
    d = z - mu
    var = jnp.mean(jnp.square(d), axis=-1, keepdims=True)
    out = d * lax.rsqrt(var + LN_EPS) * g_ref[...] + b_ref[...]
    o_ref[...] = out
    for obf_ref in maybe_obf_ref:
        obf_ref[...] = out.astype(BF16)


def _res_ln(x, y, gain, bias, *, with_bf16=True):
    m, d = x.shape
    tm = min(m, 256)
    row = lambda i: (i, 0)
    fixed = lambda i: (0, 0)
    n_out = 2 if with_bf16 else 1
    return pl.pallas_call(
        _res_ln_kernel,
        grid=(m // tm,),
        in_specs=[pl.BlockSpec((tm, d), row), pl.BlockSpec((tm, d), row),
                  pl.BlockSpec((1, d), fixed), pl.BlockSpec((1, d), fixed)],
        out_specs=[pl.BlockSpec((tm, d), row)] * n_out,
        out_shape=[jax.ShapeDtypeStruct((m, d), F32), jax.ShapeDtypeStruct((m, d), BF16)][:n_out],
        compiler_params=_params(1),
        name="residual_layernorm",
    )(x, y, gain.reshape(1, d), bias.reshape(1, d))


def _dil_attn_kernel(q_ref, kp_ref, kc_ref, vp_ref, vc_ref, o_ref, lse_ref):
    has_prev = pl.program_id(1) > 0
    qi = lax.broadcasted_iota(I32, (BAND, BAND), 0)
    kj = lax.broadcasted_iota(I32, (BAND, BAND), 1)
    mask_p = jnp.logical_and(kj >= qi, has_prev)
    mask_c = kj <= qi
    lse_ref[...] = jnp.zeros_like(lse_ref)
    heads = [slice(h * HEAD_DIM, (h + 1) * HEAD_DIM) for h in range(DIL_HEADS)]
    scores = [(_dot_nt(q_ref[:, sl], kp_ref[:, sl]), _dot_nt(q_ref[:, sl], kc_ref[:, sl])) for sl in heads]
    probs = []
    for h, (sp, sc) in enumerate(scores):
        sp = jnp.where(mask_p, sp, -jnp.inf)
        sc = jnp.where(mask_c, sc, -jnp.inf)
        m = jnp.max(jnp.maximum(sp, sc), axis=1, keepdims=True)
        pp = jnp.exp2(sp - m)
        pc = jnp.exp2(sc - m)
        l = jnp.sum(pp + pc, axis=1, keepdims=True)
        lse_ref[:, h:h + 1] = m + jnp.log2(l)
        probs.append((pp.astype(BF16), pc.astype(BF16), 1.0 / l))
    for sl, (pp, pc, inv_l) in zip(heads, probs):
        o = (jnp.dot(pp, vp_ref[:, sl], preferred_element_type=F32)
             + jnp.dot(pc, vc_ref[:, sl], preferred_element_type=F32))
        o_ref[:, sl] = (o * inv_l).astype(o_ref.dtype)


def _dilated_attention(qk, v, dilation, batch, seq):
    d = dilation
    n_chunks = seq // (BAND * d)
    cur = lambda col: (lambda b, ch, r: ((b * n_chunks + ch) * d + r, col))
    prev = lambda col: (lambda b, ch, r: ((b * n_chunks + jnp.maximum(ch - 1, 0)) * d + r, col))
    blk = (BAND, DIL_COLS)
    return pl.pallas_call(
        _dil_attn_kernel,
        grid=(batch, n_chunks, d),
        in_specs=[pl.BlockSpec(blk, cur(0)), pl.BlockSpec(blk, prev(1)), pl.BlockSpec(blk, cur(1)),
                  pl.BlockSpec(blk, prev(0)), pl.BlockSpec(blk, cur(0))],
        out_specs=[pl.BlockSpec(blk, cur(0)), pl.BlockSpec((BAND, HEAD_DIM), cur(0))],
        out_shape=[jax.ShapeDtypeStruct((batch * seq, DIL_COLS), BF16),
                   jax.ShapeDtypeStruct((batch * seq, HEAD_DIM), F32)],
        compiler_params=_params(3),
        name=f"dilated_attention_d{d}",
    )(qk, qk, qk, v, v)


def _dil_merge_kernel(o0_ref, o1_ref, o2_ref, l0_ref, l1_ref, l2_ref, out_ref):
    l0, l1, l2 = l0_ref[...], l1_ref[...], l2_ref[...]
    mx = jnp.maximum(jnp.maximum(l0, l1), l2)
    e0, e1, e2 = jnp.exp2(l0 - mx), jnp.exp2(l1 - mx), jnp.exp2(l2 - mx)
    inv = 1.0 / (e0 + e1 + e2)
    w0, w1, w2 = e0 * inv, e1 * inv, e2 * inv
    for h in range(DIL_HEADS):
        sl = slice(h * HEAD_DIM, (h + 1) * HEAD_DIM)
        col = slice(h, h + 1)
        acc = (w0[:, col] * o0_ref[:, sl].astype(F32) + w1[:, col] * o1_ref[:, sl].astype(F32)
               + w2[:, col] * o2_ref[:, sl].astype(F32))
        out_ref[:, sl] = acc.astype(out_ref.dtype)


def _dil_merge(outs, lses):
    m = outs[0].shape[0]
    tm = min(m, 512)
    row = lambda i: (i, 0)
    return pl.pallas_call(
        _dil_merge_kernel,
        grid=(m // tm,),
        in_specs=[pl.BlockSpec((tm, DIL_COLS), row)] * 3 + [pl.BlockSpec((tm, HEAD_DIM), row)] * 3,
        out_specs=pl.BlockSpec((tm, DIL_COLS), row),
        out_shape=jax.ShapeDtypeStruct((m, DIL_COLS), BF16),
        compiler_params=_params(1),
        name="dilated_merge",
    )(*outs, *lses)


def _sortable(x):
    b = lax.bitcast_convert_type(x, I32)
    return b ^ ((b >> 31) & np.int32(0x7FFFFFFF))


def _dsa_select_kernel(qi_ref, wit_ref, ki_ref, bias_ref, qs_ref, keys_ref, *, n_chunks):
    qb = pl.program_id(1)
    n_c = (qb * QUERY_BLOCK + QUERY_BLOCK - 1) // KEY_CHUNK + 1
    for h in range(IDX_HEADS):
        qs_ref[h * QUERY_BLOCK:(h + 1) * QUERY_BLOCK, :] = qi_ref[:, h * HEAD_DIM:(h + 1) * HEAD_DIM]
    w = wit_ref[...]
    kpos = lax.broadcasted_iota(I32, (KEY_CHUNK, QUERY_BLOCK), 0)
    t = qb * QUERY_BLOCK + lax.broadcasted_iota(I32, (KEY_CHUNK, QUERY_BLOCK), 1)
    rows_per_dot = IDX_HEADS_PER_DOT * QUERY_BLOCK

    def score_chunk(c, carry):
        kic = ki_ref[0, pl.ds(pl.multiple_of(c * KEY_CHUNK, KEY_CHUNK), KEY_CHUNK), :]
        sc = jnp.zeros((KEY_CHUNK, QUERY_BLOCK), F32)
        for hg in range(IDX_HEADS // IDX_HEADS_PER_DOT):
            r = _dot_nt(kic, qs_ref[hg * rows_per_dot:(hg + 1) * rows_per_dot, :])
            for j in range(IDX_HEADS_PER_DOT):
                h = hg * IDX_HEADS_PER_DOT + j
                sc = sc + jnp.maximum(r[:, j * QUERY_BLOCK:(j + 1) * QUERY_BLOCK], 0.0) * w[h:h + 1, :]
        causal = c * KEY_CHUNK + kpos <= t
        keys_ref[c] = jnp.where(causal, _sortable(sc), INT_MIN)
        return carry

    lax.fori_loop(0, n_c, score_chunk, 0)

    def count_ge(cand):
        def body(c, acc):
            x = jnp.where(keys_ref[c] >= cand, 1, 0).astype(I32)
            return acc + jnp.sum(x.reshape(KEY_CHUNK // 8, 8, QUERY_BLOCK), axis=0)
        acc = lax.fori_loop(0, n_c, body, jnp.zeros((8, QUERY_BLOCK), I32))
        return jnp.sum(acc, axis=0, keepdims=True)

    zero = jnp.zeros((1, QUERY_BLOCK), I32)
    thr = jnp.where(count_ge(zero) >= TOPK, zero, jnp.full((1, QUERY_BLOCK), INT_MIN, I32))

    def bit_step(i, thr):
        cand = thr | jnp.left_shift(jnp.int32(1), 30 - i)
        return jnp.where(count_ge(cand) >= TOPK, cand, thr)

    thr = lax.fori_loop(0, 31, bit_step, thr)
    thr = jnp.maximum(thr, INT_MIN + 1)

    def write_bias(c, carry):
        bias_ref[0, 0, c] = jnp.where(keys_ref[c] >= thr, 0.0, MASK_NEG).astype(bias_ref.dtype)
        return carry

    def write_masked(c, carry):
        bias_ref[0, 0, c] = jnp.full((KEY_CHUNK, QUERY_BLOCK), MASK_NEG, bias_ref.dtype)
        return carry

    lax.fori_loop(0, n_c, write_bias, 0)
    lax.fori_loop(n_c, n_chunks, write_masked, 0)


def _dsa_select(qi, wit, ki, batch, seq):
    nqb = seq // QUERY_BLOCK
    nkc = seq // KEY_CHUNK
    return pl.pallas_call(
        functools.partial(_dsa_select_kernel, n_chunks=nkc),
        grid=(batch, nqb),
        in_specs=[pl.BlockSpec((QUERY_BLOCK, IDX_HEADS * HEAD_DIM), lambda b, q: (b * nqb + q, 0)),
                  pl.BlockSpec((HEAD_DIM, QUERY_BLOCK), lambda b, q: (0, b * nqb + q)),
                  pl.BlockSpec((1, seq, HEAD_DIM), lambda b, q: (b, 0, 0))],
        out_specs=pl.BlockSpec((1, 1, nkc, KEY_CHUNK, QUERY_BLOCK), lambda b, q: (b, q, 0, 0, 0)),
        out_shape=jax.ShapeDtypeStruct((batch, nqb, nkc, KEY_CHUNK, QUERY_BLOCK), BF16),
        scratch_shapes=[pltpu.VMEM((IDX_HEADS * QUERY_BLOCK, HEAD_DIM), BF16),
                        pltpu.VMEM((nkc, KEY_CHUNK, QUERY_BLOCK), I32)],
        compiler_params=_params(2),
        name="dsa_select",
    )(qi, wit, ki.reshape(batch, seq, HEAD_DIM))


def _dsa_attn_kernel(qb_tab, c_tab, q_ref, k_ref, v_ref, bias_ref, o_ref, qa_ref, m_ref, l_ref, acc_ref):
    step = pl.program_id(1)
    qb = qb_tab[step]
    c = c_tab[step]
    last = (qb * QUERY_BLOCK + QUERY_BLOCK - 1) // ATT_KEY_CHUNK
    n_tiles = ATT_KEY_CHUNK // HEAD_DIM

    @pl.when(c == 0)
    def _():
        ri = lax.broadcasted_iota(I32, (QUERY_BLOCK, HEAD_DIM), 0)
        ci = lax.broadcasted_iota(I32, (QUERY_BLOCK, HEAD_DIM), 1)
        eye = jnp.where(ri == ci, 1.0, 0.0).astype(BF16)
        for g in range(DSA_KV_HEADS):
            for j in range(DSA_GROUP):
                h = g * DSA_GROUP + j
                rows = slice(j * QUERY_BLOCK, (j + 1) * QUERY_BLOCK)
                qa_ref[g, rows, :HEAD_DIM] = q_ref[:, h * HEAD_DIM:(h + 1) * HEAD_DIM]
                qa_ref[g, rows, HEAD_DIM:] = eye
        m_ref[...] = jnp.full(m_ref.shape, M_INIT, F32)
        l_ref[...] = jnp.zeros_like(l_ref)
        acc_ref[...] = jnp.zeros_like(acc_ref)

    bias_t = bias_ref[0, 0, 0]

    def scores(g):
        ka = jnp.concatenate([k_ref[:, g * HEAD_DIM:(g + 1) * HEAD_DIM], bias_t], axis=1)
        return _dot_nt(qa_ref[g], ka)

    pending = [scores(g) for g in range(SCORES_AHEAD)]
    for g in range(DSA_KV_HEADS):
        sl = slice(g * HEAD_DIM, (g + 1) * HEAD_DIM)
        s = pending.pop(0)
        if g + SCORES_AHEAD < DSA_KV_HEADS:
            pending.append(scores(g + SCORES_AHEAD))
        tiles = [s[:, j * HEAD_DIM:(j + 1) * HEAD_DIM] for j in range(n_tiles)]
        m_old = m_ref[g]
        m_new = jnp.maximum(m_old, jnp.max(functools.reduce(jnp.maximum, tiles), axis=1, keepdims=True))
        alpha = jnp.exp2(m_old - m_new)
        ps = [jnp.exp2(tl - m_new) for tl in tiles]
        l_ref[g] = alpha * l_ref[g] + jnp.sum(functools.reduce(jnp.add, ps), axis=1, keepdims=True)
        p = jnp.concatenate([tl.astype(BF16) for tl in ps], axis=1)
        acc_ref[g] = alpha * acc_ref[g] + jnp.dot(p, v_ref[:, sl], preferred_element_type=F32)
        m_ref[g] = m_new

    @pl.when(c == last)
    def _():
        for g in range(DSA_KV_HEADS):
            o = acc_ref[g] / l_ref[g]
            for j in range(DSA_GROUP):
                h = g * DSA_GROUP + j
                o_ref[:, h * HEAD_DIM:(h + 1) * HEAD_DIM] = (
                    o[j * QUERY_BLOCK:(j + 1) * QUERY_BLOCK].astype(o_ref.dtype))


def _dsa_attention(q, k, v, bias_t, batch, seq):
    nqb = seq // QUERY_BLOCK
    nkc = seq // ATT_KEY_CHUNK
    rows = DSA_GROUP * QUERY_BLOCK
    steps = [(qb, c) for qb in range(nqb)
             for c in range((qb * QUERY_BLOCK + QUERY_BLOCK - 1) // ATT_KEY_CHUNK + 1)]
    qb_tab = jnp.asarray(np.array([s[0] for s in steps], np.int32))
    c_tab = jnp.asarray(np.array([s[1] for s in steps], np.int32))
    qmap = lambda b, s, qt, ct: (b * nqb + qt[s], 0)
    kvmap = lambda b, s, qt, ct: (b * nkc + ct[s], 0)
    grid_spec = pltpu.PrefetchScalarGridSpec(
        num_scalar_prefetch=2,
        grid=(batch, len(steps)),
        in_specs=[pl.BlockSpec((QUERY_BLOCK, DSA_HEADS * HEAD_DIM), qmap),
                  pl.BlockSpec((ATT_KEY_CHUNK, DSA_KV_HEADS * HEAD_DIM), kvmap),
                  pl.BlockSpec((ATT_KEY_CHUNK, DSA_KV_HEADS * HEAD_DIM), kvmap),
                  pl.BlockSpec((1, 1, 1, ATT_KEY_CHUNK, QUERY_BLOCK),
                               lambda b, s, qt, ct: (b, qt[s], ct[s], 0, 0))],
        out_specs=pl.BlockSpec((QUERY_BLOCK, DSA_HEADS * HEAD_DIM), qmap),
        scratch_shapes=[pltpu.VMEM((DSA_KV_HEADS, rows, 2 * HEAD_DIM), BF16),
                        pltpu.VMEM((DSA_KV_HEADS, rows, HEAD_DIM), F32),
                        pltpu.VMEM((DSA_KV_HEADS, rows, HEAD_DIM), F32),
                        pltpu.VMEM((DSA_KV_HEADS, rows, HEAD_DIM), F32)])
    return pl.pallas_call(
        _dsa_attn_kernel,
        grid_spec=grid_spec,
        out_shape=jax.ShapeDtypeStruct((batch * seq, DSA_HEADS * HEAD_DIM), BF16),
        compiler_params=_params(2),
        name="dsa_attention",
    )(qb_tab, c_tab, q, k, v, bias_t.reshape(batch, nqb, nkc, ATT_KEY_CHUNK, QUERY_BLOCK))


def kernel(x, positions, l0_attn_w_in, l0_attn_w_out, l1_attn_w_in, l1_k_idx_gain, l1_k_idx_bias,
           l1_attn_w_out, l0_ln_mix_gain, l0_ln_mix_bias, l0_ffn_gate, l0_ffn_up, l0_ffn_down,
           l0_ln_ffn_gain, l0_ln_ffn_bias, l1_ln_mix_gain, l1_ln_mix_bias, l1_ffn_gate, l1_ffn_up,
           l1_ffn_down, l1_ln_ffn_gain, l1_ln_ffn_bias):
    batch, seq, d_model = x.shape
    m = batch * seq
    cosf, sinf = _rope_tables(positions)
    rope = (cosf, sinf)
    xf = x.reshape(m, d_model)
    x_by_dilation = _cast_residue_major(xf, [d for _, d in DIL_GROUPS])

    tn = 512
    n_grp = len(DIL_GROUPS)
    grp_blocks = DIL_COLS // tn
    outs, lses = [], []
    for g, (_, d) in enumerate(DIL_GROUPS):
        col_block = lambda n, p0=0, g=g: ((p0 + n // grp_blocks) * n_grp + g) * grp_blocks + n % grp_blocks
        rope_g = tuple(_to_residue_major(t, d, batch, seq) for t in rope)
        x_g = x_by_dilation[g]
        qk = _matmul(x_g, l0_attn_w_in, col_block, 2 * DIL_COLS, BF16, tn=tn, rope=rope_g,
                     scale=ATTN_SCALE * LOG2E, scale_cols=DIL_COLS, name=f"l0_qk_proj_d{d}")
        v = _matmul(x_g, l0_attn_w_in, functools.partial(col_block, p0=2), DIL_COLS, BF16, tn=tn,
                    name=f"l0_v_proj_d{d}")
        o, lse = _dilated_attention(qk, v, d, batch, seq)
        outs.append(_from_residue_major(o, d, batch, seq))
        lses.append(_from_residue_major(lse, d, batch, seq))
    o0 = _dil_merge(outs, lses)
    same = lambda n: n
    y = _matmul(o0, l0_attn_w_out, same, d_model, BF16, tn=tn, name="l0_out_proj")
    xf, xb = _res_ln(xf, y, l0_ln_mix_gain, l0_ln_mix_bias)
    h = _ffn_up(xb, l0_ffn_gate, l0_ffn_up)
    y = _matmul(h, l0_ffn_down.astype(BF16), same, d_model, BF16, tn=tn, tm=512, name="l0_ffn_down")
    xf, xb = _res_ln(xf, y, l0_ln_ffn_gain, l0_ln_ffn_bias)

    b_q = DSA_HEADS * HEAD_DIM
    b_kv = DSA_KV_HEADS * HEAD_DIM
    b_qi = IDX_HEADS * HEAD_DIM
    o_v = b_q + b_kv
    o_qi = o_v + b_kv
    o_ki = o_qi + b_qi
    o_wi = o_ki + HEAD_DIM
    at = lambda col0: (lambda n: n + col0 // tn)
    w1_t = l1_attn_w_in.T
    proj1 = functools.partial(_matmul, xb, w1_t, tn=tn, w_transposed=True)
    q1 = proj1(at(0), b_q, BF16, rope=rope, scale=ATTN_SCALE * LOG2E, scale_cols=b_q, name="l1_q_proj")
    k1 = proj1(at(b_q), b_kv, BF16, rope=rope, name="l1_k_proj")
    v1 = proj1(at(o_v), b_kv, BF16, name="l1_v_proj")
    qi = proj1(at(o_qi), b_qi, BF16, rope=rope, name="l1_qidx_proj")
    w_ki = l1_attn_w_in[:, o_ki:o_wi]
    w_wi = jnp.pad(l1_attn_w_in[:, o_wi:], ((0, 0), (0, HEAD_DIM - IDX_HEADS)))
    ki, wit = _kiwi(xb, w_ki, w_wi, cosf, sinf, l1_k_idx_gain, l1_k_idx_bias)
    bias_t = _dsa_select(qi, wit, ki, batch, seq)
    o1 = _dsa_attention(q1, k1, v1, bias_t, batch, seq)
    y = _matmul(o1, l1_attn_w_out, same, d_model, BF16, tn=tn, name="l1_out_proj")
    xf, xb = _res_ln(xf, y, l1_ln_mix_gain, l1_ln_mix_bias)
    h = _ffn_up(xb, l1_ffn_gate, l1_ffn_up)
    y = _matmul(h, l1_ffn_down.astype(BF16), same, d_model, BF16, tn=tn, tm=512, name="l1_ffn_down")
    (xf,) = _res_ln(xf, y, l1_ln_ffn_gain, l1_ln_ffn_bias, with_bf16=False)
    return xf.reshape(batch, seq, d_model)
```

```python
import functools
import math

import numpy as np
import jax
import jax.numpy as jnp
from jax import lax
from jax.experimental import pallas as pl
from jax.experimental.pallas import tpu as pltpu

F32 = jnp.float32
BF16 = jnp.bfloat16
I32 = jnp.int32

HEAD_DIM = 128
HALF = HEAD_DIM // 2
ROPE_THETA = 10000.0
LN_EPS = 1e-5
DEPTH = 2
DEEPNORM_ALPHA = (2 * DEPTH) ** 0.25
ATTN_SCALE = HEAD_DIM ** -0.5
LOG2E = math.log2(math.e)

DIL_GROUPS = ((128, 1), (512, 4), (2048, 16))
BAND = 128
DIL_HEADS = 16
DIL_COLS = DIL_HEADS * HEAD_DIM

DSA_HEADS = 32
DSA_KV_HEADS = 8
DSA_GROUP = DSA_HEADS // DSA_KV_HEADS
IDX_HEADS = 32
IDX_HEADS_PER_DOT = 4
TOPK = 256
QUERY_BLOCK = 128
KEY_CHUNK = 512
ATT_KEY_CHUNK = 1024
SCORES_AHEAD = 1

INT_MIN = np.int32(-2 ** 31)
MASK_NEG = -1e30
M_INIT = -1e29

VMEM_LIMIT = 56 * 1024 * 1024


def _params(n_axes):
    return pltpu.CompilerParams(dimension_semantics=("arbitrary",) * n_axes,
                                vmem_limit_bytes=VMEM_LIMIT)


def _dot_nt(a, b):
    return lax.dot_general(a, b, (((1,), (1,)), ((), ())), preferred_element_type=F32)


def _rope(a, c, s):
    return a * c + pltpu.roll(a, HALF, 1) * s


def _to_residue_major(t, d, batch, seq):
    if d == 1:
        return t
    c = t.shape[1]
    t = t.reshape(batch, seq // (BAND * d), BAND, d, c)
    return t.transpose(0, 1, 3, 2, 4).reshape(batch * seq, c)


def _from_residue_major(t, d, batch, seq):
    if d == 1:
        return t
    c = t.shape[1]
    t = t.reshape(batch, seq // (BAND * d), d, BAND, c)
    return t.transpose(0, 1, 3, 2, 4).reshape(batch * seq, c)


def _cast_residue_major_kernel(x_ref, *o_refs, dilations):
    rows = x_ref.shape[0]
    for o_ref, d in zip(o_refs, dilations):
        if d == 1:
            o_ref[...] = x_ref[...].astype(BF16)
            continue
        span = BAND * d
        for base in range(0, rows, span):
            for r in range(d):
                o_ref[base + r * BAND:base + (r + 1) * BAND, :] = (
                    x_ref[pl.ds(base + r, BAND, stride=d), :].astype(BF16))


def _cast_residue_major(x, dilations):
    m, dm = x.shape
    cols = HEAD_DIM
    rows = BAND * max(dilations)
    rows *= 2 if m % (2 * rows) == 0 else 1
    assert m % rows == 0 and dm % cols == 0 and all(rows % (BAND * d) == 0 for d in dilations)
    blk = pl.BlockSpec((rows, cols), lambda i, j: (i, j))
    return pl.pallas_call(
        functools.partial(_cast_residue_major_kernel, dilations=tuple(dilations)),
        grid=(m // rows, dm // cols),
        in_specs=[blk],
        out_specs=[blk] * len(dilations),
        out_shape=[jax.ShapeDtypeStruct((m, dm), BF16)] * len(dilations),
        compiler_params=_params(2),
        name="cast_residue_major",
    )(x)


def _rope_table_kernel(pos_ref, inv_ref, cos_ref, sin_ref):
    ang = pos_ref[...].astype(F32) * inv_ref[...]
    lane = lax.broadcasted_iota(I32, ang.shape, 1)
    cos_ref[...] = jnp.cos(ang)
    sin_ref[...] = jnp.where(lane < HALF, -jnp.sin(ang), jnp.sin(ang))


def _rope_tables(positions):
    m = positions.size
    tm = min(m, 1024)
    inv = ROPE_THETA ** (-jnp.arange(0, HEAD_DIM, 2, dtype=F32) / HEAD_DIM)
    inv2 = jnp.concatenate([inv, inv]).reshape(1, HEAD_DIM)
    return pl.pallas_call(
        _rope_table_kernel,
        grid=(m // tm,),
        in_specs=[pl.BlockSpec((tm, 1), lambda i: (i, 0)),
                  pl.BlockSpec((1, HEAD_DIM), lambda i: (0, 0))],
        out_specs=[pl.BlockSpec((tm, HEAD_DIM), lambda i: (i, 0))] * 2,
        out_shape=[jax.ShapeDtypeStruct((m, HEAD_DIM), F32)] * 2,
        compiler_params=_params(1),
        name="rope_tables",
    )(positions.reshape(m, 1), inv2)


MXU_WIDTH = 256


def _mm_kernel(x_ref, w_ref, *rest, w_is_f32, w_transposed, has_rope, n_blocks, scale_blocks, scale):
    rest = list(rest)
    wbf_ref = rest.pop() if w_is_f32 else w_ref
    o_ref = rest.pop()
    if w_is_f32:
        @pl.when(pl.program_id(1) == 0)
        def _():
            w = w_ref[...]
            wbf_ref[...] = (w.T if w_transposed else w).astype(BF16)

    x = x_ref[...]
    strips = [slice(j, j + MXU_WIDTH) for j in range(0, o_ref.shape[1], MXU_WIDTH)]
    accs = [jnp.dot(x, wbf_ref[:, st], preferred_element_type=F32) for st in strips]
    if not has_rope:
        for st, acc in zip(strips, accs):
            o_ref[:, st] = acc.astype(o_ref.dtype)
        return
    cos_ref, sin_ref = rest
    c, s = cos_ref[...], sin_ref[...]
    if scale_blocks:
        factor = scale if scale_blocks == n_blocks else jnp.where(pl.program_id(0) < scale_blocks, scale, 1.0)
        c, s = c * factor, s * factor
    for st, acc in zip(strips, accs):
        for j in range(0, MXU_WIDTH, HEAD_DIM):
            o_ref[:, st.start + j:st.start + j + HEAD_DIM] = (
                _rope(acc[:, j:j + HEAD_DIM], c, s).astype(o_ref.dtype))


def _matmul(x, w, col_block, n_cols, out_dtype, *, tn, tm=1024, rope=None, scale=1.0, scale_cols=0,
            w_transposed=False, name):
    m, k = x.shape
    tm = min(m, tm)
    assert m % tm == 0 and n_cols % tn == 0 and tn % MXU_WIDTH == 0 and scale_cols % tn == 0
    n_blocks = n_cols // tn
    w_is_f32 = w.dtype == F32
    assert w_is_f32 or not w_transposed
    w_mode = {} if w_is_f32 else {"pipeline_mode": pl.Buffered(1)}
    in_specs = [pl.BlockSpec((tm, k), lambda n, i: (i, 0)),
                pl.BlockSpec((tn, k), lambda n, i: (col_block(n), 0)) if w_transposed else
                pl.BlockSpec((k, tn), lambda n, i: (0, col_block(n)), **w_mode)]
    args = [x, w]
    if rope is not None:
        in_specs += [pl.BlockSpec((tm, HEAD_DIM), lambda n, i: (i, 0))] * 2
        args += list(rope)
    return pl.pallas_call(
        functools.partial(_mm_kernel, w_is_f32=w_is_f32, w_transposed=w_transposed, has_rope=rope is not None,
                          n_blocks=n_blocks, scale_blocks=scale_cols // tn, scale=scale),
        grid=(n_blocks, m // tm),
        in_specs=in_specs,
        out_specs=pl.BlockSpec((tm, tn), lambda n, i: (i, n)),
        out_shape=jax.ShapeDtypeStruct((m, n_cols), out_dtype),
        scratch_shapes=[pltpu.VMEM((k, tn), BF16)] if w_is_f32 else [],
        compiler_params=_params(2),
        name=name,
    )(*args)


def _kiwi_kernel(x_ref, wk_ref, ww_ref, cos_ref, sin_ref, g_ref, b_ref, ki_ref, wit_ref, wkbf_ref, wwbf_ref):
    @pl.when(pl.program_id(0) == 0)
    def _():
        wkbf_ref[...] = wk_ref[...].astype(BF16)
        wwbf_ref[...] = ww_ref[...].astype(BF16)

    x = x_ref[...]
    a = jnp.dot(x, wkbf_ref[...], preferred_element_type=F32)
    mu = jnp.mean(a, axis=-1, keepdims=True)
    var = jnp.mean(jnp.square(a - mu), axis=-1, keepdims=True)
    y = (a - mu) * lax.rsqrt(var + LN_EPS) * g_ref[...] + b_ref[...]
    ki_ref[...] = _rope(y, cos_ref[...], sin_ref[...]).astype(ki_ref.dtype)
    wi = jnp.dot(x, wwbf_ref[...], preferred_element_type=F32) * (IDX_HEADS ** -0.5 * HEAD_DIM ** -0.5)
    wit_ref[...] = wi.T


def _kiwi(x, w_ki, w_wi, cosf, sinf, gain, bias):
    m, k = x.shape
    tm = min(m, 1024)
    row = lambda i: (i, 0)
    fixed = lambda i: (0, 0)
    return pl.pallas_call(
        _kiwi_kernel,
        grid=(m // tm,),
        in_specs=[pl.BlockSpec((tm, k), row), pl.BlockSpec((k, HEAD_DIM), fixed),
                  pl.BlockSpec((k, HEAD_DIM), fixed),
                  pl.BlockSpec((tm, HEAD_DIM), row), pl.BlockSpec((tm, HEAD_DIM), row),
                  pl.BlockSpec((1, HEAD_DIM), fixed), pl.BlockSpec((1, HEAD_DIM), fixed)],
        out_specs=[pl.BlockSpec((tm, HEAD_DIM), row), pl.BlockSpec((HEAD_DIM, tm), lambda i: (0, i))],
        out_shape=[jax.ShapeDtypeStruct((m, HEAD_DIM), BF16),
                   jax.ShapeDtypeStruct((HEAD_DIM, m), F32)],
        scratch_shapes=[pltpu.VMEM((k, HEAD_DIM), BF16)] * 2,
        compiler_params=_params(1),
        name="dsa_kiwi_proj",
    )(x, w_ki, w_wi, cosf, sinf, gain.reshape(1, HEAD_DIM), bias.reshape(1, HEAD_DIM))


def _ffn_up_kernel(x_ref, wg_ref, wu_ref, h_ref, wgbf_ref, wubf_ref):
    @pl.when(pl.program_id(1) == 0)
    def _():
        wgbf_ref[...] = wg_ref[...].astype(BF16)
        wubf_ref[...] = wu_ref[...].astype(BF16)

    x = x_ref[...]
    g = jnp.dot(x, wgbf_ref[...], preferred_element_type=F32)
    u = jnp.dot(x, wubf_ref[...], preferred_element_type=F32)
    h_ref[...] = (g * (1.0 / (1.0 + jnp.exp(-g))) * u).astype(h_ref.dtype)


def _ffn_up(x, w_gate, w_up, *, tn=256):
    m, k = x.shape
    n = w_gate.shape[1]
    tm = min(m, 1024)
    assert n % tn == 0
    wspec = pl.BlockSpec((k, tn), lambda j, i: (0, j))
    return pl.pallas_call(
        _ffn_up_kernel,
        grid=(n // tn, m // tm),
        in_specs=[pl.BlockSpec((tm, k), lambda j, i: (i, 0)), wspec, wspec],
        out_specs=pl.BlockSpec((tm, tn), lambda j, i: (i, j)),
        out_shape=jax.ShapeDtypeStruct((m, n), BF16),
        scratch_shapes=[pltpu.VMEM((k, tn), BF16)] * 2,
        compiler_params=_params(2),
        name="ffn_up",
    )(x, w_gate, w_up)


def _res_ln_kernel(x_ref, y_ref, g_ref, b_ref, o_ref, *maybe_obf_ref):
    z = DEEPNORM_ALPHA * x_ref[...] + y_ref[...].astype(F32)
    mu = jnp.mean(z, axis=-1, keepdims=True)
    d = z - mu
    var = jnp.mean(jnp.square(d), axis=-1, keepdims=True)
    out = d * lax.rsqrt(var + LN_EPS) * g_ref[...] + b_ref[...]
    o_ref[...] = out
    for obf_ref in maybe_obf_ref:
        obf_ref[...] = out.astype(BF16)


def _res_ln(x, y, gain, bias, *, with_bf16=True):
    m, d = x.shape
    tm = min(m, 256)
    row = lambda i: (i, 0)
    fixed = lambda i: (0, 0)
    n_out = 2 if with_bf16 else 1
    return pl.pallas_call(
        _res_ln_kernel,
        grid=(m // tm,),
        in_specs=[pl.BlockSpec((tm, d), row), pl.BlockSpec((tm, d), row),
                  pl.BlockSpec((1, d), fixed), pl.BlockSpec((1, d), fixed)],
        out_specs=[pl.BlockSpec((tm, d), row)] * n_out,
        out_shape=[jax.ShapeDtypeStruct((m, d), F32), jax.ShapeDtypeStruct((m, d), BF16)][:n_out],
        compiler_params=_params(1),
        name="residual_layernorm",
    )(x, y, gain.reshape(1, d), bias.reshape(1, d))


def _dil_attn_kernel(q_ref, kp_ref, kc_ref, vp_ref, vc_ref, o_ref, lse_ref):
    has_prev = pl.program_id(1) > 0
    qi = lax.broadcasted_iota(I32, (BAND, BAND), 0)
    kj = lax.broadcasted_iota(I32, (BAND, BAND), 1)
    mask_p = jnp.logical_and(kj >= qi, has_prev)
    mask_c = kj <= qi
    lse_ref[...] = jnp.zeros_like(lse_ref)
    heads = [slice(h * HEAD_DIM, (h + 1) * HEAD_DIM) for h in range(DIL_HEADS)]
    scores = [(_dot_nt(q_ref[:, sl], kp_ref[:, sl]), _dot_nt(q_ref[:, sl], kc_ref[:, sl])) for sl in heads]
    probs = []
    for h, (sp, sc) in enumerate(scores):
        sp = jnp.where(mask_p, sp, -jnp.inf)
        sc = jnp.where(mask_c, sc, -jnp.inf)
        m = jnp.max(jnp.maximum(sp, sc), axis=1, keepdims=True)
        pp = jnp.exp2(sp - m)
        pc = jnp.exp2(sc - m)
        l = jnp.sum(pp + pc, axis=1, keepdims=True)
        lse_ref[:, h:h + 1] = m + jnp.log2(l)
        probs.append((pp.astype(BF16), pc.astype(BF16), 1.0 / l))
    for sl, (pp, pc, inv_l) in zip(heads, probs):
        o = (jnp.dot(pp, vp_ref[:, sl], preferred_element_type=F32)
             + jnp.dot(pc, vc_ref[:, sl], preferred_element_type=F32))
        o_ref[:, sl] = (o * inv_l).astype(o_ref.dtype)


def _dilated_attention(qk, v, dilation, batch, seq):
    d = dilation
    n_chunks = seq // (BAND * d)
    cur = lambda col: (lambda b, ch, r: ((b * n_chunks + ch) * d + r, col))
    prev = lambda col: (lambda b, ch, r: ((b * n_chunks + jnp.maximum(ch - 1, 0)) * d + r, col))
    blk = (BAND, DIL_COLS)
    return pl.pallas_call(
        _dil_attn_kernel,
        grid=(batch, n_chunks, d),
        in_specs=[pl.BlockSpec(blk, cur(0)), pl.BlockSpec(blk, prev(1)), pl.BlockSpec(blk, cur(1)),
                  pl.BlockSpec(blk, prev(0)), pl.BlockSpec(blk, cur(0))],
        out_specs=[pl.BlockSpec(blk, cur(0)), pl.BlockSpec((BAND, HEAD_DIM), cur(0))],
        out_shape=[jax.ShapeDtypeStruct((batch * seq, DIL_COLS), BF16),
                   jax.ShapeDtypeStruct((batch * seq, HEAD_DIM), F32)],
        compiler_params=_params(3),
        name=f"dilated_attention_d{d}",
    )(qk, qk, qk, v, v)


def _dil_merge_kernel(o0_ref, o1_ref, o2_ref, l0_ref, l1_ref, l2_ref, out_ref):
    l0, l1, l2 = l0_ref[...], l1_ref[...], l2_ref[...]
    mx = jnp.maximum(jnp.maximum(l0, l1), l2)
    e0, e1, e2 = jnp.exp2(l0 - mx), jnp.exp2(l1 - mx), jnp.exp2(l2 - mx)
    inv = 1.0 / (e0 + e1 + e2)
    w0, w1, w2 = e0 * inv, e1 * inv, e2 * inv
    for h in range(DIL_HEADS):
        sl = slice(h * HEAD_DIM, (h + 1) * HEAD_DIM)
        col = slice(h, h + 1)
        acc = (w0[:, col] * o0_ref[:, sl].astype(F32) + w1[:, col] * o1_ref[:, sl].astype(F32)
               + w2[:, col] * o2_ref[:, sl].astype(F32))
        out_ref[:, sl] = acc.astype(out_ref.dtype)


def _dil_merge(outs, lses):
    m = outs[0].shape[0]
    tm = min(m, 512)
    row = lambda i: (i, 0)
    return pl.pallas_call(
        _dil_merge_kernel,
        grid=(m // tm,),
        in_specs=[pl.BlockSpec((tm, DIL_COLS), row)] * 3 + [pl.BlockSpec((tm, HEAD_DIM), row)] * 3,
        out_specs=pl.BlockSpec((tm, DIL_COLS), row),
        out_shape=jax.ShapeDtypeStruct((m, DIL_COLS), BF16),
        compiler_params=_params(1),
        name="dilated_merge",
    )(*outs, *lses)


def _sortable(x):
    b = lax.bitcast_convert_type(x, I32)
    return b ^ ((b >> 31) & np.int32(0x7FFFFFFF))


def _dsa_select_kernel(qi_ref, wit_ref, ki_ref, bias_ref, qs_ref, keys_ref, *, n_chunks):
    qb = pl.program_id(1)
    n_c = (qb * QUERY_BLOCK + QUERY_BLOCK - 1) // KEY_CHUNK + 1
    for h in range(IDX_HEADS):
        qs_ref[h * QUERY_BLOCK:(h + 1) * QUERY_BLOCK, :] = qi_ref[:, h * HEAD_DIM:(h + 1) * HEAD_DIM]
    w = wit_ref[...]
    kpos = lax.broadcasted_iota(I32, (KEY_CHUNK, QUERY_BLOCK), 0)
    t = qb * QUERY_BLOCK + lax.broadcasted_iota(I32, (KEY_CHUNK, QUERY_BLOCK), 1)
    rows_per_dot = IDX_HEADS_PER_DOT * QUERY_BLOCK

    def score_chunk(c, carry):
        kic = ki_ref[0, pl.ds(pl.multiple_of(c * KEY_CHUNK, KEY_CHUNK), KEY_CHUNK), :]
        sc = jnp.zeros((KEY_CHUNK, QUERY_BLOCK), F32)
        for hg in range(IDX_HEADS // IDX_HEADS_PER_DOT):
            r = _dot_nt(kic, qs_ref[hg * rows_per_dot:(hg + 1) * rows_per_dot, :])
            for j in range(IDX_HEADS_PER_DOT):
                h = hg * IDX_HEADS_PER_DOT + j
                sc = sc + jnp.maximum(r[:, j * QUERY_BLOCK:(j + 1) * QUERY_BLOCK], 0.0) * w[h:h + 1, :]
        causal = c * KEY_CHUNK + kpos <= t
        keys_ref[c] = jnp.where(causal, _sortable(sc), INT_MIN)
        return carry

    lax.fori_loop(0, n_c, score_chunk, 0)

    def count_ge(cand):
        def body(c, acc):
            x = jnp.where(keys_ref[c] >= cand, 1, 0).astype(I32)
            return acc + jnp.sum(x.reshape(KEY_CHUNK // 8, 8, QUERY_BLOCK), axis=0)
        acc = lax.fori_loop(0, n_c, body, jnp.zeros((8, QUERY_BLOCK), I32))
        return jnp.sum(acc, axis=0, keepdims=True)

    zero = jnp.zeros((1, QUERY_BLOCK), I32)
    thr = jnp.where(count_ge(zero) >= TOPK, zero, jnp.full((1, QUERY_BLOCK), INT_MIN, I32))

    def bit_step(i, thr):
        cand = thr | jnp.left_shift(jnp.int32(1), 30 - i)
        return jnp.where(count_ge(cand) >= TOPK, cand, thr)

    thr = lax.fori_loop(0, 31, bit_step, thr)
    thr = jnp.maximum(thr, INT_MIN + 1)
    n_ge = count_ge(thr)
    has_ties = jnp.max(n_ge) > TOPK

    @pl.when(jnp.logical_not(has_ties))
    def _():
        def write_bias(c, carry):
            bias_ref[0, 0, c] = jnp.where(keys_ref[c] >= thr, 0.0, MASK_NEG).astype(bias_ref.dtype)
            return carry
        lax.fori_loop(0, n_c, write_bias, 0)

    @pl.when(has_ties)
    def _():
        room = TOPK - count_ge(thr + 1)

        def count_tied_below(limit):
            def body(c, acc):
                tied = jnp.logical_and(keys_ref[c] == thr, c * KEY_CHUNK + kpos < limit)
                x = jnp.where(tied, 1, 0).astype(I32)
                return acc + jnp.sum(x.reshape(KEY_CHUNK // 8, 8, QUERY_BLOCK), axis=0)
            acc = lax.fori_loop(0, n_c, body, jnp.zeros((8, QUERY_BLOCK), I32))
            return jnp.sum(acc, axis=0, keepdims=True)

        n_bits = (n_chunks * KEY_CHUNK).bit_length()

        def limit_step(i, limit):
            cand = limit | jnp.left_shift(jnp.int32(1), n_bits - 1 - i)
            return jnp.where(count_tied_below(cand) <= room, cand, limit)

        limit = lax.fori_loop(0, n_bits, limit_step, jnp.zeros((1, QUERY_BLOCK), I32))

        def write_bias(c, carry):
            keys = keys_ref[c]
            keep = jnp.logical_or(keys > thr, jnp.logical_and(keys == thr, c * KEY_CHUNK + kpos < limit))
            bias_ref[0, 0, c] = jnp.where(keep, 0.0, MASK_NEG).astype(bias_ref.dtype)
            return carry
        lax.fori_loop(0, n_c, write_bias, 0)

    def write_masked(c, carry):
        bias_ref[0, 0, c] = jnp.full((KEY_CHUNK, QUERY_BLOCK), MASK_NEG, bias_ref.dtype)
        return carry

    lax.fori_loop(n_c, n_chunks, write_masked, 0)


def _dsa_select(qi, wit, ki, batch, seq):
    nqb = seq // QUERY_BLOCK
    nkc = seq // KEY_CHUNK
    return pl.pallas_call(
        functools.partial(_dsa_select_kernel, n_chunks=nkc),
        grid=(batch, nqb),
        in_specs=[pl.BlockSpec((QUERY_BLOCK, IDX_HEADS * HEAD_DIM), lambda b, q: (b * nqb + q, 0)),
                  pl.BlockSpec((HEAD_DIM, QUERY_BLOCK), lambda b, q: (0, b * nqb + q)),
                  pl.BlockSpec((1, seq, HEAD_DIM), lambda b, q: (b, 0, 0))],
        out_specs=pl.BlockSpec((1, 1, nkc, KEY_CHUNK, QUERY_BLOCK), lambda b, q: (b, q, 0, 0, 0)),
        out_shape=jax.ShapeDtypeStruct((batch, nqb, nkc, KEY_CHUNK, QUERY_BLOCK), BF16),
        scratch_shapes=[pltpu.VMEM((IDX_HEADS * QUERY_BLOCK, HEAD_DIM), BF16),
                        pltpu.VMEM((nkc, KEY_CHUNK, QUERY_BLOCK), I32)],
        compiler_params=_params(2),
        name="dsa_select",
    )(qi, wit, ki.reshape(batch, seq, HEAD_DIM))


def _dsa_attn_kernel(qb_tab, c_tab, q_ref, k_ref, v_ref, bias_ref, o_ref, qa_ref, m_ref, l_ref, acc_ref):
    step = pl.program_id(1)
    qb = qb_tab[step]
    c = c_tab[step]
    last = (qb * QUERY_BLOCK + QUERY_BLOCK - 1) // ATT_KEY_CHUNK
    n_tiles = ATT_KEY_CHUNK // HEAD_DIM

    @pl.when(c == 0)
    def _():
        ri = lax.broadcasted_iota(I32, (QUERY_BLOCK, HEAD_DIM), 0)
        ci = lax.broadcasted_iota(I32, (QUERY_BLOCK, HEAD_DIM), 1)
        eye = jnp.where(ri == ci, 1.0, 0.0).astype(BF16)
        for g in range(DSA_KV_HEADS):
            for j in range(DSA_GROUP):
                h = g * DSA_GROUP + j
                rows = slice(j * QUERY_BLOCK, (j + 1) * QUERY_BLOCK)
                qa_ref[g, rows, :HEAD_DIM] = q_ref[:, h * HEAD_DIM:(h + 1) * HEAD_DIM]
                qa_ref[g, rows, HEAD_DIM:] = eye
        m_ref[...] = jnp.full(m_ref.shape, M_INIT, F32)
        l_ref[...] = jnp.zeros_like(l_ref)
        acc_ref[...] = jnp.zeros_like(acc_ref)

    bias_t = bias_ref[0, 0, 0]

    def scores(g):
        ka = jnp.concatenate([k_ref[:, g * HEAD_DIM:(g + 1) * HEAD_DIM], bias_t], axis=1)
        return _dot_nt(qa_ref[g], ka)

    pending = [scores(g) for g in range(SCORES_AHEAD)]
    for g in range(DSA_KV_HEADS):
        sl = slice(g * HEAD_DIM, (g + 1) * HEAD_DIM)
        s = pending.pop(0)
        if g + SCORES_AHEAD < DSA_KV_HEADS:
            pending.append(scores(g + SCORES_AHEAD))
        tiles = [s[:, j * HEAD_DIM:(j + 1) * HEAD_DIM] for j in range(n_tiles)]
        m_old = m_ref[g]
        m_new = jnp.maximum(m_old, jnp.max(functools.reduce(jnp.maximum, tiles), axis=1, keepdims=True))
        alpha = jnp.exp2(m_old - m_new)
        ps = [jnp.exp2(tl - m_new) for tl in tiles]
        l_ref[g] = alpha * l_ref[g] + jnp.sum(functools.reduce(jnp.add, ps), axis=1, keepdims=True)
        p = jnp.concatenate([tl.astype(BF16) for tl in ps], axis=1)
        acc_ref[g] = alpha * acc_ref[g] + jnp.dot(p, v_ref[:, sl], preferred_element_type=F32)
        m_ref[g] = m_new

    @pl.when(c == last)
    def _():
        for g in range(DSA_KV_HEADS):
            o = acc_ref[g] / l_ref[g]
            for j in range(DSA_GROUP):
                h = g * DSA_GROUP + j
                o_ref[:, h * HEAD_DIM:(h + 1) * HEAD_DIM] = (
                    o[j * QUERY_BLOCK:(j + 1) * QUERY_BLOCK].astype(o_ref.dtype))


def _dsa_attention(q, k, v, bias_t, batch, seq):
    nqb = seq // QUERY_BLOCK
    nkc = seq // ATT_KEY_CHUNK
    rows = DSA_GROUP * QUERY_BLOCK
    steps = [(qb, c) for qb in range(nqb)
             for c in range((qb * QUERY_BLOCK + QUERY_BLOCK - 1) // ATT_KEY_CHUNK + 1)]
    qb_tab = jnp.asarray(np.array([s[0] for s in steps], np.int32))
    c_tab = jnp.asarray(np.array([s[1] for s in steps], np.int32))
    qmap = lambda b, s, qt, ct: (b * nqb + qt[s], 0)
    kvmap = lambda b, s, qt, ct: (b * nkc + ct[s], 0)
    grid_spec = pltpu.PrefetchScalarGridSpec(
        num_scalar_prefetch=2,
        grid=(batch, len(steps)),
        in_specs=[pl.BlockSpec((QUERY_BLOCK, DSA_HEADS * HEAD_DIM), qmap),
                  pl.BlockSpec((ATT_KEY_CHUNK, DSA_KV_HEADS * HEAD_DIM), kvmap),
                  pl.BlockSpec((ATT_KEY_CHUNK, DSA_KV_HEADS * HEAD_DIM), kvmap),
                  pl.BlockSpec((1, 1, 1, ATT_KEY_CHUNK, QUERY_BLOCK),
                               lambda b, s, qt, ct: (b, qt[s], ct[s], 0, 0))],
        out_specs=pl.BlockSpec((QUERY_BLOCK, DSA_HEADS * HEAD_DIM), qmap),
        scratch_shapes=[pltpu.VMEM((DSA_KV_HEADS, rows, 2 * HEAD_DIM), BF16),
                        pltpu.VMEM((DSA_KV_HEADS, rows, HEAD_DIM), F32),
                        pltpu.VMEM((DSA_KV_HEADS, rows, HEAD_DIM), F32),
                        pltpu.VMEM((DSA_KV_HEADS, rows, HEAD_DIM), F32)])
    return pl.pallas_call(
        _dsa_attn_kernel,
        grid_spec=grid_spec,
        out_shape=jax.ShapeDtypeStruct((batch * seq, DSA_HEADS * HEAD_DIM), BF16),
        compiler_params=_params(2),
        name="dsa_attention",
    )(qb_tab, c_tab, q, k, v, bias_t.reshape(batch, nqb, nkc, ATT_KEY_CHUNK, QUERY_BLOCK))


def kernel(x, positions, l0_attn_w_in, l0_attn_w_out, l1_attn_w_in, l1_k_idx_gain, l1_k_idx_bias,
           l1_attn_w_out, l0_ln_mix_gain, l0_ln_mix_bias, l0_ffn_gate, l0_ffn_up, l0_ffn_down,
           l0_ln_ffn_gain, l0_ln_ffn_bias, l1_ln_mix_gain, l1_ln_mix_bias, l1_ffn_gate, l1_ffn_up,
           l1_ffn_down, l1_ln_ffn_gain, l1_ln_ffn_bias):
    batch, seq, d_model = x.shape
    m = batch * seq
    cosf, sinf = _rope_tables(positions)
    rope = (cosf, sinf)
    xf = x.reshape(m, d_model)
    x_by_dilation = _cast_residue_major(xf, [d for _, d in DIL_GROUPS])

    tn = 512
    n_grp = len(DIL_GROUPS)
    grp_blocks = DIL_COLS // tn
    outs, lses = [], []
    for g, (_, d) in enumerate(DIL_GROUPS):
        col_block = lambda n, p0=0, g=g: ((p0 + n // grp_blocks) * n_grp + g) * grp_blocks + n % grp_blocks
        rope_g = tuple(_to_residue_major(t, d, batch, seq) for t in rope)
        x_g = x_by_dilation[g]
        qk = _matmul(x_g, l0_attn_w_in, col_block, 2 * DIL_COLS, BF16, tn=tn, rope=rope_g,
                     scale=ATTN_SCALE * LOG2E, scale_cols=DIL_COLS, name=f"l0_qk_proj_d{d}")
        v = _matmul(x_g, l0_attn_w_in, functools.partial(col_block, p0=2), DIL_COLS, BF16, tn=tn,
                    name=f"l0_v_proj_d{d}")
        o, lse = _dilated_attention(qk, v, d, batch, seq)
        outs.append(_from_residue_major(o, d, batch, seq))
        lses.append(_from_residue_major(lse, d, batch, seq))
    o0 = _dil_merge(outs, lses)
    same = lambda n: n
    y = _matmul(o0, l0_attn_w_out, same, d_model, BF16, tn=tn, name="l0_out_proj")
    xf, xb = _res_ln(xf, y, l0_ln_mix_gain, l0_ln_mix_bias)
    h = _ffn_up(xb, l0_ffn_gate, l0_ffn_up)
    y = _matmul(h, l0_ffn_down.astype(BF16), same, d_model, BF16, tn=tn, tm=512, name="l0_ffn_down")
    xf, xb = _res_ln(xf, y, l0_ln_ffn_gain, l0_ln_ffn_bias)

    b_q = DSA_HEADS * HEAD_DIM
    b_kv = DSA_KV_HEADS * HEAD_DIM
    b_qi = IDX_HEADS * HEAD_DIM
    o_v = b_q + b_kv
    o_qi = o_v + b_kv
    o_ki = o_qi + b_qi
    o_wi = o_ki + HEAD_DIM
    at = lambda col0: (lambda n: n + col0 // tn)
    w1_t = l1_attn_w_in.T
    proj1 = functools.partial(_matmul, xb, w1_t, tn=tn, w_transposed=True)
    q1 = proj1(at(0), b_q, BF16, rope=rope, scale=ATTN_SCALE * LOG2E, scale_cols=b_q, name="l1_q_proj")
    k1 = proj1(at(b_q), b_kv, BF16, rope=rope, name="l1_k_proj")
    v1 = proj1(at(o_v), b_kv, BF16, name="l1_v_proj")
    qi = proj1(at(o_qi), b_qi, BF16, rope=rope, name="l1_qidx_proj")
    w_ki = l1_attn_w_in[:, o_ki:o_wi]
    w_wi = jnp.pad(l1_attn_w_in[:, o_wi:], ((0, 0), (0, HEAD_DIM - IDX_HEADS)))
    ki, wit = _kiwi(xb, w_ki, w_wi, cosf, sinf, l1_k_idx_gain, l1_k_idx_bias)
    bias_t = _dsa_select(qi, wit, ki, batch, seq)
    o1 = _dsa_attention(q1, k1, v1, bias_t, batch, seq)
    y = _matmul(o1, l1_attn_w_out, same, d_model, BF16, tn=tn, name="l1_out_proj")
    xf, xb = _res_ln(xf, y, l1_ln_mix_gain, l1_ln_mix_bias)
    h = _ffn_up(xb, l1_ffn_gate, l1_ffn_up)
    y = _matmul(h, l1_ffn_down.astype(BF16), same, d_model, BF16, tn=tn, tm=512, name="l1_ffn_down")
    (xf,) = _res_ln(xf, y, l1_ln_ffn_gain, l1_ln_ffn_bias, with_bf16=False)
    return xf.reshape(batch, seq, d_model)
```

```python
import functools
import math

import numpy as np
import jax
import jax.numpy as jnp
from jax import lax
from jax.experimental import pallas as pl
from jax.experimental.pallas import tpu as pltpu

F32 = jnp.float32
BF16 = jnp.bfloat16
I32 = jnp.int32

HEAD_DIM = 128
HALF = HEAD_DIM // 2
ROPE_THETA = 10000.0
LN_EPS = 1e-5
DEPTH = 2
DEEPNORM_ALPHA = (2 * DEPTH) ** 0.25
ATTN_SCALE = HEAD_DIM ** -0.5
LOG2E = math.log2(math.e)

DIL_GROUPS = ((128, 1), (512, 4), (2048, 16))
BAND = 128
DIL_HEADS = 16
DIL_COLS = DIL_HEADS * HEAD_DIM

DSA_HEADS = 32
DSA_KV_HEADS = 8
DSA_GROUP = DSA_HEADS // DSA_KV_HEADS
IDX_HEADS = 32
IDX_HEADS_PER_DOT = 4
TOPK = 256
QUERY_BLOCK = 128
KEY_CHUNK = 512
ATT_KEY_CHUNK = 1024
SCORES_AHEAD = 1

INT_MIN = np.int32(-2 ** 31)
MASK_NEG = -1e30
M_INIT = -1e29

VMEM_LIMIT = 56 * 1024 * 1024

TOKEN_TILE = 1024
PROJ_TN = 512
FFN_TN = 256
DOWN_TOKEN_TILE = 512
LN_TOKEN_TILE = 256
MERGE_TOKEN_TILE = 512


def _params(n_axes):
    return pltpu.CompilerParams(dimension_semantics=("arbitrary",) * n_axes,
                                vmem_limit_bytes=VMEM_LIMIT)


def _dot_nt(a, b):
    return lax.dot_general(a, b, (((1,), (1,)), ((), ())), preferred_element_type=F32)


def _rope(a, c, s):
    return a * c + pltpu.roll(a, HALF, 1) * s


def _to_residue_major(t, d, batch, seq):
    if d == 1:
        return t
    c = t.shape[1]
    t = t.reshape(batch, seq // (BAND * d), BAND, d, c)
    return t.transpose(0, 1, 3, 2, 4).reshape(batch * seq, c)


def _from_residue_major(t, d, batch, seq):
    if d == 1:
        return t
    c = t.shape[1]
    t = t.reshape(batch, seq // (BAND * d), d, BAND, c)
    return t.transpose(0, 1, 3, 2, 4).reshape(batch * seq, c)


def _cast_residue_major_kernel(x_ref, *o_refs, dilations):
    rows = x_ref.shape[0]
    for o_ref, d in zip(o_refs, dilations):
        if d == 1:
            o_ref[...] = x_ref[...].astype(BF16)
            continue
        span = BAND * d
        for base in range(0, rows, span):
            for r in range(d):
                o_ref[base + r * BAND:base + (r + 1) * BAND, :] = (
                    x_ref[pl.ds(base + r, BAND, stride=d), :].astype(BF16))


def _cast_residue_major(x, dilations):
    m, dm = x.shape
    cols = HEAD_DIM
    rows = BAND * max(dilations)
    rows *= 2 if m % (2 * rows) == 0 else 1
    assert m % rows == 0 and dm % cols == 0 and all(rows % (BAND * d) == 0 for d in dilations)
    blk = pl.BlockSpec((rows, cols), lambda i, j: (i, j))
    return pl.pallas_call(
        functools.partial(_cast_residue_major_kernel, dilations=tuple(dilations)),
        grid=(m // rows, dm // cols),
        in_specs=[blk],
        out_specs=[blk] * len(dilations),
        out_shape=[jax.ShapeDtypeStruct((m, dm), BF16)] * len(dilations),
        compiler_params=_params(2),
        name="cast_residue_major",
    )(x)


def _rope_table_kernel(pos_ref, inv_ref, cos_ref, sin_ref):
    ang = pos_ref[...].astype(F32) * inv_ref[...]
    lane = lax.broadcasted_iota(I32, ang.shape, 1)
    cos_ref[...] = jnp.cos(ang)
    sin_ref[...] = jnp.where(lane < HALF, -jnp.sin(ang), jnp.sin(ang))


def _rope_tables(positions):
    m = positions.size
    tm = min(m, TOKEN_TILE)
    inv = ROPE_THETA ** (-jnp.arange(0, HEAD_DIM, 2, dtype=F32) / HEAD_DIM)
    inv2 = jnp.concatenate([inv, inv]).reshape(1, HEAD_DIM)
    return pl.pallas_call(
        _rope_table_kernel,
        grid=(m // tm,),
        in_specs=[pl.BlockSpec((tm, 1), lambda i: (i, 0)),
                  pl.BlockSpec((1, HEAD_DIM), lambda i: (0, 0))],
        out_specs=[pl.BlockSpec((tm, HEAD_DIM), lambda i: (i, 0))] * 2,
        out_shape=[jax.ShapeDtypeStruct((m, HEAD_DIM), F32)] * 2,
        compiler_params=_params(1),
        name="rope_tables",
    )(positions.reshape(m, 1), inv2)


MXU_WIDTH = 256


def _mm_kernel(x_ref, w_ref, *rest, w_is_f32, w_transposed, has_rope, n_blocks, scale_blocks, scale):
    rest = list(rest)
    wbf_ref = rest.pop() if w_is_f32 else w_ref
    o_ref = rest.pop()
    if w_is_f32:
        @pl.when(pl.program_id(1) == 0)
        def _():
            w = w_ref[...]
            wbf_ref[...] = (w.T if w_transposed else w).astype(BF16)

    x = x_ref[...]
    strips = [slice(j, j + MXU_WIDTH) for j in range(0, o_ref.shape[1], MXU_WIDTH)]
    accs = [jnp.dot(x, wbf_ref[:, st], preferred_element_type=F32) for st in strips]
    if not has_rope:
        for st, acc in zip(strips, accs):
            o_ref[:, st] = acc.astype(o_ref.dtype)
        return
    cos_ref, sin_ref = rest
    c, s = cos_ref[...], sin_ref[...]
    if scale_blocks:
        factor = scale if scale_blocks == n_blocks else jnp.where(pl.program_id(0) < scale_blocks, scale, 1.0)
        c, s = c * factor, s * factor
    for st, acc in zip(strips, accs):
        for j in range(0, MXU_WIDTH, HEAD_DIM):
            o_ref[:, st.start + j:st.start + j + HEAD_DIM] = (
                _rope(acc[:, j:j + HEAD_DIM], c, s).astype(o_ref.dtype))


def _matmul(x, w, col_block, n_cols, out_dtype, *, tn=PROJ_TN, tm=TOKEN_TILE, rope=None, scale=1.0, scale_cols=0,
            w_transposed=False, name):
    m, k = x.shape
    tm = min(m, tm)
    assert m % tm == 0 and n_cols % tn == 0 and tn % MXU_WIDTH == 0 and scale_cols % tn == 0
    n_blocks = n_cols // tn
    w_is_f32 = w.dtype == F32
    assert w_is_f32 or not w_transposed
    w_mode = {} if w_is_f32 else {"pipeline_mode": pl.Buffered(1)}
    in_specs = [pl.BlockSpec((tm, k), lambda n, i: (i, 0)),
                pl.BlockSpec((tn, k), lambda n, i: (col_block(n), 0)) if w_transposed else
                pl.BlockSpec((k, tn), lambda n, i: (0, col_block(n)), **w_mode)]
    args = [x, w]
    if rope is not None:
        in_specs += [pl.BlockSpec((tm, HEAD_DIM), lambda n, i: (i, 0))] * 2
        args += list(rope)
    return pl.pallas_call(
        functools.partial(_mm_kernel, w_is_f32=w_is_f32, w_transposed=w_transposed, has_rope=rope is not None,
                          n_blocks=n_blocks, scale_blocks=scale_cols // tn, scale=scale),
        grid=(n_blocks, m // tm),
        in_specs=in_specs,
        out_specs=pl.BlockSpec((tm, tn), lambda n, i: (i, n)),
        out_shape=jax.ShapeDtypeStruct((m, n_cols), out_dtype),
        scratch_shapes=[pltpu.VMEM((k, tn), BF16)] if w_is_f32 else [],
        compiler_params=_params(2),
        name=name,
    )(*args)


def _kiwi_kernel(x_ref, wk_ref, ww_ref, cos_ref, sin_ref, g_ref, b_ref, ki_ref, wit_ref, wkbf_ref, wwbf_ref):
    @pl.when(pl.program_id(0) == 0)
    def _():
        wkbf_ref[...] = wk_ref[...].astype(BF16)
        wwbf_ref[...] = ww_ref[...].astype(BF16)

    x = x_ref[...]
    a = jnp.dot(x, wkbf_ref[...], preferred_element_type=F32)
    mu = jnp.mean(a, axis=-1, keepdims=True)
    var = jnp.mean(jnp.square(a - mu), axis=-1, keepdims=True)
    y = (a - mu) * lax.rsqrt(var + LN_EPS) * g_ref[...] + b_ref[...]
    ki_ref[...] = _rope(y, cos_ref[...], sin_ref[...]).astype(ki_ref.dtype)
    wi = jnp.dot(x, wwbf_ref[...], preferred_element_type=F32) * (IDX_HEADS ** -0.5 * HEAD_DIM ** -0.5)
    wit_ref[...] = wi.T


def _kiwi(x, w_ki, w_wi, cosf, sinf, gain, bias):
    m, k = x.shape
    tm = min(m, TOKEN_TILE)
    row = lambda i: (i, 0)
    fixed = lambda i: (0, 0)
    return pl.pallas_call(
        _kiwi_kernel,
        grid=(m // tm,),
        in_specs=[pl.BlockSpec((tm, k), row), pl.BlockSpec((k, HEAD_DIM), fixed),
                  pl.BlockSpec((k, HEAD_DIM), fixed),
                  pl.BlockSpec((tm, HEAD_DIM), row), pl.BlockSpec((tm, HEAD_DIM), row),
                  pl.BlockSpec((1, HEAD_DIM), fixed), pl.BlockSpec((1, HEAD_DIM), fixed)],
        out_specs=[pl.BlockSpec((tm, HEAD_DIM), row), pl.BlockSpec((HEAD_DIM, tm), lambda i: (0, i))],
        out_shape=[jax.ShapeDtypeStruct((m, HEAD_DIM), BF16),
                   jax.ShapeDtypeStruct((HEAD_DIM, m), F32)],
        scratch_shapes=[pltpu.VMEM((k, HEAD_DIM), BF16)] * 2,
        compiler_params=_params(1),
        name="dsa_kiwi_proj",
    )(x, w_ki, w_wi, cosf, sinf, gain.reshape(1, HEAD_DIM), bias.reshape(1, HEAD_DIM))


def _ffn_up_kernel(x_ref, wg_ref, wu_ref, h_ref, wgbf_ref, wubf_ref):
    @pl.when(pl.program_id(1) == 0)
    def _():
        wgbf_ref[...] = wg_ref[...].astype(BF16)
        wubf_ref[...] = wu_ref[...].astype(BF16)

    x = x_ref[...]
    g = jnp.dot(x, wgbf_ref[...], preferred_element_type=F32)
    u = jnp.dot(x, wubf_ref[...], preferred_element_type=F32)
    h_ref[...] = (g * (1.0 / (1.0 + jnp.exp(-g))) * u).astype(h_ref.dtype)


def _ffn_up(x, w_gate, w_up, *, tn=FFN_TN):
    m, k = x.shape
    n = w_gate.shape[1]
    tm = min(m, TOKEN_TILE)
    assert n % tn == 0
    wspec = pl.BlockSpec((k, tn), lambda j, i: (0, j))
    return pl.pallas_call(
        _ffn_up_kernel,
        grid=(n // tn, m // tm),
        in_specs=[pl.BlockSpec((tm, k), lambda j, i: (i, 0)), wspec, wspec],
        out_specs=pl.BlockSpec((tm, tn), lambda j, i: (i, j)),
        out_shape=jax.ShapeDtypeStruct((m, n), BF16),
        scratch_shapes=[pltpu.VMEM((k, tn), BF16)] * 2,
        compiler_params=_params(2),
        name="ffn_up",
    )(x, w_gate, w_up)


def _res_ln_kernel(x_ref, y_ref, g_ref, b_ref, o_ref, *maybe_obf_ref):
    z = DEEPNORM_ALPHA * x_ref[...] + y_ref[...].astype(F32)
    mu = jnp.mean(z, axis=-1, keepdims=True)
    d = z - mu
    var = jnp.mean(jnp.square(d), axis=-1, keepdims=True)
    out = d * lax.rsqrt(var + LN_EPS) * g_ref[...] + b_ref[...]
    o_ref[...] = out
    for obf_ref in maybe_obf_ref:
        obf_ref[...] = out.astype(BF16)


def _res_ln(x, y, gain, bias, *, with_bf16=True):
    m, d = x.shape
    tm = min(m, LN_TOKEN_TILE)
    row = lambda i: (i, 0)
    fixed = lambda i: (0, 0)
    n_out = 2 if with_bf16 else 1
    return pl.pallas_call(
        _res_ln_kernel,
        grid=(m // tm,),
        in_specs=[pl.BlockSpec((tm, d), row), pl.BlockSpec((tm, d), row),
                  pl.BlockSpec((1, d), fixed), pl.BlockSpec((1, d), fixed)],
        out_specs=[pl.BlockSpec((tm, d), row)] * n_out,
        out_shape=[jax.ShapeDtypeStruct((m, d), F32), jax.ShapeDtypeStruct((m, d), BF16)][:n_out],
        compiler_params=_params(1),
        name="residual_layernorm",
    )(x, y, gain.reshape(1, d), bias.reshape(1, d))


def _dil_attn_kernel(q_ref, kp_ref, kc_ref, vp_ref, vc_ref, o_ref, lse_ref):
    has_prev = pl.program_id(1) > 0
    qi = lax.broadcasted_iota(I32, (BAND, BAND), 0)
    kj = lax.broadcasted_iota(I32, (BAND, BAND), 1)
    mask_p = jnp.logical_and(kj >= qi, has_prev)
    mask_c = kj <= qi
    lse_ref[...] = jnp.zeros_like(lse_ref)
    heads = [slice(h * HEAD_DIM, (h + 1) * HEAD_DIM) for h in range(DIL_HEADS)]
    scores = [(_dot_nt(q_ref[:, sl], kp_ref[:, sl]), _dot_nt(q_ref[:, sl], kc_ref[:, sl])) for sl in heads]
    probs = []
    for h, (sp, sc) in enumerate(scores):
        sp = jnp.where(mask_p, sp, -jnp.inf)
        sc = jnp.where(mask_c, sc, -jnp.inf)
        m = jnp.max(jnp.maximum(sp, sc), axis=1, keepdims=True)
        pp = jnp.exp2(sp - m)
        pc = jnp.exp2(sc - m)
        l = jnp.sum(pp + pc, axis=1, keepdims=True)
        lse_ref[:, h:h + 1] = m + jnp.log2(l)
        probs.append((pp.astype(BF16), pc.astype(BF16), 1.0 / l))
    for sl, (pp, pc, inv_l) in zip(heads, probs):
        o = (jnp.dot(pp, vp_ref[:, sl], preferred_element_type=F32)
             + jnp.dot(pc, vc_ref[:, sl], preferred_element_type=F32))
        o_ref[:, sl] = (o * inv_l).astype(o_ref.dtype)


def _dilated_attention(qk, v, dilation, batch, seq):
    d = dilation
    n_chunks = seq // (BAND * d)
    cur = lambda col: (lambda b, ch, r: ((b * n_chunks + ch) * d + r, col))
    prev = lambda col: (lambda b, ch, r: ((b * n_chunks + jnp.maximum(ch - 1, 0)) * d + r, col))
    blk = (BAND, DIL_COLS)
    return pl.pallas_call(
        _dil_attn_kernel,
        grid=(batch, n_chunks, d),
        in_specs=[pl.BlockSpec(blk, cur(0)), pl.BlockSpec(blk, prev(1)), pl.BlockSpec(blk, cur(1)),
                  pl.BlockSpec(blk, prev(0)), pl.BlockSpec(blk, cur(0))],
        out_specs=[pl.BlockSpec(blk, cur(0)), pl.BlockSpec((BAND, HEAD_DIM), cur(0))],
        out_shape=[jax.ShapeDtypeStruct((batch * seq, DIL_COLS), BF16),
                   jax.ShapeDtypeStruct((batch * seq, HEAD_DIM), F32)],
        compiler_params=_params(3),
        name=f"dilated_attention_d{d}",
    )(qk, qk, qk, v, v)


def _dil_merge_kernel(o0_ref, o1_ref, o2_ref, l0_ref, l1_ref, l2_ref, out_ref):
    l0, l1, l2 = l0_ref[...], l1_ref[...], l2_ref[...]
    mx = jnp.maximum(jnp.maximum(l0, l1), l2)
    e0, e1, e2 = jnp.exp2(l0 - mx), jnp.exp2(l1 - mx), jnp.exp2(l2 - mx)
    inv = 1.0 / (e0 + e1 + e2)
    w0, w1, w2 = e0 * inv, e1 * inv, e2 * inv
    for h in range(DIL_HEADS):
        sl = slice(h * HEAD_DIM, (h + 1) * HEAD_DIM)
        col = slice(h, h + 1)
        acc = (w0[:, col] * o0_ref[:, sl].astype(F32) + w1[:, col] * o1_ref[:, sl].astype(F32)
               + w2[:, col] * o2_ref[:, sl].astype(F32))
        out_ref[:, sl] = acc.astype(out_ref.dtype)


def _dil_merge(outs, lses):
    m = outs[0].shape[0]
    tm = min(m, MERGE_TOKEN_TILE)
    row = lambda i: (i, 0)
    return pl.pallas_call(
        _dil_merge_kernel,
        grid=(m // tm,),
        in_specs=[pl.BlockSpec((tm, DIL_COLS), row)] * 3 + [pl.BlockSpec((tm, HEAD_DIM), row)] * 3,
        out_specs=pl.BlockSpec((tm, DIL_COLS), row),
        out_shape=jax.ShapeDtypeStruct((m, DIL_COLS), BF16),
        compiler_params=_params(1),
        name="dilated_merge",
    )(*outs, *lses)


def _sortable(x):
    b = lax.bitcast_convert_type(x, I32)
    return b ^ ((b >> 31) & np.int32(0x7FFFFFFF))


def _dsa_select_kernel(qi_ref, wit_ref, ki_ref, bias_ref, qs_ref, keys_ref, *, n_chunks):
    qb = pl.program_id(1)
    n_c = (qb * QUERY_BLOCK + QUERY_BLOCK - 1) // KEY_CHUNK + 1
    for h in range(IDX_HEADS):
        qs_ref[h * QUERY_BLOCK:(h + 1) * QUERY_BLOCK, :] = qi_ref[:, h * HEAD_DIM:(h + 1) * HEAD_DIM]
    w = wit_ref[...]
    kpos = lax.broadcasted_iota(I32, (KEY_CHUNK, QUERY_BLOCK), 0)
    t = qb * QUERY_BLOCK + lax.broadcasted_iota(I32, (KEY_CHUNK, QUERY_BLOCK), 1)
    rows_per_dot = IDX_HEADS_PER_DOT * QUERY_BLOCK

    def score_chunk(c, carry):
        kic = ki_ref[0, pl.ds(pl.multiple_of(c * KEY_CHUNK, KEY_CHUNK), KEY_CHUNK), :]
        sc = jnp.zeros((KEY_CHUNK, QUERY_BLOCK), F32)
        for hg in range(IDX_HEADS // IDX_HEADS_PER_DOT):
            r = _dot_nt(kic, qs_ref[hg * rows_per_dot:(hg + 1) * rows_per_dot, :])
            for j in range(IDX_HEADS_PER_DOT):
                h = hg * IDX_HEADS_PER_DOT + j
                sc = sc + jnp.maximum(r[:, j * QUERY_BLOCK:(j + 1) * QUERY_BLOCK], 0.0) * w[h:h + 1, :]
        causal = c * KEY_CHUNK + kpos <= t
        keys_ref[c] = jnp.where(causal, _sortable(sc), INT_MIN)
        return carry

    lax.fori_loop(0, n_c, score_chunk, 0)

    @pl.when(n_c % 2 == 1)
    def _():
        keys_ref[n_c] = jnp.full((KEY_CHUNK, QUERY_BLOCK), INT_MIN, I32)

    n_pairs = (n_c + 1) // 2

    def sublane_counts(hit):
        return jnp.sum(jnp.where(hit, 1, 0).astype(I32).reshape(KEY_CHUNK // 8, 8, QUERY_BLOCK), axis=0)

    def count_ge(cand):
        def body(c2, acc):
            return (acc + sublane_counts(keys_ref[2 * c2] >= cand)
                    + sublane_counts(keys_ref[2 * c2 + 1] >= cand))
        acc = lax.fori_loop(0, n_pairs, body, jnp.zeros((8, QUERY_BLOCK), I32))
        return jnp.sum(acc, axis=0, keepdims=True)

    zero = jnp.zeros((1, QUERY_BLOCK), I32)
    thr = jnp.where(count_ge(zero) >= TOPK, zero, jnp.full((1, QUERY_BLOCK), INT_MIN, I32))

    def bit_step(i, thr):
        cand = thr | jnp.left_shift(jnp.int32(1), 30 - i)
        return jnp.where(count_ge(cand) >= TOPK, cand, thr)

    thr = lax.fori_loop(0, 31, bit_step, thr)
    thr = jnp.maximum(thr, INT_MIN + 1)
    n_ge = count_ge(thr)
    has_ties = jnp.max(n_ge) > TOPK

    @pl.when(jnp.logical_not(has_ties))
    def _():
        def write_bias(c, carry):
            bias_ref[0, 0, c] = jnp.where(keys_ref[c] >= thr, 0.0, MASK_NEG).astype(bias_ref.dtype)
            return carry
        lax.fori_loop(0, n_c, write_bias, 0)

    @pl.when(has_ties)
    def _():
        room = TOPK - count_ge(thr + 1)

        def count_tied_below(limit):
            def body(c, acc):
                tied = jnp.logical_and(keys_ref[c] == thr, c * KEY_CHUNK + kpos < limit)
                x = jnp.where(tied, 1, 0).astype(I32)
                return acc + jnp.sum(x.reshape(KEY_CHUNK // 8, 8, QUERY_BLOCK), axis=0)
            acc = lax.fori_loop(0, n_c, body, jnp.zeros((8, QUERY_BLOCK), I32))
            return jnp.sum(acc, axis=0, keepdims=True)

        n_bits = (n_chunks * KEY_CHUNK).bit_length()

        def limit_step(i, limit):
            cand = limit | jnp.left_shift(jnp.int32(1), n_bits - 1 - i)
            return jnp.where(count_tied_below(cand) <= room, cand, limit)

        limit = lax.fori_loop(0, n_bits, limit_step, jnp.zeros((1, QUERY_BLOCK), I32))

        def write_bias(c, carry):
            keys = keys_ref[c]
            keep = jnp.logical_or(keys > thr, jnp.logical_and(keys == thr, c * KEY_CHUNK + kpos < limit))
            bias_ref[0, 0, c] = jnp.where(keep, 0.0, MASK_NEG).astype(bias_ref.dtype)
            return carry
        lax.fori_loop(0, n_c, write_bias, 0)

    def write_masked(c, carry):
        bias_ref[0, 0, c] = jnp.full((KEY_CHUNK, QUERY_BLOCK), MASK_NEG, bias_ref.dtype)
        return carry

    lax.fori_loop(n_c, n_chunks, write_masked, 0)


def _dsa_select(qi, wit, ki, batch, seq):
    nqb = seq // QUERY_BLOCK
    nkc = seq // KEY_CHUNK
    return pl.pallas_call(
        functools.partial(_dsa_select_kernel, n_chunks=nkc),
        grid=(batch, nqb),
        in_specs=[pl.BlockSpec((QUERY_BLOCK, IDX_HEADS * HEAD_DIM), lambda b, q: (b * nqb + q, 0)),
                  pl.BlockSpec((HEAD_DIM, QUERY_BLOCK), lambda b, q: (0, b * nqb + q)),
                  pl.BlockSpec((1, seq, HEAD_DIM), lambda b, q: (b, 0, 0))],
        out_specs=pl.BlockSpec((1, 1, nkc, KEY_CHUNK, QUERY_BLOCK), lambda b, q: (b, q, 0, 0, 0)),
        out_shape=jax.ShapeDtypeStruct((batch, nqb, nkc, KEY_CHUNK, QUERY_BLOCK), BF16),
        scratch_shapes=[pltpu.VMEM((IDX_HEADS * QUERY_BLOCK, HEAD_DIM), BF16),
                        pltpu.VMEM((nkc + nkc % 2, KEY_CHUNK, QUERY_BLOCK), I32)],
        compiler_params=_params(2),
        name="dsa_select",
    )(qi, wit, ki.reshape(batch, seq, HEAD_DIM))


def _dsa_attn_kernel(qb_tab, c_tab, q_ref, k_ref, v_ref, bias_ref, o_ref, qa_ref, m_ref, l_ref, acc_ref):
    step = pl.program_id(1)
    qb = qb_tab[step]
    c = c_tab[step]
    last = (qb * QUERY_BLOCK + QUERY_BLOCK - 1) // ATT_KEY_CHUNK
    n_tiles = ATT_KEY_CHUNK // HEAD_DIM

    @pl.when(c == 0)
    def _():
        ri = lax.broadcasted_iota(I32, (QUERY_BLOCK, HEAD_DIM), 0)
        ci = lax.broadcasted_iota(I32, (QUERY_BLOCK, HEAD_DIM), 1)
        eye = jnp.where(ri == ci, 1.0, 0.0).astype(BF16)
        for g in range(DSA_KV_HEADS):
            for j in range(DSA_GROUP):
                h = g * DSA_GROUP + j
                rows = slice(j * QUERY_BLOCK, (j + 1) * QUERY_BLOCK)
                qa_ref[g, rows, :HEAD_DIM] = q_ref[:, h * HEAD_DIM:(h + 1) * HEAD_DIM]
                qa_ref[g, rows, HEAD_DIM:] = eye
        m_ref[...] = jnp.full(m_ref.shape, M_INIT, F32)
        l_ref[...] = jnp.zeros_like(l_ref)
        acc_ref[...] = jnp.zeros_like(acc_ref)

    bias_t = bias_ref[0, 0, 0]

    def scores(g):
        ka = jnp.concatenate([k_ref[:, g * HEAD_DIM:(g + 1) * HEAD_DIM], bias_t], axis=1)
        return _dot_nt(qa_ref[g], ka)

    pending = [scores(g) for g in range(SCORES_AHEAD)]
    for g in range(DSA_KV_HEADS):
        sl = slice(g * HEAD_DIM, (g + 1) * HEAD_DIM)
        s = pending.pop(0)
        if g + SCORES_AHEAD < DSA_KV_HEADS:
            pending.append(scores(g + SCORES_AHEAD))
        tiles = [s[:, j * HEAD_DIM:(j + 1) * HEAD_DIM] for j in range(n_tiles)]
        m_old = m_ref[g]
        m_new = jnp.maximum(m_old, jnp.max(functools.reduce(jnp.maximum, tiles), axis=1, keepdims=True))
        alpha = jnp.exp2(m_old - m_new)
        ps = [jnp.exp2(tl - m_new) for tl in tiles]
        l_ref[g] = alpha * l_ref[g] + jnp.sum(functools.reduce(jnp.add, ps), axis=1, keepdims=True)
        p = jnp.concatenate([tl.astype(BF16) for tl in ps], axis=1)
        acc_ref[g] = alpha * acc_ref[g] + jnp.dot(p, v_ref[:, sl], preferred_element_type=F32)
        m_ref[g] = m_new

    @pl.when(c == last)
    def _():
        for g in range(DSA_KV_HEADS):
            o = acc_ref[g] / l_ref[g]
            for j in range(DSA_GROUP):
                h = g * DSA_GROUP + j
                o_ref[:, h * HEAD_DIM:(h + 1) * HEAD_DIM] = (
                    o[j * QUERY_BLOCK:(j + 1) * QUERY_BLOCK].astype(o_ref.dtype))


def _dsa_attention(q, k, v, bias_t, batch, seq):
    nqb = seq // QUERY_BLOCK
    nkc = seq // ATT_KEY_CHUNK
    rows = DSA_GROUP * QUERY_BLOCK
    steps = [(qb, c) for qb in range(nqb)
             for c in range((qb * QUERY_BLOCK + QUERY_BLOCK - 1) // ATT_KEY_CHUNK + 1)]
    qb_tab = jnp.asarray(np.array([s[0] for s in steps], np.int32))
    c_tab = jnp.asarray(np.array([s[1] for s in steps], np.int32))
    qmap = lambda b, s, qt, ct: (b * nqb + qt[s], 0)
    kvmap = lambda b, s, qt, ct: (b * nkc + ct[s], 0)
    grid_spec = pltpu.PrefetchScalarGridSpec(
        num_scalar_prefetch=2,
        grid=(batch, len(steps)),
        in_specs=[pl.BlockSpec((QUERY_BLOCK, DSA_HEADS * HEAD_DIM), qmap),
                  pl.BlockSpec((ATT_KEY_CHUNK, DSA_KV_HEADS * HEAD_DIM), kvmap),
                  pl.BlockSpec((ATT_KEY_CHUNK, DSA_KV_HEADS * HEAD_DIM), kvmap),
                  pl.BlockSpec((1, 1, 1, ATT_KEY_CHUNK, QUERY_BLOCK),
                               lambda b, s, qt, ct: (b, qt[s], ct[s], 0, 0))],
        out_specs=pl.BlockSpec((QUERY_BLOCK, DSA_HEADS * HEAD_DIM), qmap),
        scratch_shapes=[pltpu.VMEM((DSA_KV_HEADS, rows, 2 * HEAD_DIM), BF16),
                        pltpu.VMEM((DSA_KV_HEADS, rows, HEAD_DIM), F32),
                        pltpu.VMEM((DSA_KV_HEADS, rows, HEAD_DIM), F32),
                        pltpu.VMEM((DSA_KV_HEADS, rows, HEAD_DIM), F32)])
    return pl.pallas_call(
        _dsa_attn_kernel,
        grid_spec=grid_spec,
        out_shape=jax.ShapeDtypeStruct((batch * seq, DSA_HEADS * HEAD_DIM), BF16),
        compiler_params=_params(2),
        name="dsa_attention",
    )(qb_tab, c_tab, q, k, v, bias_t.reshape(batch, nqb, nkc, ATT_KEY_CHUNK, QUERY_BLOCK))


def kernel(x, positions, l0_attn_w_in, l0_attn_w_out, l1_attn_w_in, l1_k_idx_gain, l1_k_idx_bias,
           l1_attn_w_out, l0_ln_mix_gain, l0_ln_mix_bias, l0_ffn_gate, l0_ffn_up, l0_ffn_down,
           l0_ln_ffn_gain, l0_ln_ffn_bias, l1_ln_mix_gain, l1_ln_mix_bias, l1_ffn_gate, l1_ffn_up,
           l1_ffn_down, l1_ln_ffn_gain, l1_ln_ffn_bias):
    batch, seq, d_model = x.shape
    m = batch * seq
    cosf, sinf = _rope_tables(positions)
    rope = (cosf, sinf)
    xf = x.reshape(m, d_model)
    x_by_dilation = _cast_residue_major(xf, [d for _, d in DIL_GROUPS])

    tn = PROJ_TN
    n_grp = len(DIL_GROUPS)
    grp_blocks = DIL_COLS // tn
    outs, lses = [], []
    for g, (_, d) in enumerate(DIL_GROUPS):
        col_block = lambda n, p0=0, g=g: ((p0 + n // grp_blocks) * n_grp + g) * grp_blocks + n % grp_blocks
        rope_g = tuple(_to_residue_major(t, d, batch, seq) for t in rope)
        x_g = x_by_dilation[g]
        qk = _matmul(x_g, l0_attn_w_in, col_block, 2 * DIL_COLS, BF16, tn=tn, rope=rope_g,
                     scale=ATTN_SCALE * LOG2E, scale_cols=DIL_COLS, name=f"l0_qk_proj_d{d}")
        v = _matmul(x_g, l0_attn_w_in, functools.partial(col_block, p0=2), DIL_COLS, BF16, tn=tn,
                    name=f"l0_v_proj_d{d}")
        o, lse = _dilated_attention(qk, v, d, batch, seq)
        outs.append(_from_residue_major(o, d, batch, seq))
        lses.append(_from_residue_major(lse, d, batch, seq))
    o0 = _dil_merge(outs, lses)
    same = lambda n: n
    y = _matmul(o0, l0_attn_w_out, same, d_model, BF16, tn=tn, name="l0_out_proj")
    xf, xb = _res_ln(xf, y, l0_ln_mix_gain, l0_ln_mix_bias)
    h = _ffn_up(xb, l0_ffn_gate, l0_ffn_up)
    y = _matmul(h, l0_ffn_down.astype(BF16), same, d_model, BF16, tm=DOWN_TOKEN_TILE, name="l0_ffn_down")
    xf, xb = _res_ln(xf, y, l0_ln_ffn_gain, l0_ln_ffn_bias)

    b_q = DSA_HEADS * HEAD_DIM
    b_kv = DSA_KV_HEADS * HEAD_DIM
    b_qi = IDX_HEADS * HEAD_DIM
    o_v = b_q + b_kv
    o_qi = o_v + b_kv
    o_ki = o_qi + b_qi
    o_wi = o_ki + HEAD_DIM
    at = lambda col0: (lambda n: n + col0 // tn)
    w1_t = l1_attn_w_in.T
    proj1 = functools.partial(_matmul, xb, w1_t, tn=tn, w_transposed=True)
    q1 = proj1(at(0), b_q, BF16, rope=rope, scale=ATTN_SCALE * LOG2E, scale_cols=b_q, name="l1_q_proj")
    k1 = proj1(at(b_q), b_kv, BF16, rope=rope, name="l1_k_proj")
    v1 = proj1(at(o_v), b_kv, BF16, name="l1_v_proj")
    qi = proj1(at(o_qi), b_qi, BF16, rope=rope, name="l1_qidx_proj")
    w_ki = l1_attn_w_in[:, o_ki:o_wi]
    w_wi = jnp.pad(l1_attn_w_in[:, o_wi:], ((0, 0), (0, HEAD_DIM - IDX_HEADS)))
    ki, wit = _kiwi(xb, w_ki, w_wi, cosf, sinf, l1_k_idx_gain, l1_k_idx_bias)
    bias_t = _dsa_select(qi, wit, ki, batch, seq)
    o1 = _dsa_attention(q1, k1, v1, bias_t, batch, seq)
    y = _matmul(o1, l1_attn_w_out, same, d_model, BF16, tn=tn, name="l1_out_proj")
    xf, xb = _res_ln(xf, y, l1_ln_mix_gain, l1_ln_mix_bias)
    h = _ffn_up(xb, l1_ffn_gate, l1_ffn_up)
    y = _matmul(h, l1_ffn_down.astype(BF16), same, d_model, BF16, tm=DOWN_TOKEN_TILE, name="l1_ffn_down")
    (xf,) = _res_ln(xf, y, l1_ln_ffn_gain, l1_ln_ffn_bias, with_bf16=False)
    return xf.reshape(batch, seq, d_model)
```

```python
import functools
import math

import numpy as np
import jax
import jax.numpy as jnp
from jax import lax
from jax.experimental import pallas as pl
from jax.experimental.pallas import tpu as pltpu

F32 = jnp.float32
BF16 = jnp.bfloat16
I32 = jnp.int32

HEAD_DIM = 128
HALF = HEAD_DIM // 2
ROPE_THETA = 10000.0
LN_EPS = 1e-5
DEPTH = 2
DEEPNORM_ALPHA = (2 * DEPTH) ** 0.25
ATTN_SCALE = HEAD_DIM ** -0.5
LOG2E = math.log2(math.e)

DIL_GROUPS = ((128, 1), (512, 4), (2048, 16))
BAND = 128
DIL_HEADS = 16
DIL_COLS = DIL_HEADS * HEAD_DIM

DSA_HEADS = 32
DSA_KV_HEADS = 8
DSA_GROUP = DSA_HEADS // DSA_KV_HEADS
IDX_HEADS = 32
IDX_HEADS_PER_DOT = 4
TOPK = 256
QUERY_BLOCK = 128
KEY_CHUNK = 512
ATT_KEY_CHUNK = 1024
SCORES_AHEAD = 1

INT_MIN = np.int32(-2 ** 31)
MASK_NEG = -1e30
M_INIT = -1e29

VMEM_LIMIT = 56 * 1024 * 1024

TOKEN_TILE = 1024
PROJ_TN = 512
FFN_TN = 256
DOWN_TOKEN_TILE = 512
LN_TOKEN_TILE = 256
MERGE_HEADS = 4


def _params(n_axes):
    return pltpu.CompilerParams(dimension_semantics=("arbitrary",) * n_axes,
                                vmem_limit_bytes=VMEM_LIMIT)


def _dot_nt(a, b):
    return lax.dot_general(a, b, (((1,), (1,)), ((), ())), preferred_element_type=F32)


def _rope(a, c, s):
    return a * c + pltpu.roll(a, HALF, 1) * s


def _to_residue_major(t, d, batch, seq):
    if d == 1:
        return t
    c = t.shape[1]
    t = t.reshape(batch, seq // (BAND * d), BAND, d, c)
    return t.transpose(0, 1, 3, 2, 4).reshape(batch * seq, c)


def _cast_residue_major_kernel(x_ref, *o_refs, dilations):
    rows = x_ref.shape[0]
    for o_ref, d in zip(o_refs, dilations):
        if d == 1:
            o_ref[...] = x_ref[...].astype(BF16)
            continue
        span = BAND * d
        for base in range(0, rows, span):
            for r in range(d):
                o_ref[base + r * BAND:base + (r + 1) * BAND, :] = (
                    x_ref[pl.ds(base + r, BAND, stride=d), :].astype(BF16))


def _cast_residue_major(x, dilations):
    m, dm = x.shape
    cols = HEAD_DIM
    rows = BAND * max(dilations)
    rows *= 2 if m % (2 * rows) == 0 else 1
    assert m % rows == 0 and dm % cols == 0 and all(rows % (BAND * d) == 0 for d in dilations)
    blk = pl.BlockSpec((rows, cols), lambda i, j: (i, j))
    return pl.pallas_call(
        functools.partial(_cast_residue_major_kernel, dilations=tuple(dilations)),
        grid=(m // rows, dm // cols),
        in_specs=[blk],
        out_specs=[blk] * len(dilations),
        out_shape=[jax.ShapeDtypeStruct((m, dm), BF16)] * len(dilations),
        compiler_params=_params(2),
        name="cast_residue_major",
    )(x)


def _rope_table_kernel(pos_ref, inv_ref, cos_ref, sin_ref):
    ang = pos_ref[...].astype(F32) * inv_ref[...]
    lane = lax.broadcasted_iota(I32, ang.shape, 1)
    cos_ref[...] = jnp.cos(ang)
    sin_ref[...] = jnp.where(lane < HALF, -jnp.sin(ang), jnp.sin(ang))


def _rope_tables(positions):
    m = positions.size
    tm = min(m, TOKEN_TILE)
    inv = ROPE_THETA ** (-jnp.arange(0, HEAD_DIM, 2, dtype=F32) / HEAD_DIM)
    inv2 = jnp.concatenate([inv, inv]).reshape(1, HEAD_DIM)
    return pl.pallas_call(
        _rope_table_kernel,
        grid=(m // tm,),
        in_specs=[pl.BlockSpec((tm, 1), lambda i: (i, 0)),
                  pl.BlockSpec((1, HEAD_DIM), lambda i: (0, 0))],
        out_specs=[pl.BlockSpec((tm, HEAD_DIM), lambda i: (i, 0))] * 2,
        out_shape=[jax.ShapeDtypeStruct((m, HEAD_DIM), F32)] * 2,
        compiler_params=_params(1),
        name="rope_tables",
    )(positions.reshape(m, 1), inv2)


MXU_WIDTH = 256


def _mm_kernel(x_ref, w_ref, *rest, w_is_f32, w_transposed, has_rope, n_blocks, scale_blocks, scale):
    rest = list(rest)
    wbf_ref = rest.pop() if w_is_f32 else w_ref
    o_ref = rest.pop()
    if w_is_f32:
        @pl.when(pl.program_id(1) == 0)
        def _():
            w = w_ref[...]
            wbf_ref[...] = (w.T if w_transposed else w).astype(BF16)

    x = x_ref[...]
    strips = [slice(j, j + MXU_WIDTH) for j in range(0, o_ref.shape[1], MXU_WIDTH)]
    accs = [jnp.dot(x, wbf_ref[:, st], preferred_element_type=F32) for st in strips]
    if not has_rope:
        for st, acc in zip(strips, accs):
            o_ref[:, st] = acc.astype(o_ref.dtype)
        return
    cos_ref, sin_ref = rest
    c, s = cos_ref[...], sin_ref[...]
    if scale_blocks:
        factor = scale if scale_blocks == n_blocks else jnp.where(pl.program_id(0) < scale_blocks, scale, 1.0)
        c, s = c * factor, s * factor
    for st, acc in zip(strips, accs):
        for j in range(0, MXU_WIDTH, HEAD_DIM):
            o_ref[:, st.start + j:st.start + j + HEAD_DIM] = (
                _rope(acc[:, j:j + HEAD_DIM], c, s).astype(o_ref.dtype))


def _matmul(x, w, col_block, n_cols, out_dtype, *, tn=PROJ_TN, tm=TOKEN_TILE, rope=None, scale=1.0, scale_cols=0,
            w_transposed=False, name):
    m, k = x.shape
    tm = min(m, tm)
    assert m % tm == 0 and n_cols % tn == 0 and tn % MXU_WIDTH == 0 and scale_cols % tn == 0
    n_blocks = n_cols // tn
    w_is_f32 = w.dtype == F32
    assert w_is_f32 or not w_transposed
    w_mode = {} if w_is_f32 else {"pipeline_mode": pl.Buffered(1)}
    in_specs = [pl.BlockSpec((tm, k), lambda n, i: (i, 0)),
                pl.BlockSpec((tn, k), lambda n, i: (col_block(n), 0)) if w_transposed else
                pl.BlockSpec((k, tn), lambda n, i: (0, col_block(n)), **w_mode)]
    args = [x, w]
    if rope is not None:
        in_specs += [pl.BlockSpec((tm, HEAD_DIM), lambda n, i: (i, 0))] * 2
        args += list(rope)
    return pl.pallas_call(
        functools.partial(_mm_kernel, w_is_f32=w_is_f32, w_transposed=w_transposed, has_rope=rope is not None,
                          n_blocks=n_blocks, scale_blocks=scale_cols // tn, scale=scale),
        grid=(n_blocks, m // tm),
        in_specs=in_specs,
        out_specs=pl.BlockSpec((tm, tn), lambda n, i: (i, n)),
        out_shape=jax.ShapeDtypeStruct((m, n_cols), out_dtype),
        scratch_shapes=[pltpu.VMEM((k, tn), BF16)] if w_is_f32 else [],
        compiler_params=_params(2),
        name=name,
    )(*args)


def _kiwi_kernel(x_ref, wk_ref, ww_ref, cos_ref, sin_ref, g_ref, b_ref, ki_ref, wit_ref, wkbf_ref, wwbf_ref):
    @pl.when(pl.program_id(0) == 0)
    def _():
        wkbf_ref[...] = wk_ref[...].astype(BF16)
        wwbf_ref[...] = ww_ref[...].astype(BF16)

    x = x_ref[...]
    a = jnp.dot(x, wkbf_ref[...], preferred_element_type=F32)
    mu = jnp.mean(a, axis=-1, keepdims=True)
    var = jnp.mean(jnp.square(a - mu), axis=-1, keepdims=True)
    y = (a - mu) * lax.rsqrt(var + LN_EPS) * g_ref[...] + b_ref[...]
    ki_ref[...] = _rope(y, cos_ref[...], sin_ref[...]).astype(ki_ref.dtype)
    wi = jnp.dot(x, wwbf_ref[...], preferred_element_type=F32) * (IDX_HEADS ** -0.5 * HEAD_DIM ** -0.5)
    wit_ref[...] = wi.T


def _kiwi(x, w_ki, w_wi, cosf, sinf, gain, bias):
    m, k = x.shape
    tm = min(m, TOKEN_TILE)
    row = lambda i: (i, 0)
    fixed = lambda i: (0, 0)
    return pl.pallas_call(
        _kiwi_kernel,
        grid=(m // tm,),
        in_specs=[pl.BlockSpec((tm, k), row), pl.BlockSpec((k, HEAD_DIM), fixed),
                  pl.BlockSpec((k, HEAD_DIM), fixed),
                  pl.BlockSpec((tm, HEAD_DIM), row), pl.BlockSpec((tm, HEAD_DIM), row),
                  pl.BlockSpec((1, HEAD_DIM), fixed), pl.BlockSpec((1, HEAD_DIM), fixed)],
        out_specs=[pl.BlockSpec((tm, HEAD_DIM), row), pl.BlockSpec((HEAD_DIM, tm), lambda i: (0, i))],
        out_shape=[jax.ShapeDtypeStruct((m, HEAD_DIM), BF16),
                   jax.ShapeDtypeStruct((HEAD_DIM, m), F32)],
        scratch_shapes=[pltpu.VMEM((k, HEAD_DIM), BF16)] * 2,
        compiler_params=_params(1),
        name="dsa_kiwi_proj",
    )(x, w_ki, w_wi, cosf, sinf, gain.reshape(1, HEAD_DIM), bias.reshape(1, HEAD_DIM))


def _ffn_up_kernel(x_ref, wg_ref, wu_ref, h_ref, wgbf_ref, wubf_ref):
    @pl.when(pl.program_id(1) == 0)
    def _():
        wgbf_ref[...] = wg_ref[...].astype(BF16)
        wubf_ref[...] = wu_ref[...].astype(BF16)

    x = x_ref[...]
    g = jnp.dot(x, wgbf_ref[...], preferred_element_type=F32)
    u = jnp.dot(x, wubf_ref[...], preferred_element_type=F32)
    h_ref[...] = (g * (1.0 / (1.0 + jnp.exp(-g))) * u).astype(h_ref.dtype)


def _ffn_up(x, w_gate, w_up, *, tn=FFN_TN):
    m, k = x.shape
    n = w_gate.shape[1]
    tm = min(m, TOKEN_TILE)
    assert n % tn == 0
    wspec = pl.BlockSpec((k, tn), lambda j, i: (0, j))
    return pl.pallas_call(
        _ffn_up_kernel,
        grid=(n // tn, m // tm),
        in_specs=[pl.BlockSpec((tm, k), lambda j, i: (i, 0)), wspec, wspec],
        out_specs=pl.BlockSpec((tm, tn), lambda j, i: (i, j)),
        out_shape=jax.ShapeDtypeStruct((m, n), BF16),
        scratch_shapes=[pltpu.VMEM((k, tn), BF16)] * 2,
        compiler_params=_params(2),
        name="ffn_up",
    )(x, w_gate, w_up)


def _res_ln_kernel(x_ref, y_ref, g_ref, b_ref, o_ref, *maybe_obf_ref):
    z = DEEPNORM_ALPHA * x_ref[...] + y_ref[...].astype(F32)
    mu = jnp.mean(z, axis=-1, keepdims=True)
    d = z - mu
    var = jnp.mean(jnp.square(d), axis=-1, keepdims=True)
    out = d * lax.rsqrt(var + LN_EPS) * g_ref[...] + b_ref[...]
    o_ref[...] = out
    for obf_ref in maybe_obf_ref:
        obf_ref[...] = out.astype(BF16)


def _res_ln(x, y, gain, bias, *, with_bf16=True):
    m, d = x.shape
    tm = min(m, LN_TOKEN_TILE)
    row = lambda i: (i, 0)
    fixed = lambda i: (0, 0)
    n_out = 2 if with_bf16 else 1
    return pl.pallas_call(
        _res_ln_kernel,
        grid=(m // tm,),
        in_specs=[pl.BlockSpec((tm, d), row), pl.BlockSpec((tm, d), row),
                  pl.BlockSpec((1, d), fixed), pl.BlockSpec((1, d), fixed)],
        out_specs=[pl.BlockSpec((tm, d), row)] * n_out,
        out_shape=[jax.ShapeDtypeStruct((m, d), F32), jax.ShapeDtypeStruct((m, d), BF16)][:n_out],
        compiler_params=_params(1),
        name="residual_layernorm",
    )(x, y, gain.reshape(1, d), bias.reshape(1, d))


def _dil_attn_kernel(q_ref, kp_ref, kc_ref, vp_ref, vc_ref, o_ref, lse_ref):
    has_prev = pl.program_id(1) > 0
    qi = lax.broadcasted_iota(I32, (BAND, BAND), 0)
    kj = lax.broadcasted_iota(I32, (BAND, BAND), 1)
    mask_p = jnp.logical_and(kj >= qi, has_prev)
    mask_c = kj <= qi
    lse_ref[...] = jnp.zeros_like(lse_ref)
    heads = [slice(h * HEAD_DIM, (h + 1) * HEAD_DIM) for h in range(DIL_HEADS)]
    scores = [(_dot_nt(q_ref[:, sl], kp_ref[:, sl]), _dot_nt(q_ref[:, sl], kc_ref[:, sl])) for sl in heads]
    probs = []
    for h, (sp, sc) in enumerate(scores):
        sp = jnp.where(mask_p, sp, -jnp.inf)
        sc = jnp.where(mask_c, sc, -jnp.inf)
        m = jnp.max(jnp.maximum(sp, sc), axis=1, keepdims=True)
        pp = jnp.exp2(sp - m)
        pc = jnp.exp2(sc - m)
        l = jnp.sum(pp + pc, axis=1, keepdims=True)
        lse_ref[h // MERGE_HEADS, :, h % MERGE_HEADS:h % MERGE_HEADS + 1] = m + jnp.log2(l)
        probs.append((pp.astype(BF16), pc.astype(BF16), 1.0 / l))
    for sl, (pp, pc, inv_l) in zip(heads, probs):
        o = (jnp.dot(pp, vp_ref[:, sl], preferred_element_type=F32)
             + jnp.dot(pc, vc_ref[:, sl], preferred_element_type=F32))
        o_ref[:, sl] = (o * inv_l).astype(o_ref.dtype)


def _dilated_attention(qk, v, dilation, batch, seq):
    d = dilation
    n_chunks = seq // (BAND * d)
    cur = lambda col: (lambda b, ch, r: ((b * n_chunks + ch) * d + r, col))
    prev = lambda col: (lambda b, ch, r: ((b * n_chunks + jnp.maximum(ch - 1, 0)) * d + r, col))
    blk = (BAND, DIL_COLS)
    planes = DIL_HEADS // MERGE_HEADS
    return pl.pallas_call(
        _dil_attn_kernel,
        grid=(batch, n_chunks, d),
        in_specs=[pl.BlockSpec(blk, cur(0)), pl.BlockSpec(blk, prev(1)), pl.BlockSpec(blk, cur(1)),
                  pl.BlockSpec(blk, prev(0)), pl.BlockSpec(blk, cur(0))],
        out_specs=[pl.BlockSpec(blk, cur(0)),
                   pl.BlockSpec((planes, BAND, HEAD_DIM), lambda b, ch, r: (0,) + cur(0)(b, ch, r))],
        out_shape=[jax.ShapeDtypeStruct((batch * seq, DIL_COLS), BF16),
                   jax.ShapeDtypeStruct((planes, batch * seq, HEAD_DIM), F32)],
        compiler_params=_params(3),
        name=f"dilated_attention_d{d}",
    )(qk, qk, qk, v, v)


def _dil_merge_kernel(*refs, dilations):
    n = len(dilations)
    o_refs, l_refs, out_ref = refs[:n], refs[n:2 * n], refs[2 * n]
    scratch = dict(zip([d for d in dilations if d > 1], refs[2 * n + 1:]))
    rows = out_ref.shape[0]

    def token_order(tile, d):
        if d == 1:
            return tile
        buf = scratch[d]
        span = BAND * d
        for base in range(0, rows, span):
            for r in range(d):
                buf[pl.ds(base + r, BAND, stride=d), :] = tile[base + r * BAND:base + (r + 1) * BAND, :]
        return buf[...]

    lses = [token_order(l_ref[0], d) for l_ref, d in zip(l_refs, dilations)]
    mx = functools.reduce(jnp.maximum, lses)
    es = [jnp.exp2(l - mx) for l in lses]
    inv = 1.0 / functools.reduce(jnp.add, es)
    ws = [e * inv for e in es]
    for j in range(MERGE_HEADS):
        sl = slice(j * HEAD_DIM, (j + 1) * HEAD_DIM)
        acc = None
        for o_ref, w, d in zip(o_refs, ws, dilations):
            term = w[:, j:j + 1] * token_order(o_ref[:, sl].astype(F32), d)
            acc = term if acc is None else acc + term
        out_ref[:, sl] = acc.astype(out_ref.dtype)


def _dil_merge(outs, lses, dilations):
    m = outs[0].shape[0]
    rows = BAND * max(dilations)
    cols = MERGE_HEADS * HEAD_DIM
    assert m % rows == 0 and all(rows % (BAND * d) == 0 for d in dilations)
    oblk = pl.BlockSpec((rows, cols), lambda i, hb: (i, hb))
    lblk = pl.BlockSpec((1, rows, HEAD_DIM), lambda i, hb: (hb, i, 0))
    n = len(dilations)
    return pl.pallas_call(
        functools.partial(_dil_merge_kernel, dilations=tuple(dilations)),
        grid=(m // rows, DIL_HEADS // MERGE_HEADS),
        in_specs=[oblk] * n + [lblk] * n,
        out_specs=oblk,
        out_shape=jax.ShapeDtypeStruct((m, DIL_COLS), BF16),
        scratch_shapes=[pltpu.VMEM((rows, HEAD_DIM), F32) for d in dilations if d > 1],
        compiler_params=_params(2),
        name="dilated_merge",
    )(*outs, *lses)


def _sortable(x):
    b = lax.bitcast_convert_type(x, I32)
    return b ^ ((b >> 31) & np.int32(0x7FFFFFFF))


def _dsa_select_kernel(qi_ref, wit_ref, ki_ref, bias_ref, qs_ref, keys_ref, *, n_chunks):
    qb = pl.program_id(1)
    n_c = (qb * QUERY_BLOCK + QUERY_BLOCK - 1) // KEY_CHUNK + 1
    for h in range(IDX_HEADS):
        qs_ref[h * QUERY_BLOCK:(h + 1) * QUERY_BLOCK, :] = qi_ref[:, h * HEAD_DIM:(h + 1) * HEAD_DIM]
    w = wit_ref[...]
    kpos = lax.broadcasted_iota(I32, (KEY_CHUNK, QUERY_BLOCK), 0)
    t = qb * QUERY_BLOCK + lax.broadcasted_iota(I32, (KEY_CHUNK, QUERY_BLOCK), 1)
    rows_per_dot = IDX_HEADS_PER_DOT * QUERY_BLOCK

    def score_chunk(c, carry):
        kic = ki_ref[0, pl.ds(pl.multiple_of(c * KEY_CHUNK, KEY_CHUNK), KEY_CHUNK), :]
        sc = jnp.zeros((KEY_CHUNK, QUERY_BLOCK), F32)
        for hg in range(IDX_HEADS // IDX_HEADS_PER_DOT):
            r = _dot_nt(kic, qs_ref[hg * rows_per_dot:(hg + 1) * rows_per_dot, :])
            for j in range(IDX_HEADS_PER_DOT):
                h = hg * IDX_HEADS_PER_DOT + j
                sc = sc + jnp.maximum(r[:, j * QUERY_BLOCK:(j + 1) * QUERY_BLOCK], 0.0) * w[h:h + 1, :]
        causal = c * KEY_CHUNK + kpos <= t
        keys_ref[c] = jnp.where(causal, _sortable(sc), INT_MIN)
        return carry

    lax.fori_loop(0, n_c, score_chunk, 0)

    @pl.when(n_c % 2 == 1)
    def _():
        keys_ref[n_c] = jnp.full((KEY_CHUNK, QUERY_BLOCK), INT_MIN, I32)

    n_pairs = (n_c + 1) // 2

    def sublane_counts(hit):
        return jnp.sum(jnp.where(hit, 1, 0).astype(I32).reshape(KEY_CHUNK // 8, 8, QUERY_BLOCK), axis=0)

    def count_ge(cand):
        def body(c2, acc):
            return (acc + sublane_counts(keys_ref[2 * c2] >= cand)
                    + sublane_counts(keys_ref[2 * c2 + 1] >= cand))
        acc = lax.fori_loop(0, n_pairs, body, jnp.zeros((8, QUERY_BLOCK), I32))
        return jnp.sum(acc, axis=0, keepdims=True)

    zero = jnp.zeros((1, QUERY_BLOCK), I32)
    thr = jnp.where(count_ge(zero) >= TOPK, zero, jnp.full((1, QUERY_BLOCK), INT_MIN, I32))

    def bit_step(i, thr):
        cand = thr | jnp.left_shift(jnp.int32(1), 30 - i)
        return jnp.where(count_ge(cand) >= TOPK, cand, thr)

    thr = lax.fori_loop(0, 31, bit_step, thr)
    thr = jnp.maximum(thr, INT_MIN + 1)
    n_ge = count_ge(thr)
    has_ties = jnp.max(n_ge) > TOPK

    @pl.when(jnp.logical_not(has_ties))
    def _():
        def write_bias(c, carry):
            bias_ref[0, 0, c] = jnp.where(keys_ref[c] >= thr, 0.0, MASK_NEG).astype(bias_ref.dtype)
            return carry
        lax.fori_loop(0, n_c, write_bias, 0)

    @pl.when(has_ties)
    def _():
        room = TOPK - count_ge(thr + 1)

        def count_tied_below(limit):
            def body(c, acc):
                tied = jnp.logical_and(keys_ref[c] == thr, c * KEY_CHUNK + kpos < limit)
                x = jnp.where(tied, 1, 0).astype(I32)
                return acc + jnp.sum(x.reshape(KEY_CHUNK // 8, 8, QUERY_BLOCK), axis=0)
            acc = lax.fori_loop(0, n_c, body, jnp.zeros((8, QUERY_BLOCK), I32))
            return jnp.sum(acc, axis=0, keepdims=True)

        n_bits = (n_chunks * KEY_CHUNK).bit_length()

        def limit_step(i, limit):
            cand = limit | jnp.left_shift(jnp.int32(1), n_bits - 1 - i)
            return jnp.where(count_tied_below(cand) <= room, cand, limit)

        limit = lax.fori_loop(0, n_bits, limit_step, jnp.zeros((1, QUERY_BLOCK), I32))

        def write_bias(c, carry):
            keys = keys_ref[c]
            keep = jnp.logical_or(keys > thr, jnp.logical_and(keys == thr, c * KEY_CHUNK + kpos < limit))
            bias_ref[0, 0, c] = jnp.where(keep, 0.0, MASK_NEG).astype(bias_ref.dtype)
            return carry
        lax.fori_loop(0, n_c, write_bias, 0)

    def write_masked(c, carry):
        bias_ref[0, 0, c] = jnp.full((KEY_CHUNK, QUERY_BLOCK), MASK_NEG, bias_ref.dtype)
        return carry

    lax.fori_loop(n_c, n_chunks, write_masked, 0)


def _dsa_select(qi, wit, ki, batch, seq):
    nqb = seq // QUERY_BLOCK
    nkc = seq // KEY_CHUNK
    return pl.pallas_call(
        functools.partial(_dsa_select_kernel, n_chunks=nkc),
        grid=(batch, nqb),
        in_specs=[pl.BlockSpec((QUERY_BLOCK, IDX_HEADS * HEAD_DIM), lambda b, q: (b * nqb + q, 0)),
                  pl.BlockSpec((HEAD_DIM, QUERY_BLOCK), lambda b, q: (0, b * nqb + q)),
                  pl.BlockSpec((1, seq, HEAD_DIM), lambda b, q: (b, 0, 0))],
        out_specs=pl.BlockSpec((1, 1, nkc, KEY_CHUNK, QUERY_BLOCK), lambda b, q: (b, q, 0, 0, 0)),
        out_shape=jax.ShapeDtypeStruct((batch, nqb, nkc, KEY_CHUNK, QUERY_BLOCK), BF16),
        scratch_shapes=[pltpu.VMEM((IDX_HEADS * QUERY_BLOCK, HEAD_DIM), BF16),
                        pltpu.VMEM((nkc + nkc % 2, KEY_CHUNK, QUERY_BLOCK), I32)],
        compiler_params=_params(2),
        name="dsa_select",
    )(qi, wit, ki.reshape(batch, seq, HEAD_DIM))


def _dsa_attn_kernel(qb_tab, c_tab, q_ref, k_ref, v_ref, bias_ref, o_ref, qa_ref, m_ref, l_ref, acc_ref):
    step = pl.program_id(1)
    qb = qb_tab[step]
    c = c_tab[step]
    last = (qb * QUERY_BLOCK + QUERY_BLOCK - 1) // ATT_KEY_CHUNK
    n_tiles = ATT_KEY_CHUNK // HEAD_DIM

    @pl.when(c == 0)
    def _():
        ri = lax.broadcasted_iota(I32, (QUERY_BLOCK, HEAD_DIM), 0)
        ci = lax.broadcasted_iota(I32, (QUERY_BLOCK, HEAD_DIM), 1)
        eye = jnp.where(ri == ci, 1.0, 0.0).astype(BF16)
        for g in range(DSA_KV_HEADS):
            for j in range(DSA_GROUP):
                h = g * DSA_GROUP + j
                rows = slice(j * QUERY_BLOCK, (j + 1) * QUERY_BLOCK)
                qa_ref[g, rows, :HEAD_DIM] = q_ref[:, h * HEAD_DIM:(h + 1) * HEAD_DIM]
                qa_ref[g, rows, HEAD_DIM:] = eye
        m_ref[...] = jnp.full(m_ref.shape, M_INIT, F32)
        l_ref[...] = jnp.zeros_like(l_ref)
        acc_ref[...] = jnp.zeros_like(acc_ref)

    bias_t = bias_ref[0, 0, 0]

    def scores(g):
        ka = jnp.concatenate([k_ref[:, g * HEAD_DIM:(g + 1) * HEAD_DIM], bias_t], axis=1)
        return _dot_nt(qa_ref[g], ka)

    pending = [scores(g) for g in range(SCORES_AHEAD)]
    for g in range(DSA_KV_HEADS):
        sl = slice(g * HEAD_DIM, (g + 1) * HEAD_DIM)
        s = pending.pop(0)
        if g + SCORES_AHEAD < DSA_KV_HEADS:
            pending.append(scores(g + SCORES_AHEAD))
        tiles = [s[:, j * HEAD_DIM:(j + 1) * HEAD_DIM] for j in range(n_tiles)]
        m_old = m_ref[g]
        m_new = jnp.maximum(m_old, jnp.max(functools.reduce(jnp.maximum, tiles), axis=1, keepdims=True))
        alpha = jnp.exp2(m_old - m_new)
        ps = [jnp.exp2(tl - m_new) for tl in tiles]
        l_ref[g] = alpha * l_ref[g] + jnp.sum(functools.reduce(jnp.add, ps), axis=1, keepdims=True)
        p = jnp.concatenate([tl.astype(BF16) for tl in ps], axis=1)
        acc_ref[g] = alpha * acc_ref[g] + jnp.dot(p, v_ref[:, sl], preferred_element_type=F32)
        m_ref[g] = m_new

    @pl.when(c == last)
    def _():
        for g in range(DSA_KV_HEADS):
            o = acc_ref[g] / l_ref[g]
            for j in range(DSA_GROUP):
                h = g * DSA_GROUP + j
                o_ref[:, h * HEAD_DIM:(h + 1) * HEAD_DIM] = (
                    o[j * QUERY_BLOCK:(j + 1) * QUERY_BLOCK].astype(o_ref.dtype))


def _dsa_attention(q, k, v, bias_t, batch, seq):
    nqb = seq // QUERY_BLOCK
    nkc = seq // ATT_KEY_CHUNK
    rows = DSA_GROUP * QUERY_BLOCK
    steps = [(qb, c) for qb in range(nqb)
             for c in range((qb * QUERY_BLOCK + QUERY_BLOCK - 1) // ATT_KEY_CHUNK + 1)]
    qb_tab = jnp.asarray(np.array([s[0] for s in steps], np.int32))
    c_tab = jnp.asarray(np.array([s[1] for s in steps], np.int32))
    qmap = lambda b, s, qt, ct: (b * nqb + qt[s], 0)
    kvmap = lambda b, s, qt, ct: (b * nkc + ct[s], 0)
    grid_spec = pltpu.PrefetchScalarGridSpec(
        num_scalar_prefetch=2,
        grid=(batch, len(steps)),
        in_specs=[pl.BlockSpec((QUERY_BLOCK, DSA_HEADS * HEAD_DIM), qmap),
                  pl.BlockSpec((ATT_KEY_CHUNK, DSA_KV_HEADS * HEAD_DIM), kvmap),
                  pl.BlockSpec((ATT_KEY_CHUNK, DSA_KV_HEADS * HEAD_DIM), kvmap),
                  pl.BlockSpec((1, 1, 1, ATT_KEY_CHUNK, QUERY_BLOCK),
                               lambda b, s, qt, ct: (b, qt[s], ct[s], 0, 0))],
        out_specs=pl.BlockSpec((QUERY_BLOCK, DSA_HEADS * HEAD_DIM), qmap),
        scratch_shapes=[pltpu.VMEM((DSA_KV_HEADS, rows, 2 * HEAD_DIM), BF16),
                        pltpu.VMEM((DSA_KV_HEADS, rows, HEAD_DIM), F32),
                        pltpu.VMEM((DSA_KV_HEADS, rows, HEAD_DIM), F32),
                        pltpu.VMEM((DSA_KV_HEADS, rows, HEAD_DIM), F32)])
    return pl.pallas_call(
        _dsa_attn_kernel,
        grid_spec=grid_spec,
        out_shape=jax.ShapeDtypeStruct((batch * seq, DSA_HEADS * HEAD_DIM), BF16),
        compiler_params=_params(2),
        name="dsa_attention",
    )(qb_tab, c_tab, q, k, v, bias_t.reshape(batch, nqb, nkc, ATT_KEY_CHUNK, QUERY_BLOCK))


def kernel(x, positions, l0_attn_w_in, l0_attn_w_out, l1_attn_w_in, l1_k_idx_gain, l1_k_idx_bias,
           l1_attn_w_out, l0_ln_mix_gain, l0_ln_mix_bias, l0_ffn_gate, l0_ffn_up, l0_ffn_down,
           l0_ln_ffn_gain, l0_ln_ffn_bias, l1_ln_mix_gain, l1_ln_mix_bias, l1_ffn_gate, l1_ffn_up,
           l1_ffn_down, l1_ln_ffn_gain, l1_ln_ffn_bias):
    batch, seq, d_model = x.shape
    m = batch * seq
    cosf, sinf = _rope_tables(positions)
    rope = (cosf, sinf)
    xf = x.reshape(m, d_model)
    x_by_dilation = _cast_residue_major(xf, [d for _, d in DIL_GROUPS])

    tn = PROJ_TN
    n_grp = len(DIL_GROUPS)
    grp_blocks = DIL_COLS // tn
    outs, lses = [], []
    for g, (_, d) in enumerate(DIL_GROUPS):
        col_block = lambda n, p0=0, g=g: ((p0 + n // grp_blocks) * n_grp + g) * grp_blocks + n % grp_blocks
        rope_g = tuple(_to_residue_major(t, d, batch, seq) for t in rope)
        x_g = x_by_dilation[g]
        qk = _matmul(x_g, l0_attn_w_in, col_block, 2 * DIL_COLS, BF16, tn=tn, rope=rope_g,
                     scale=ATTN_SCALE * LOG2E, scale_cols=DIL_COLS, name=f"l0_qk_proj_d{d}")
        v = _matmul(x_g, l0_attn_w_in, functools.partial(col_block, p0=2), DIL_COLS, BF16, tn=tn,
                    name=f"l0_v_proj_d{d}")
        o, lse = _dilated_attention(qk, v, d, batch, seq)
        outs.append(o)
        lses.append(lse)
    o0 = _dil_merge(outs, lses, [d for _, d in DIL_GROUPS])
    same = lambda n: n
    y = _matmul(o0, l0_attn_w_out, same, d_model, BF16, tn=tn, name="l0_out_proj")
    xf, xb = _res_ln(xf, y, l0_ln_mix_gain, l0_ln_mix_bias)
    h = _ffn_up(xb, l0_ffn_gate, l0_ffn_up)
    y = _matmul(h, l0_ffn_down.astype(BF16), same, d_model, BF16, tm=DOWN_TOKEN_TILE, name="l0_ffn_down")
    xf, xb = _res_ln(xf, y, l0_ln_ffn_gain, l0_ln_ffn_bias)

    b_q = DSA_HEADS * HEAD_DIM
    b_kv = DSA_KV_HEADS * HEAD_DIM
    b_qi = IDX_HEADS * HEAD_DIM
    o_v = b_q + b_kv
    o_qi = o_v + b_kv
    o_ki = o_qi + b_qi
    o_wi = o_ki + HEAD_DIM
    at = lambda col0: (lambda n: n + col0 // tn)
    w1_t = l1_attn_w_in.T
    proj1 = functools.partial(_matmul, xb, w1_t, tn=tn, w_transposed=True)
    q1 = proj1(at(0), b_q, BF16, rope=rope, scale=ATTN_SCALE * LOG2E, scale_cols=b_q, name="l1_q_proj")
    k1 = proj1(at(b_q), b_kv, BF16, rope=rope, name="l1_k_proj")
    v1 = proj1(at(o_v), b_kv, BF16, name="l1_v_proj")
    qi = proj1(at(o_qi), b_qi, BF16, rope=rope, name="l1_qidx_proj")
    w_ki = l1_attn_w_in[:, o_ki:o_wi]
    w_wi = jnp.pad(l1_attn_w_in[:, o_wi:], ((0, 0), (0, HEAD_DIM - IDX_HEADS)))
    ki, wit = _kiwi(xb, w_ki, w_wi, cosf, sinf, l1_k_idx_gain, l1_k_idx_bias)
    bias_t = _dsa_select(qi, wit, ki, batch, seq)
    o1 = _dsa_attention(q1, k1, v1, bias_t, batch, seq)
    y = _matmul(o1, l1_attn_w_out, same, d_model, BF16, tn=tn, name="l1_out_proj")
    xf, xb = _res_ln(xf, y, l1_ln_mix_gain, l1_ln_mix_bias)
    h = _ffn_up(xb, l1_ffn_gate, l1_ffn_up)
    y = _matmul(h, l1_ffn_down.astype(BF16), same, d_model, BF16, tm=DOWN_TOKEN_TILE, name="l1_ffn_down")
    (xf,) = _res_ln(xf, y, l1_ln_ffn_gain, l1_ln_ffn_bias, with_bf16=False)
    return xf.reshape(batch, seq, d_model)
```

```python
import functools
import math

import numpy as np
import jax
import jax.numpy as jnp
from jax import lax
from jax.experimental import pallas as pl
from jax.experimental.pallas import tpu as pltpu

F32 = jnp.float32
BF16 = jnp.bfloat16
I32 = jnp.int32

HEAD_DIM = 128
HALF = HEAD_DIM // 2
ROPE_THETA = 10000.0
LN_EPS = 1e-5
DEPTH = 2
DEEPNORM_ALPHA = (2 * DEPTH) ** 0.25
ATTN_SCALE = HEAD_DIM ** -0.5
LOG2E = math.log2(math.e)

DIL_GROUPS = ((128, 1), (512, 4), (2048, 16))
BAND = 128
DIL_HEADS = 16
DIL_COLS = DIL_HEADS * HEAD_DIM

DSA_HEADS = 32
DSA_KV_HEADS = 8
DSA_GROUP = DSA_HEADS // DSA_KV_HEADS
IDX_HEADS = 32
IDX_HEADS_PER_DOT = 4
TOPK = 256
QUERY_BLOCK = 128
KEY_CHUNK = 512
ATT_KEY_CHUNK = 1024
SCORES_AHEAD = 1

INT_MIN = np.int32(-2 ** 31)
MASK_NEG = -1e30
M_INIT = -1e29

VMEM_LIMIT = 56 * 1024 * 1024

TOKEN_TILE = 1024
PROJ_TN = 512
FFN_TN = 256
DOWN_TOKEN_TILE = 512
LN_TOKEN_TILE = 256
MERGE_HEADS = 4


def _params(n_axes):
    return pltpu.CompilerParams(dimension_semantics=("arbitrary",) * n_axes,
                                vmem_limit_bytes=VMEM_LIMIT)


def _dot_nt(a, b):
    return lax.dot_general(a, b, (((1,), (1,)), ((), ())), preferred_element_type=F32)


def _rope(a, c, s):
    return a * c + pltpu.roll(a, HALF, 1) * s


def _to_residue_major(t, d, batch, seq):
    if d == 1:
        return t
    c = t.shape[1]
    t = t.reshape(batch, seq // (BAND * d), BAND, d, c)
    return t.transpose(0, 1, 3, 2, 4).reshape(batch * seq, c)


def _cast_residue_major_kernel(x_ref, *o_refs, dilations):
    rows = x_ref.shape[0]
    for o_ref, d in zip(o_refs, dilations):
        if d == 1:
            o_ref[...] = x_ref[...].astype(BF16)
            continue
        span = BAND * d
        for base in range(0, rows, span):
            for r in range(d):
                o_ref[base + r * BAND:base + (r + 1) * BAND, :] = (
                    x_ref[pl.ds(base + r, BAND, stride=d), :].astype(BF16))


def _cast_residue_major(x, dilations):
    m, dm = x.shape
    cols = HEAD_DIM
    rows = BAND * max(dilations)
    rows *= 2 if m % (2 * rows) == 0 else 1
    assert m % rows == 0 and dm % cols == 0 and all(rows % (BAND * d) == 0 for d in dilations)
    blk = pl.BlockSpec((rows, cols), lambda i, j: (i, j))
    return pl.pallas_call(
        functools.partial(_cast_residue_major_kernel, dilations=tuple(dilations)),
        grid=(m // rows, dm // cols),
        in_specs=[blk],
        out_specs=[blk] * len(dilations),
        out_shape=[jax.ShapeDtypeStruct((m, dm), BF16)] * len(dilations),
        compiler_params=_params(2),
        name="cast_residue_major",
    )(x)


def _rope_table_kernel(pos_ref, inv_ref, cos_ref, sin_ref):
    ang = pos_ref[...].astype(F32) * inv_ref[...]
    lane = lax.broadcasted_iota(I32, ang.shape, 1)
    cos_ref[...] = jnp.cos(ang)
    sin_ref[...] = jnp.where(lane < HALF, -jnp.sin(ang), jnp.sin(ang))


def _rope_tables(positions):
    m = positions.size
    tm = min(m, TOKEN_TILE)
    inv = ROPE_THETA ** (-jnp.arange(0, HEAD_DIM, 2, dtype=F32) / HEAD_DIM)
    inv2 = jnp.concatenate([inv, inv]).reshape(1, HEAD_DIM)
    return pl.pallas_call(
        _rope_table_kernel,
        grid=(m // tm,),
        in_specs=[pl.BlockSpec((tm, 1), lambda i: (i, 0)),
                  pl.BlockSpec((1, HEAD_DIM), lambda i: (0, 0))],
        out_specs=[pl.BlockSpec((tm, HEAD_DIM), lambda i: (i, 0))] * 2,
        out_shape=[jax.ShapeDtypeStruct((m, HEAD_DIM), F32)] * 2,
        compiler_params=_params(1),
        name="rope_tables",
    )(positions.reshape(m, 1), inv2)


MXU_WIDTH = 256


def _mm_kernel(x_ref, w_ref, *rest, w_is_f32, w_transposed, has_rope, n_blocks, scale_blocks, scale):
    rest = list(rest)
    wbf_ref = rest.pop() if w_is_f32 else w_ref
    o_ref = rest.pop()
    if w_is_f32:
        @pl.when(pl.program_id(1) == 0)
        def _():
            w = w_ref[...]
            wbf_ref[...] = (w.T if w_transposed else w).astype(BF16)

    x = x_ref[...]
    strips = [slice(j, j + MXU_WIDTH) for j in range(0, o_ref.shape[1], MXU_WIDTH)]
    accs = [jnp.dot(x, wbf_ref[:, st], preferred_element_type=F32) for st in strips]
    if not has_rope:
        for st, acc in zip(strips, accs):
            o_ref[:, st] = acc.astype(o_ref.dtype)
        return
    cos_ref, sin_ref = rest
    c, s = cos_ref[...], sin_ref[...]
    if scale_blocks:
        factor = scale if scale_blocks == n_blocks else jnp.where(pl.program_id(0) < scale_blocks, scale, 1.0)
        c, s = c * factor, s * factor
    for st, acc in zip(strips, accs):
        for j in range(0, MXU_WIDTH, HEAD_DIM):
            o_ref[:, st.start + j:st.start + j + HEAD_DIM] = (
                _rope(acc[:, j:j + HEAD_DIM], c, s).astype(o_ref.dtype))


def _matmul(x, w, col_block, n_cols, out_dtype, *, tn=PROJ_TN, tm=TOKEN_TILE, rope=None, scale=1.0, scale_cols=0,
            w_transposed=False, name):
    m, k = x.shape
    tm = min(m, tm)
    assert m % tm == 0 and n_cols % tn == 0 and tn % MXU_WIDTH == 0 and scale_cols % tn == 0
    n_blocks = n_cols // tn
    w_is_f32 = w.dtype == F32
    assert w_is_f32 or not w_transposed
    w_mode = {} if w_is_f32 else {"pipeline_mode": pl.Buffered(1)}
    in_specs = [pl.BlockSpec((tm, k), lambda n, i: (i, 0)),
                pl.BlockSpec((tn, k), lambda n, i: (col_block(n), 0)) if w_transposed else
                pl.BlockSpec((k, tn), lambda n, i: (0, col_block(n)), **w_mode)]
    args = [x, w]
    if rope is not None:
        in_specs += [pl.BlockSpec((tm, HEAD_DIM), lambda n, i: (i, 0))] * 2
        args += list(rope)
    return pl.pallas_call(
        functools.partial(_mm_kernel, w_is_f32=w_is_f32, w_transposed=w_transposed, has_rope=rope is not None,
                          n_blocks=n_blocks, scale_blocks=scale_cols // tn, scale=scale),
        grid=(n_blocks, m // tm),
        in_specs=in_specs,
        out_specs=pl.BlockSpec((tm, tn), lambda n, i: (i, n)),
        out_shape=jax.ShapeDtypeStruct((m, n_cols), out_dtype),
        scratch_shapes=[pltpu.VMEM((k, tn), BF16)] if w_is_f32 else [],
        compiler_params=_params(2),
        name=name,
    )(*args)


def _kiwi_kernel(x_ref, wk_ref, ww_ref, cos_ref, sin_ref, g_ref, b_ref, ki_ref, wit_ref, wkbf_ref, wwbf_ref):
    @pl.when(pl.program_id(0) == 0)
    def _():
        wkbf_ref[...] = wk_ref[...].astype(BF16)
        wwbf_ref[...] = ww_ref[...].astype(BF16)

    x = x_ref[...]
    a = jnp.dot(x, wkbf_ref[...], preferred_element_type=F32)
    mu = jnp.mean(a, axis=-1, keepdims=True)
    var = jnp.mean(jnp.square(a - mu), axis=-1, keepdims=True)
    y = (a - mu) * lax.rsqrt(var + LN_EPS) * g_ref[...] + b_ref[...]
    ki_ref[...] = _rope(y, cos_ref[...], sin_ref[...]).astype(ki_ref.dtype)
    wi = jnp.dot(x, wwbf_ref[...], preferred_element_type=F32) * (IDX_HEADS ** -0.5 * HEAD_DIM ** -0.5)
    wit_ref[...] = wi.T


def _kiwi(x, w_ki, w_wi, cosf, sinf, gain, bias):
    m, k = x.shape
    tm = min(m, TOKEN_TILE)
    row = lambda i: (i, 0)
    fixed = lambda i: (0, 0)
    return pl.pallas_call(
        _kiwi_kernel,
        grid=(m // tm,),
        in_specs=[pl.BlockSpec((tm, k), row), pl.BlockSpec((k, HEAD_DIM), fixed),
                  pl.BlockSpec((k, HEAD_DIM), fixed),
                  pl.BlockSpec((tm, HEAD_DIM), row), pl.BlockSpec((tm, HEAD_DIM), row),
                  pl.BlockSpec((1, HEAD_DIM), fixed), pl.BlockSpec((1, HEAD_DIM), fixed)],
        out_specs=[pl.BlockSpec((tm, HEAD_DIM), row), pl.BlockSpec((HEAD_DIM, tm), lambda i: (0, i))],
        out_shape=[jax.ShapeDtypeStruct((m, HEAD_DIM), BF16),
                   jax.ShapeDtypeStruct((HEAD_DIM, m), F32)],
        scratch_shapes=[pltpu.VMEM((k, HEAD_DIM), BF16)] * 2,
        compiler_params=_params(1),
        name="dsa_kiwi_proj",
    )(x, w_ki, w_wi, cosf, sinf, gain.reshape(1, HEAD_DIM), bias.reshape(1, HEAD_DIM))


def _ffn_up_kernel(x_ref, wg_ref, wu_ref, h_ref, wgbf_ref, wubf_ref):
    @pl.when(pl.program_id(1) == 0)
    def _():
        wgbf_ref[...] = wg_ref[...].astype(BF16)
        wubf_ref[...] = wu_ref[...].astype(BF16)

    x = x_ref[...]
    g = jnp.dot(x, wgbf_ref[...], preferred_element_type=F32)
    u = jnp.dot(x, wubf_ref[...], preferred_element_type=F32)
    h_ref[...] = (g * (1.0 / (1.0 + jnp.exp(-g))) * u).astype(h_ref.dtype)


def _ffn_up(x, w_gate, w_up, *, tn=FFN_TN):
    m, k = x.shape
    n = w_gate.shape[1]
    tm = min(m, TOKEN_TILE)
    assert n % tn == 0
    wspec = pl.BlockSpec((k, tn), lambda j, i: (0, j))
    return pl.pallas_call(
        _ffn_up_kernel,
        grid=(n // tn, m // tm),
        in_specs=[pl.BlockSpec((tm, k), lambda j, i: (i, 0)), wspec, wspec],
        out_specs=pl.BlockSpec((tm, tn), lambda j, i: (i, j)),
        out_shape=jax.ShapeDtypeStruct((m, n), BF16),
        scratch_shapes=[pltpu.VMEM((k, tn), BF16)] * 2,
        compiler_params=_params(2),
        name="ffn_up",
    )(x, w_gate, w_up)


def _res_ln_kernel(x_ref, y_ref, g_ref, b_ref, o_ref, *maybe_obf_ref):
    z = DEEPNORM_ALPHA * x_ref[...] + y_ref[...].astype(F32)
    mu = jnp.mean(z, axis=-1, keepdims=True)
    d = z - mu
    var = jnp.mean(jnp.square(d), axis=-1, keepdims=True)
    out = d * lax.rsqrt(var + LN_EPS) * g_ref[...] + b_ref[...]
    o_ref[...] = out
    for obf_ref in maybe_obf_ref:
        obf_ref[...] = out.astype(BF16)


def _res_ln(x, y, gain, bias, *, with_bf16=True):
    m, d = x.shape
    tm = min(m, LN_TOKEN_TILE)
    row = lambda i: (i, 0)
    fixed = lambda i: (0, 0)
    n_out = 2 if with_bf16 else 1
    return pl.pallas_call(
        _res_ln_kernel,
        grid=(m // tm,),
        in_specs=[pl.BlockSpec((tm, d), row), pl.BlockSpec((tm, d), row),
                  pl.BlockSpec((1, d), fixed), pl.BlockSpec((1, d), fixed)],
        out_specs=[pl.BlockSpec((tm, d), row)] * n_out,
        out_shape=[jax.ShapeDtypeStruct((m, d), F32), jax.ShapeDtypeStruct((m, d), BF16)][:n_out],
        compiler_params=_params(1),
        name="residual_layernorm",
    )(x, y, gain.reshape(1, d), bias.reshape(1, d))


def _dil_attn_kernel(q_ref, kp_ref, kc_ref, vp_ref, vc_ref, o_ref, lse_ref):
    has_prev = pl.program_id(1) > 0
    qi = lax.broadcasted_iota(I32, (BAND, BAND), 0)
    kj = lax.broadcasted_iota(I32, (BAND, BAND), 1)
    mask_p = jnp.logical_and(kj >= qi, has_prev)
    mask_c = kj <= qi
    lse_ref[...] = jnp.zeros_like(lse_ref)
    heads = [slice(h * HEAD_DIM, (h + 1) * HEAD_DIM) for h in range(DIL_HEADS)]
    scores = [(_dot_nt(q_ref[:, sl], kp_ref[:, sl]), _dot_nt(q_ref[:, sl], kc_ref[:, sl])) for sl in heads]
    probs = []
    for h, (sp, sc) in enumerate(scores):
        sp = jnp.where(mask_p, sp, -jnp.inf)
        sc = jnp.where(mask_c, sc, -jnp.inf)
        m = jnp.max(jnp.maximum(sp, sc), axis=1, keepdims=True)
        pp = jnp.exp2(sp - m)
        pc = jnp.exp2(sc - m)
        l = jnp.sum(pp + pc, axis=1, keepdims=True)
        lse_ref[h // MERGE_HEADS, :, h % MERGE_HEADS:h % MERGE_HEADS + 1] = m + jnp.log2(l)
        probs.append((pp.astype(BF16), pc.astype(BF16), 1.0 / l))
    for sl, (pp, pc, inv_l) in zip(heads, probs):
        o = (jnp.dot(pp, vp_ref[:, sl], preferred_element_type=F32)
             + jnp.dot(pc, vc_ref[:, sl], preferred_element_type=F32))
        o_ref[:, sl] = (o * inv_l).astype(o_ref.dtype)


def _dilated_attention(qk, v, dilation, batch, seq):
    d = dilation
    n_chunks = seq // (BAND * d)
    cur = lambda col: (lambda b, ch, r: ((b * n_chunks + ch) * d + r, col))
    prev = lambda col: (lambda b, ch, r: ((b * n_chunks + jnp.maximum(ch - 1, 0)) * d + r, col))
    blk = (BAND, DIL_COLS)
    planes = DIL_HEADS // MERGE_HEADS
    return pl.pallas_call(
        _dil_attn_kernel,
        grid=(batch, n_chunks, d),
        in_specs=[pl.BlockSpec(blk, cur(0)), pl.BlockSpec(blk, prev(1)), pl.BlockSpec(blk, cur(1)),
                  pl.BlockSpec(blk, prev(0)), pl.BlockSpec(blk, cur(0))],
        out_specs=[pl.BlockSpec(blk, cur(0)),
                   pl.BlockSpec((planes, BAND, HEAD_DIM), lambda b, ch, r: (0,) + cur(0)(b, ch, r))],
        out_shape=[jax.ShapeDtypeStruct((batch * seq, DIL_COLS), BF16),
                   jax.ShapeDtypeStruct((planes, batch * seq, HEAD_DIM), F32)],
        compiler_params=_params(3),
        name=f"dilated_attention_d{d}",
    )(qk, qk, qk, v, v)


def _dil_merge_kernel(*refs, dilations):
    n = len(dilations)
    o_refs, l_refs, out_ref = refs[:n], refs[n:2 * n], refs[2 * n]
    scratch = dict(zip([d for d in dilations if d > 1], refs[2 * n + 1:]))
    rows = out_ref.shape[0]

    def token_order(tile, d):
        if d == 1:
            return tile
        buf = scratch[d]
        span = BAND * d
        for base in range(0, rows, span):
            for r in range(d):
                buf[pl.ds(base + r, BAND, stride=d), :] = tile[base + r * BAND:base + (r + 1) * BAND, :]
        return buf[...]

    lses = [token_order(l_ref[0], d) for l_ref, d in zip(l_refs, dilations)]
    mx = functools.reduce(jnp.maximum, lses)
    es = [jnp.exp2(l - mx) for l in lses]
    inv = 1.0 / functools.reduce(jnp.add, es)
    ws = [e * inv for e in es]
    for j in range(MERGE_HEADS):
        sl = slice(j * HEAD_DIM, (j + 1) * HEAD_DIM)
        acc = None
        for o_ref, w, d in zip(o_refs, ws, dilations):
            term = w[:, j:j + 1] * token_order(o_ref[:, sl].astype(F32), d)
            acc = term if acc is None else acc + term
        out_ref[:, sl] = acc.astype(out_ref.dtype)


def _dil_merge(outs, lses, dilations):
    m = outs[0].shape[0]
    rows = BAND * max(dilations)
    cols = MERGE_HEADS * HEAD_DIM
    assert m % rows == 0 and all(rows % (BAND * d) == 0 for d in dilations)
    oblk = pl.BlockSpec((rows, cols), lambda i, hb: (i, hb))
    lblk = pl.BlockSpec((1, rows, HEAD_DIM), lambda i, hb: (hb, i, 0))
    n = len(dilations)
    return pl.pallas_call(
        functools.partial(_dil_merge_kernel, dilations=tuple(dilations)),
        grid=(m // rows, DIL_HEADS // MERGE_HEADS),
        in_specs=[oblk] * n + [lblk] * n,
        out_specs=oblk,
        out_shape=jax.ShapeDtypeStruct((m, DIL_COLS), BF16),
        scratch_shapes=[pltpu.VMEM((rows, HEAD_DIM), F32) for d in dilations if d > 1],
        compiler_params=_params(2),
        name="dilated_merge",
    )(*outs, *lses)


def _sortable(x):
    b = lax.bitcast_convert_type(x, I32)
    return b ^ ((b >> 31) & np.int32(0x7FFFFFFF))


def _dsa_select_kernel(qi_ref, wit_ref, ki_ref, bias_ref, qs_ref, keys_ref, *, n_chunks):
    qb = pl.program_id(1)
    n_c = (qb * QUERY_BLOCK + QUERY_BLOCK - 1) // KEY_CHUNK + 1
    for h in range(IDX_HEADS):
        qs_ref[h * QUERY_BLOCK:(h + 1) * QUERY_BLOCK, :] = qi_ref[:, h * HEAD_DIM:(h + 1) * HEAD_DIM]
    w = wit_ref[...]
    kpos = lax.broadcasted_iota(I32, (KEY_CHUNK, QUERY_BLOCK), 0)
    t = qb * QUERY_BLOCK + lax.broadcasted_iota(I32, (KEY_CHUNK, QUERY_BLOCK), 1)
    rows_per_dot = IDX_HEADS_PER_DOT * QUERY_BLOCK

    def score_chunk(c):
        kic = ki_ref[0, pl.ds(pl.multiple_of(c * KEY_CHUNK, KEY_CHUNK), KEY_CHUNK), :]
        sc = jnp.zeros((KEY_CHUNK, QUERY_BLOCK), F32)
        for hg in range(IDX_HEADS // IDX_HEADS_PER_DOT):
            r = _dot_nt(kic, qs_ref[hg * rows_per_dot:(hg + 1) * rows_per_dot, :])
            for j in range(IDX_HEADS_PER_DOT):
                h = hg * IDX_HEADS_PER_DOT + j
                sc = sc + jnp.maximum(r[:, j * QUERY_BLOCK:(j + 1) * QUERY_BLOCK], 0.0) * w[h:h + 1, :]
        causal = c * KEY_CHUNK + kpos <= t
        keys_ref[c] = jnp.where(causal, _sortable(sc), INT_MIN)

    def score_pair(c2, carry):
        score_chunk(2 * c2)
        score_chunk(2 * c2 + 1)
        return carry

    lax.fori_loop(0, n_c // 2, score_pair, 0)

    @pl.when(n_c % 2 == 1)
    def _():
        score_chunk(n_c - 1)
        keys_ref[n_c] = jnp.full((KEY_CHUNK, QUERY_BLOCK), INT_MIN, I32)

    n_pairs = (n_c + 1) // 2

    def sublane_counts(hit):
        return jnp.sum(jnp.where(hit, 1, 0).astype(I32).reshape(KEY_CHUNK // 8, 8, QUERY_BLOCK), axis=0)

    def count_ge(cand):
        def body(c2, acc):
            return (acc + sublane_counts(keys_ref[2 * c2] >= cand)
                    + sublane_counts(keys_ref[2 * c2 + 1] >= cand))
        acc = lax.fori_loop(0, n_pairs, body, jnp.zeros((8, QUERY_BLOCK), I32))
        return jnp.sum(acc, axis=0, keepdims=True)

    zero = jnp.zeros((1, QUERY_BLOCK), I32)
    n_zero = count_ge(zero)
    thr = jnp.where(n_zero >= TOPK, zero, jnp.full((1, QUERY_BLOCK), INT_MIN, I32))
    n_ge = jnp.where(n_zero >= TOPK, n_zero, zero)

    def bit_step(i, carry):
        thr, n_ge = carry
        cand = thr | jnp.left_shift(jnp.int32(1), 30 - i)
        n_cand = count_ge(cand)
        take = n_cand >= TOPK
        return jnp.where(take, cand, thr), jnp.where(take, n_cand, n_ge)

    thr, n_ge = lax.fori_loop(0, 31, bit_step, (thr, n_ge))
    thr = jnp.maximum(thr, INT_MIN + 1)
    has_ties = jnp.max(n_ge) > TOPK

    @pl.when(jnp.logical_not(has_ties))
    def _():
        def write_bias(c, carry):
            bias_ref[0, 0, c] = jnp.where(keys_ref[c] >= thr, 0.0, MASK_NEG).astype(bias_ref.dtype)
            return carry
        lax.fori_loop(0, n_c, write_bias, 0)

    @pl.when(has_ties)
    def _():
        room = TOPK - count_ge(thr + 1)

        def count_tied_below(limit):
            def body(c, acc):
                tied = jnp.logical_and(keys_ref[c] == thr, c * KEY_CHUNK + kpos < limit)
                x = jnp.where(tied, 1, 0).astype(I32)
                return acc + jnp.sum(x.reshape(KEY_CHUNK // 8, 8, QUERY_BLOCK), axis=0)
            acc = lax.fori_loop(0, n_c, body, jnp.zeros((8, QUERY_BLOCK), I32))
            return jnp.sum(acc, axis=0, keepdims=True)

        n_bits = (n_chunks * KEY_CHUNK).bit_length()

        def limit_step(i, limit):
            cand = limit | jnp.left_shift(jnp.int32(1), n_bits - 1 - i)
            return jnp.where(count_tied_below(cand) <= room, cand, limit)

        limit = lax.fori_loop(0, n_bits, limit_step, jnp.zeros((1, QUERY_BLOCK), I32))

        def write_bias(c, carry):
            keys = keys_ref[c]
            keep = jnp.logical_or(keys > thr, jnp.logical_and(keys == thr, c * KEY_CHUNK + kpos < limit))
            bias_ref[0, 0, c] = jnp.where(keep, 0.0, MASK_NEG).astype(bias_ref.dtype)
            return carry
        lax.fori_loop(0, n_c, write_bias, 0)

    def write_masked(c, carry):
        bias_ref[0, 0, c] = jnp.full((KEY_CHUNK, QUERY_BLOCK), MASK_NEG, bias_ref.dtype)
        return carry

    lax.fori_loop(n_c, n_chunks, write_masked, 0)


def _dsa_select(qi, wit, ki, batch, seq):
    nqb = seq // QUERY_BLOCK
    nkc = seq // KEY_CHUNK
    return pl.pallas_call(
        functools.partial(_dsa_select_kernel, n_chunks=nkc),
        grid=(batch, nqb),
        in_specs=[pl.BlockSpec((QUERY_BLOCK, IDX_HEADS * HEAD_DIM), lambda b, q: (b * nqb + q, 0)),
                  pl.BlockSpec((HEAD_DIM, QUERY_BLOCK), lambda b, q: (0, b * nqb + q)),
                  pl.BlockSpec((1, seq, HEAD_DIM), lambda b, q: (b, 0, 0))],
        out_specs=pl.BlockSpec((1, 1, nkc, KEY_CHUNK, QUERY_BLOCK), lambda b, q: (b, q, 0, 0, 0)),
        out_shape=jax.ShapeDtypeStruct((batch, nqb, nkc, KEY_CHUNK, QUERY_BLOCK), BF16),
        scratch_shapes=[pltpu.VMEM((IDX_HEADS * QUERY_BLOCK, HEAD_DIM), BF16),
                        pltpu.VMEM((nkc + nkc % 2, KEY_CHUNK, QUERY_BLOCK), I32)],
        compiler_params=_params(2),
        name="dsa_select",
    )(qi, wit, ki.reshape(batch, seq, HEAD_DIM))


def _dsa_attn_kernel(qb_tab, c_tab, q_ref, k_ref, v_ref, bias_ref, o_ref, qa_ref, m_ref, l_ref, acc_ref):
    step = pl.program_id(1)
    qb = qb_tab[step]
    c = c_tab[step]
    last = (qb * QUERY_BLOCK + QUERY_BLOCK - 1) // ATT_KEY_CHUNK
    n_tiles = ATT_KEY_CHUNK // HEAD_DIM

    @pl.when(c == 0)
    def _():
        ri = lax.broadcasted_iota(I32, (QUERY_BLOCK, HEAD_DIM), 0)
        ci = lax.broadcasted_iota(I32, (QUERY_BLOCK, HEAD_DIM), 1)
        eye = jnp.where(ri == ci, 1.0, 0.0).astype(BF16)
        for g in range(DSA_KV_HEADS):
            for j in range(DSA_GROUP):
                h = g * DSA_GROUP + j
                rows = slice(j * QUERY_BLOCK, (j + 1) * QUERY_BLOCK)
                qa_ref[g, rows, :HEAD_DIM] = q_ref[:, h * HEAD_DIM:(h + 1) * HEAD_DIM]
                qa_ref[g, rows, HEAD_DIM:] = eye
        m_ref[...] = jnp.full(m_ref.shape, M_INIT, F32)
        l_ref[...] = jnp.zeros_like(l_ref)
        acc_ref[...] = jnp.zeros_like(acc_ref)

    bias_t = bias_ref[0, 0, 0]

    def scores(g):
        ka = jnp.concatenate([k_ref[:, g * HEAD_DIM:(g + 1) * HEAD_DIM], bias_t], axis=1)
        return _dot_nt(qa_ref[g], ka)

    pending = [scores(g) for g in range(SCORES_AHEAD)]
    for g in range(DSA_KV_HEADS):
        sl = slice(g * HEAD_DIM, (g + 1) * HEAD_DIM)
        s = pending.pop(0)
        if g + SCORES_AHEAD < DSA_KV_HEADS:
            pending.append(scores(g + SCORES_AHEAD))
        tiles = [s[:, j * HEAD_DIM:(j + 1) * HEAD_DIM] for j in range(n_tiles)]
        m_old = m_ref[g]
        m_new = jnp.maximum(m_old, jnp.max(functools.reduce(jnp.maximum, tiles), axis=1, keepdims=True))
        alpha = jnp.exp2(m_old - m_new)
        ps = [jnp.exp2(tl - m_new) for tl in tiles]
        l_ref[g] = alpha * l_ref[g] + jnp.sum(functools.reduce(jnp.add, ps), axis=1, keepdims=True)
        p = jnp.concatenate([tl.astype(BF16) for tl in ps], axis=1)
        acc_ref[g] = alpha * acc_ref[g] + jnp.dot(p, v_ref[:, sl], preferred_element_type=F32)
        m_ref[g] = m_new

    @pl.when(c == last)
    def _():
        for g in range(DSA_KV_HEADS):
            o = acc_ref[g] / l_ref[g]
            for j in range(DSA_GROUP):
                h = g * DSA_GROUP + j
                o_ref[:, h * HEAD_DIM:(h + 1) * HEAD_DIM] = (
                    o[j * QUERY_BLOCK:(j + 1) * QUERY_BLOCK].astype(o_ref.dtype))


def _dsa_attention(q, k, v, bias_t, batch, seq):
    nqb = seq // QUERY_BLOCK
    nkc = seq // ATT_KEY_CHUNK
    rows = DSA_GROUP * QUERY_BLOCK
    steps = [(qb, c) for qb in range(nqb)
             for c in range((qb * QUERY_BLOCK + QUERY_BLOCK - 1) // ATT_KEY_CHUNK + 1)]
    qb_tab = jnp.asarray(np.array([s[0] for s in steps], np.int32))
    c_tab = jnp.asarray(np.array([s[1] for s in steps], np.int32))
    qmap = lambda b, s, qt, ct: (b * nqb + qt[s], 0)
    kvmap = lambda b, s, qt, ct: (b * nkc + ct[s], 0)
    grid_spec = pltpu.PrefetchScalarGridSpec(
        num_scalar_prefetch=2,
        grid=(batch, len(steps)),
        in_specs=[pl.BlockSpec((QUERY_BLOCK, DSA_HEADS * HEAD_DIM), qmap),
                  pl.BlockSpec((ATT_KEY_CHUNK, DSA_KV_HEADS * HEAD_DIM), kvmap),
                  pl.BlockSpec((ATT_KEY_CHUNK, DSA_KV_HEADS * HEAD_DIM), kvmap),
                  pl.BlockSpec((1, 1, 1, ATT_KEY_CHUNK, QUERY_BLOCK),
                               lambda b, s, qt, ct: (b, qt[s], ct[s], 0, 0))],
        out_specs=pl.BlockSpec((QUERY_BLOCK, DSA_HEADS * HEAD_DIM), qmap),
        scratch_shapes=[pltpu.VMEM((DSA_KV_HEADS, rows, 2 * HEAD_DIM), BF16),
                        pltpu.VMEM((DSA_KV_HEADS, rows, HEAD_DIM), F32),
                        pltpu.VMEM((DSA_KV_HEADS, rows, HEAD_DIM), F32),
                        pltpu.VMEM((DSA_KV_HEADS, rows, HEAD_DIM), F32)])
    return pl.pallas_call(
        _dsa_attn_kernel,
        grid_spec=grid_spec,
        out_shape=jax.ShapeDtypeStruct((batch * seq, DSA_HEADS * HEAD_DIM), BF16),
        compiler_params=_params(2),
        name="dsa_attention",
    )(qb_tab, c_tab, q, k, v, bias_t.reshape(batch, nqb, nkc, ATT_KEY_CHUNK, QUERY_BLOCK))


def kernel(x, positions, l0_attn_w_in, l0_attn_w_out, l1_attn_w_in, l1_k_idx_gain, l1_k_idx_bias,
           l1_attn_w_out, l0_ln_mix_gain, l0_ln_mix_bias, l0_ffn_gate, l0_ffn_up, l0_ffn_down,
           l0_ln_ffn_gain, l0_ln_ffn_bias, l1_ln_mix_gain, l1_ln_mix_bias, l1_ffn_gate, l1_ffn_up,
           l1_ffn_down, l1_ln_ffn_gain, l1_ln_ffn_bias):
    batch, seq, d_model = x.shape
    m = batch * seq
    cosf, sinf = _rope_tables(positions)
    rope = (cosf, sinf)
    xf = x.reshape(m, d_model)
    x_by_dilation = _cast_residue_major(xf, [d for _, d in DIL_GROUPS])

    tn = PROJ_TN
    n_grp = len(DIL_GROUPS)
    grp_blocks = DIL_COLS // tn
    outs, lses = [], []
    for g, (_, d) in enumerate(DIL_GROUPS):
        col_block = lambda n, p0=0, g=g: ((p0 + n // grp_blocks) * n_grp + g) * grp_blocks + n % grp_blocks
        rope_g = tuple(_to_residue_major(t, d, batch, seq) for t in rope)
        x_g = x_by_dilation[g]
        qk = _matmul(x_g, l0_attn_w_in, col_block, 2 * DIL_COLS, BF16, tn=tn, rope=rope_g,
                     scale=ATTN_SCALE * LOG2E, scale_cols=DIL_COLS, name=f"l0_qk_proj_d{d}")
        v = _matmul(x_g, l0_attn_w_in, functools.partial(col_block, p0=2), DIL_COLS, BF16, tn=tn,
                    name=f"l0_v_proj_d{d}")
        o, lse = _dilated_attention(qk, v, d, batch, seq)
        outs.append(o)
        lses.append(lse)
    o0 = _dil_merge(outs, lses, [d for _, d in DIL_GROUPS])
    same = lambda n: n
    y = _matmul(o0, l0_attn_w_out, same, d_model, BF16, tn=tn, name="l0_out_proj")
    xf, xb = _res_ln(xf, y, l0_ln_mix_gain, l0_ln_mix_bias)
    h = _ffn_up(xb, l0_ffn_gate, l0_ffn_up)
    y = _matmul(h, l0_ffn_down.astype(BF16), same, d_model, BF16, tm=DOWN_TOKEN_TILE, name="l0_ffn_down")
    xf, xb = _res_ln(xf, y, l0_ln_ffn_gain, l0_ln_ffn_bias)

    b_q = DSA_HEADS * HEAD_DIM
    b_kv = DSA_KV_HEADS * HEAD_DIM
    b_qi = IDX_HEADS * HEAD_DIM
    o_v = b_q + b_kv
    o_qi = o_v + b_kv
    o_ki = o_qi + b_qi
    o_wi = o_ki + HEAD_DIM
    at = lambda col0: (lambda n: n + col0 // tn)
    w1_t = l1_attn_w_in.T
    proj1 = functools.partial(_matmul, xb, w1_t, tn=tn, w_transposed=True)
    q1 = proj1(at(0), b_q, BF16, rope=rope, scale=ATTN_SCALE * LOG2E, scale_cols=b_q, name="l1_q_proj")
    k1 = proj1(at(b_q), b_kv, BF16, rope=rope, name="l1_k_proj")
    v1 = proj1(at(o_v), b_kv, BF16, name="l1_v_proj")
    qi = proj1(at(o_qi), b_qi, BF16, rope=rope, name="l1_qidx_proj")
    w_ki = l1_attn_w_in[:, o_ki:o_wi]
    w_wi = jnp.pad(l1_attn_w_in[:, o_wi:], ((0, 0), (0, HEAD_DIM - IDX_HEADS)))
    ki, wit = _kiwi(xb, w_ki, w_wi, cosf, sinf, l1_k_idx_gain, l1_k_idx_bias)
    bias_t = _dsa_select(qi, wit, ki, batch, seq)
    o1 = _dsa_attention(q1, k1, v1, bias_t, batch, seq)
    y = _matmul(o1, l1_attn_w_out, same, d_model, BF16, tn=tn, name="l1_out_proj")
    xf, xb = _res_ln(xf, y, l1_ln_mix_gain, l1_ln_mix_bias)
    h = _ffn_up(xb, l1_ffn_gate, l1_ffn_up)
    y = _matmul(h, l1_ffn_down.astype(BF16), same, d_model, BF16, tm=DOWN_TOKEN_TILE, name="l1_ffn_down")
    (xf,) = _res_ln(xf, y, l1_ln_ffn_gain, l1_ln_ffn_bias, with_bf16=False)
    return xf.reshape(batch, seq, d_model)
```

```python
import functools
import math

import numpy as np
import jax
import jax.numpy as jnp
from jax import lax
from jax.experimental import pallas as pl
from jax.experimental.pallas import tpu as pltpu

F32 = jnp.float32
BF16 = jnp.bfloat16
I32 = jnp.int32

HEAD_DIM = 128
HALF = HEAD_DIM // 2
ROPE_THETA = 10000.0
LN_EPS = 1e-5
DEPTH = 2
DEEPNORM_ALPHA = (2 * DEPTH) ** 0.25
ATTN_SCALE = HEAD_DIM ** -0.5
LOG2E = math.log2(math.e)

DIL_GROUPS = ((128, 1), (512, 4), (2048, 16))
BAND = 128
DIL_HEADS = 16
DIL_COLS = DIL_HEADS * HEAD_DIM

DSA_HEADS = 32
DSA_KV_HEADS = 8
DSA_GROUP = DSA_HEADS // DSA_KV_HEADS
IDX_HEADS = 32
IDX_HEADS_PER_DOT = 4
TOPK = 256
QUERY_BLOCK = 128
KEY_CHUNK = 512
ATT_KEY_CHUNK = 1024
SCORES_AHEAD = 1

INT_MIN = np.int32(-2 ** 31)
MASK_NEG = -1e30
M_INIT = -1e29

VMEM_LIMIT = 56 * 1024 * 1024

TOKEN_TILE = 1024
PROJ_TN = 512
FFN_TN = 256
FFN_TOKEN_TILE = 2048
DOWN_TOKEN_TILE = 512
LN_TOKEN_TILE = 256
MERGE_HEADS = 4


def _params(n_axes):
    return pltpu.CompilerParams(dimension_semantics=("arbitrary",) * n_axes,
                                vmem_limit_bytes=VMEM_LIMIT)


def _dot_nt(a, b):
    return lax.dot_general(a, b, (((1,), (1,)), ((), ())), preferred_element_type=F32)


def _rope(a, c, s):
    return a * c + pltpu.roll(a, HALF, 1) * s


def _to_residue_major(t, d, batch, seq):
    if d == 1:
        return t
    c = t.shape[1]
    t = t.reshape(batch, seq // (BAND * d), BAND, d, c)
    return t.transpose(0, 1, 3, 2, 4).reshape(batch * seq, c)


def _cast_residue_major_kernel(x_ref, *o_refs, dilations):
    rows = x_ref.shape[0]
    for o_ref, d in zip(o_refs, dilations):
        if d == 1:
            o_ref[...] = x_ref[...].astype(BF16)
            continue
        span = BAND * d
        for base in range(0, rows, span):
            for r in range(d):
                o_ref[base + r * BAND:base + (r + 1) * BAND, :] = (
                    x_ref[pl.ds(base + r, BAND, stride=d), :].astype(BF16))


def _cast_residue_major(x, dilations):
    m, dm = x.shape
    cols = HEAD_DIM
    rows = BAND * max(dilations)
    rows *= 2 if m % (2 * rows) == 0 else 1
    assert m % rows == 0 and dm % cols == 0 and all(rows % (BAND * d) == 0 for d in dilations)
    blk = pl.BlockSpec((rows, cols), lambda i, j: (i, j))
    return pl.pallas_call(
        functools.partial(_cast_residue_major_kernel, dilations=tuple(dilations)),
        grid=(m // rows, dm // cols),
        in_specs=[blk],
        out_specs=[blk] * len(dilations),
        out_shape=[jax.ShapeDtypeStruct((m, dm), BF16)] * len(dilations),
        compiler_params=_params(2),
        name="cast_residue_major",
    )(x)


def _rope_table_kernel(pos_ref, inv_ref, cos_ref, sin_ref):
    ang = pos_ref[...].astype(F32) * inv_ref[...]
    lane = lax.broadcasted_iota(I32, ang.shape, 1)
    cos_ref[...] = jnp.cos(ang)
    sin_ref[...] = jnp.where(lane < HALF, -jnp.sin(ang), jnp.sin(ang))


def _rope_tables(positions):
    m = positions.size
    tm = min(m, TOKEN_TILE)
    inv = ROPE_THETA ** (-jnp.arange(0, HEAD_DIM, 2, dtype=F32) / HEAD_DIM)
    inv2 = jnp.concatenate([inv, inv]).reshape(1, HEAD_DIM)
    return pl.pallas_call(
        _rope_table_kernel,
        grid=(m // tm,),
        in_specs=[pl.BlockSpec((tm, 1), lambda i: (i, 0)),
                  pl.BlockSpec((1, HEAD_DIM), lambda i: (0, 0))],
        out_specs=[pl.BlockSpec((tm, HEAD_DIM), lambda i: (i, 0))] * 2,
        out_shape=[jax.ShapeDtypeStruct((m, HEAD_DIM), F32)] * 2,
        compiler_params=_params(1),
        name="rope_tables",
    )(positions.reshape(m, 1), inv2)


MXU_WIDTH = 256


def _mm_kernel(x_ref, w_ref, *rest, w_is_f32, w_transposed, has_rope, n_blocks, scale_blocks, scale):
    rest = list(rest)
    wbf_ref = rest.pop() if w_is_f32 else w_ref
    o_ref = rest.pop()
    if w_is_f32:
        @pl.when(pl.program_id(1) == 0)
        def _():
            w = w_ref[...]
            wbf_ref[...] = (w.T if w_transposed else w).astype(BF16)

    x = x_ref[...]
    strips = [slice(j, j + MXU_WIDTH) for j in range(0, o_ref.shape[1], MXU_WIDTH)]
    accs = [jnp.dot(x, wbf_ref[:, st], preferred_element_type=F32) for st in strips]
    if not has_rope:
        for st, acc in zip(strips, accs):
            o_ref[:, st] = acc.astype(o_ref.dtype)
        return
    cos_ref, sin_ref = rest
    c, s = cos_ref[...], sin_ref[...]
    if scale_blocks:
        factor = scale if scale_blocks == n_blocks else jnp.where(pl.program_id(0) < scale_blocks, scale, 1.0)
        c, s = c * factor, s * factor
    for st, acc in zip(strips, accs):
        for j in range(0, MXU_WIDTH, HEAD_DIM):
            o_ref[:, st.start + j:st.start + j + HEAD_DIM] = (
                _rope(acc[:, j:j + HEAD_DIM], c, s).astype(o_ref.dtype))


def _matmul(x, w, col_block, n_cols, out_dtype, *, tn=PROJ_TN, tm=TOKEN_TILE, rope=None, scale=1.0, scale_cols=0,
            w_transposed=False, name):
    m, k = x.shape
    tm = min(m, tm)
    assert m % tm == 0 and n_cols % tn == 0 and tn % MXU_WIDTH == 0 and scale_cols % tn == 0
    n_blocks = n_cols // tn
    w_is_f32 = w.dtype == F32
    assert w_is_f32 or not w_transposed
    w_mode = {} if w_is_f32 else {"pipeline_mode": pl.Buffered(1)}
    in_specs = [pl.BlockSpec((tm, k), lambda n, i: (i, 0)),
                pl.BlockSpec((tn, k), lambda n, i: (col_block(n), 0)) if w_transposed else
                pl.BlockSpec((k, tn), lambda n, i: (0, col_block(n)), **w_mode)]
    args = [x, w]
    if rope is not None:
        in_specs += [pl.BlockSpec((tm, HEAD_DIM), lambda n, i: (i, 0))] * 2
        args += list(rope)
    return pl.pallas_call(
        functools.partial(_mm_kernel, w_is_f32=w_is_f32, w_transposed=w_transposed, has_rope=rope is not None,
                          n_blocks=n_blocks, scale_blocks=scale_cols // tn, scale=scale),
        grid=(n_blocks, m // tm),
        in_specs=in_specs,
        out_specs=pl.BlockSpec((tm, tn), lambda n, i: (i, n)),
        out_shape=jax.ShapeDtypeStruct((m, n_cols), out_dtype),
        scratch_shapes=[pltpu.VMEM((k, tn), BF16)] if w_is_f32 else [],
        compiler_params=_params(2),
        name=name,
    )(*args)


def _kiwi_kernel(x_ref, wk_ref, ww_ref, cos_ref, sin_ref, g_ref, b_ref, ki_ref, wit_ref, wkbf_ref, wwbf_ref):
    @pl.when(pl.program_id(0) == 0)
    def _():
        wkbf_ref[...] = wk_ref[...].astype(BF16)
        wwbf_ref[...] = ww_ref[...].astype(BF16)

    x = x_ref[...]
    a = jnp.dot(x, wkbf_ref[...], preferred_element_type=F32)
    mu = jnp.mean(a, axis=-1, keepdims=True)
    var = jnp.mean(jnp.square(a - mu), axis=-1, keepdims=True)
    y = (a - mu) * lax.rsqrt(var + LN_EPS) * g_ref[...] + b_ref[...]
    ki_ref[...] = _rope(y, cos_ref[...], sin_ref[...]).astype(ki_ref.dtype)
    wi = jnp.dot(x, wwbf_ref[...], preferred_element_type=F32) * (IDX_HEADS ** -0.5 * HEAD_DIM ** -0.5)
    wit_ref[...] = wi.T


def _kiwi(x, w_ki, w_wi, cosf, sinf, gain, bias):
    m, k = x.shape
    tm = min(m, TOKEN_TILE)
    row = lambda i: (i, 0)
    fixed = lambda i: (0, 0)
    return pl.pallas_call(
        _kiwi_kernel,
        grid=(m // tm,),
        in_specs=[pl.BlockSpec((tm, k), row), pl.BlockSpec((k, HEAD_DIM), fixed),
                  pl.BlockSpec((k, HEAD_DIM), fixed),
                  pl.BlockSpec((tm, HEAD_DIM), row), pl.BlockSpec((tm, HEAD_DIM), row),
                  pl.BlockSpec((1, HEAD_DIM), fixed), pl.BlockSpec((1, HEAD_DIM), fixed)],
        out_specs=[pl.BlockSpec((tm, HEAD_DIM), row), pl.BlockSpec((HEAD_DIM, tm), lambda i: (0, i))],
        out_shape=[jax.ShapeDtypeStruct((m, HEAD_DIM), BF16),
                   jax.ShapeDtypeStruct((HEAD_DIM, m), F32)],
        scratch_shapes=[pltpu.VMEM((k, HEAD_DIM), BF16)] * 2,
        compiler_params=_params(1),
        name="dsa_kiwi_proj",
    )(x, w_ki, w_wi, cosf, sinf, gain.reshape(1, HEAD_DIM), bias.reshape(1, HEAD_DIM))


def _ffn_up_kernel(x_ref, wg_ref, wu_ref, h_ref, wgbf_ref, wubf_ref):
    @pl.when(pl.program_id(1) == 0)
    def _():
        wgbf_ref[...] = wg_ref[...].astype(BF16)
        wubf_ref[...] = wu_ref[...].astype(BF16)

    for r in range(0, x_ref.shape[0], TOKEN_TILE):
        rows = slice(r, min(r + TOKEN_TILE, x_ref.shape[0]))
        x = x_ref[rows, :]
        g = jnp.dot(x, wgbf_ref[...], preferred_element_type=F32)
        u = jnp.dot(x, wubf_ref[...], preferred_element_type=F32)
        h_ref[rows, :] = (g * (1.0 / (1.0 + jnp.exp(-g))) * u).astype(h_ref.dtype)


def _ffn_up(x, w_gate, w_up, *, tn=FFN_TN):
    m, k = x.shape
    n = w_gate.shape[1]
    tm = min(m, FFN_TOKEN_TILE)
    assert n % tn == 0 and m % tm == 0
    wspec = pl.BlockSpec((k, tn), lambda j, i: (0, j))
    return pl.pallas_call(
        _ffn_up_kernel,
        grid=(n // tn, m // tm),
        in_specs=[pl.BlockSpec((tm, k), lambda j, i: (i, 0)), wspec, wspec],
        out_specs=pl.BlockSpec((tm, tn), lambda j, i: (i, j)),
        out_shape=jax.ShapeDtypeStruct((m, n), BF16),
        scratch_shapes=[pltpu.VMEM((k, tn), BF16)] * 2,
        compiler_params=_params(2),
        name="ffn_up",
    )(x, w_gate, w_up)


def _res_ln_kernel(x_ref, y_ref, g_ref, b_ref, o_ref, *maybe_obf_ref):
    z = DEEPNORM_ALPHA * x_ref[...] + y_ref[...].astype(F32)
    mu = jnp.mean(z, axis=-1, keepdims=True)
    d = z - mu
    var = jnp.mean(jnp.square(d), axis=-1, keepdims=True)
    out = d * lax.rsqrt(var + LN_EPS) * g_ref[...] + b_ref[...]
    o_ref[...] = out
    for obf_ref in maybe_obf_ref:
        obf_ref[...] = out.astype(BF16)


def _res_ln(x, y, gain, bias, *, with_bf16=True):
    m, d = x.shape
    tm = min(m, LN_TOKEN_TILE)
    row = lambda i: (i, 0)
    fixed = lambda i: (0, 0)
    n_out = 2 if with_bf16 else 1
    return pl.pallas_call(
        _res_ln_kernel,
        grid=(m // tm,),
        in_specs=[pl.BlockSpec((tm, d), row), pl.BlockSpec((tm, d), row),
                  pl.BlockSpec((1, d), fixed), pl.BlockSpec((1, d), fixed)],
        out_specs=[pl.BlockSpec((tm, d), row)] * n_out,
        out_shape=[jax.ShapeDtypeStruct((m, d), F32), jax.ShapeDtypeStruct((m, d), BF16)][:n_out],
        compiler_params=_params(1),
        name="residual_layernorm",
    )(x, y, gain.reshape(1, d), bias.reshape(1, d))


def _dil_attn_kernel(q_ref, kp_ref, kc_ref, vp_ref, vc_ref, o_ref, lse_ref):
    has_prev = pl.program_id(1) > 0
    qi = lax.broadcasted_iota(I32, (BAND, BAND), 0)
    kj = lax.broadcasted_iota(I32, (BAND, BAND), 1)
    mask_p = jnp.logical_and(kj >= qi, has_prev)
    mask_c = kj <= qi
    lse_ref[...] = jnp.zeros_like(lse_ref)
    heads = [slice(h * HEAD_DIM, (h + 1) * HEAD_DIM) for h in range(DIL_HEADS)]
    scores = [(_dot_nt(q_ref[:, sl], kp_ref[:, sl]), _dot_nt(q_ref[:, sl], kc_ref[:, sl])) for sl in heads]
    probs = []
    for h, (sp, sc) in enumerate(scores):
        sp = jnp.where(mask_p, sp, -jnp.inf)
        sc = jnp.where(mask_c, sc, -jnp.inf)
        m = jnp.max(jnp.maximum(sp, sc), axis=1, keepdims=True)
        pp = jnp.exp2(sp - m)
        pc = jnp.exp2(sc - m)
        l = jnp.sum(pp + pc, axis=1, keepdims=True)
        lse_ref[h // MERGE_HEADS, :, h % MERGE_HEADS:h % MERGE_HEADS + 1] = m + jnp.log2(l)
        probs.append((pp.astype(BF16), pc.astype(BF16), 1.0 / l))
    for sl, (pp, pc, inv_l) in zip(heads, probs):
        o = (jnp.dot(pp, vp_ref[:, sl], preferred_element_type=F32)
             + jnp.dot(pc, vc_ref[:, sl], preferred_element_type=F32))
        o_ref[:, sl] = (o * inv_l).astype(o_ref.dtype)


def _dilated_attention(qk, v, dilation, batch, seq):
    d = dilation
    n_chunks = seq // (BAND * d)
    cur = lambda col: (lambda b, ch, r: ((b * n_chunks + ch) * d + r, col))
    prev = lambda col: (lambda b, ch, r: ((b * n_chunks + jnp.maximum(ch - 1, 0)) * d + r, col))
    blk = (BAND, DIL_COLS)
    planes = DIL_HEADS // MERGE_HEADS
    return pl.pallas_call(
        _dil_attn_kernel,
        grid=(batch, n_chunks, d),
        in_specs=[pl.BlockSpec(blk, cur(0)), pl.BlockSpec(blk, prev(1)), pl.BlockSpec(blk, cur(1)),
                  pl.BlockSpec(blk, prev(0)), pl.BlockSpec(blk, cur(0))],
        out_specs=[pl.BlockSpec(blk, cur(0)),
                   pl.BlockSpec((planes, BAND, HEAD_DIM), lambda b, ch, r: (0,) + cur(0)(b, ch, r))],
        out_shape=[jax.ShapeDtypeStruct((batch * seq, DIL_COLS), BF16),
                   jax.ShapeDtypeStruct((planes, batch * seq, HEAD_DIM), F32)],
        compiler_params=_params(3),
        name=f"dilated_attention_d{d}",
    )(qk, qk, qk, v, v)


def _dil_merge_kernel(*refs, dilations):
    n = len(dilations)
    o_refs, l_refs, out_ref = refs[:n], refs[n:2 * n], refs[2 * n]
    scratch = dict(zip([d for d in dilations if d > 1], refs[2 * n + 1:]))
    rows = out_ref.shape[0]

    def token_order(tile, d):
        if d == 1:
            return tile
        buf = scratch[d]
        span = BAND * d
        for base in range(0, rows, span):
            for r in range(d):
                buf[pl.ds(base + r, BAND, stride=d), :] = tile[base + r * BAND:base + (r + 1) * BAND, :]
        return buf[...]

    lses = [token_order(l_ref[0], d) for l_ref, d in zip(l_refs, dilations)]
    mx = functools.reduce(jnp.maximum, lses)
    es = [jnp.exp2(l - mx) for l in lses]
    inv = 1.0 / functools.reduce(jnp.add, es)
    ws = [e * inv for e in es]
    for j in range(MERGE_HEADS):
        sl = slice(j * HEAD_DIM, (j + 1) * HEAD_DIM)
        acc = None
        for o_ref, w, d in zip(o_refs, ws, dilations):
            term = w[:, j:j + 1] * token_order(o_ref[:, sl].astype(F32), d)
            acc = term if acc is None else acc + term
        out_ref[:, sl] = acc.astype(out_ref.dtype)


def _dil_merge(outs, lses, dilations):
    m = outs[0].shape[0]
    rows = BAND * max(dilations)
    cols = MERGE_HEADS * HEAD_DIM
    assert m % rows == 0 and all(rows % (BAND * d) == 0 for d in dilations)
    oblk = pl.BlockSpec((rows, cols), lambda i, hb: (i, hb))
    lblk = pl.BlockSpec((1, rows, HEAD_DIM), lambda i, hb: (hb, i, 0))
    n = len(dilations)
    return pl.pallas_call(
        functools.partial(_dil_merge_kernel, dilations=tuple(dilations)),
        grid=(m // rows, DIL_HEADS // MERGE_HEADS),
        in_specs=[oblk] * n + [lblk] * n,
        out_specs=oblk,
        out_shape=jax.ShapeDtypeStruct((m, DIL_COLS), BF16),
        scratch_shapes=[pltpu.VMEM((rows, HEAD_DIM), F32) for d in dilations if d > 1],
        compiler_params=_params(2),
        name="dilated_merge",
    )(*outs, *lses)


def _sortable(x):
    b = lax.bitcast_convert_type(x, I32)
    return b ^ ((b >> 31) & np.int32(0x7FFFFFFF))


def _dsa_select_kernel(qi_ref, wit_ref, ki_ref, bias_ref, qs_ref, keys_ref, *, n_chunks):
    qb = pl.program_id(1)
    n_c = (qb * QUERY_BLOCK + QUERY_BLOCK - 1) // KEY_CHUNK + 1
    for h in range(IDX_HEADS):
        qs_ref[h * QUERY_BLOCK:(h + 1) * QUERY_BLOCK, :] = qi_ref[:, h * HEAD_DIM:(h + 1) * HEAD_DIM]
    w = wit_ref[...]
    kpos = lax.broadcasted_iota(I32, (KEY_CHUNK, QUERY_BLOCK), 0)
    t = qb * QUERY_BLOCK + lax.broadcasted_iota(I32, (KEY_CHUNK, QUERY_BLOCK), 1)
    rows_per_dot = IDX_HEADS_PER_DOT * QUERY_BLOCK

    def score_chunk(c):
        kic = ki_ref[0, pl.ds(pl.multiple_of(c * KEY_CHUNK, KEY_CHUNK), KEY_CHUNK), :]
        sc = jnp.zeros((KEY_CHUNK, QUERY_BLOCK), F32)
        for hg in range(IDX_HEADS // IDX_HEADS_PER_DOT):
            r = _dot_nt(kic, qs_ref[hg * rows_per_dot:(hg + 1) * rows_per_dot, :])
            for j in range(IDX_HEADS_PER_DOT):
                h = hg * IDX_HEADS_PER_DOT + j
                sc = sc + jnp.maximum(r[:, j * QUERY_BLOCK:(j + 1) * QUERY_BLOCK], 0.0) * w[h:h + 1, :]
        causal = c * KEY_CHUNK + kpos <= t
        keys_ref[c] = jnp.where(causal, _sortable(sc), INT_MIN)

    def score_pair(c2, carry):
        score_chunk(2 * c2)
        score_chunk(2 * c2 + 1)
        return carry

    lax.fori_loop(0, n_c // 2, score_pair, 0)

    @pl.when(n_c % 2 == 1)
    def _():
        score_chunk(n_c - 1)
        keys_ref[n_c] = jnp.full((KEY_CHUNK, QUERY_BLOCK), INT_MIN, I32)

    n_pairs = (n_c + 1) // 2

    def sublane_counts(hit):
        return jnp.sum(jnp.where(hit, 1, 0).astype(I32).reshape(KEY_CHUNK // 8, 8, QUERY_BLOCK), axis=0)

    def count_ge(cand):
        def body(c2, acc):
            return (acc + sublane_counts(keys_ref[2 * c2] >= cand)
                    + sublane_counts(keys_ref[2 * c2 + 1] >= cand))
        acc = lax.fori_loop(0, n_pairs, body, jnp.zeros((8, QUERY_BLOCK), I32))
        return jnp.sum(acc, axis=0, keepdims=True)

    zero = jnp.zeros((1, QUERY_BLOCK), I32)
    n_zero = count_ge(zero)
    thr = jnp.where(n_zero >= TOPK, zero, jnp.full((1, QUERY_BLOCK), INT_MIN, I32))
    n_ge = jnp.where(n_zero >= TOPK, n_zero, zero)

    def bit_step(i, carry):
        thr, n_ge = carry
        cand = thr | jnp.left_shift(jnp.int32(1), 30 - i)
        n_cand = count_ge(cand)
        take = n_cand >= TOPK
        return jnp.where(take, cand, thr), jnp.where(take, n_cand, n_ge)

    thr, n_ge = lax.fori_loop(0, 31, bit_step, (thr, n_ge))
    thr = jnp.maximum(thr, INT_MIN + 1)
    has_ties = jnp.max(n_ge) > TOPK

    @pl.when(jnp.logical_not(has_ties))
    def _():
        def write_bias(c, carry):
            bias_ref[0, 0, c] = jnp.where(keys_ref[c] >= thr, 0.0, MASK_NEG).astype(bias_ref.dtype)
            return carry
        lax.fori_loop(0, n_c, write_bias, 0)

    @pl.when(has_ties)
    def _():
        room = TOPK - count_ge(thr + 1)

        def count_tied_below(limit):
            def body(c, acc):
                tied = jnp.logical_and(keys_ref[c] == thr, c * KEY_CHUNK + kpos < limit)
                x = jnp.where(tied, 1, 0).astype(I32)
                return acc + jnp.sum(x.reshape(KEY_CHUNK // 8, 8, QUERY_BLOCK), axis=0)
            acc = lax.fori_loop(0, n_c, body, jnp.zeros((8, QUERY_BLOCK), I32))
            return jnp.sum(acc, axis=0, keepdims=True)

        n_bits = (n_chunks * KEY_CHUNK).bit_length()

        def limit_step(i, limit):
            cand = limit | jnp.left_shift(jnp.int32(1), n_bits - 1 - i)
            return jnp.where(count_tied_below(cand) <= room, cand, limit)

        limit = lax.fori_loop(0, n_bits, limit_step, jnp.zeros((1, QUERY_BLOCK), I32))

        def write_bias(c, carry):
            keys = keys_ref[c]
            keep = jnp.logical_or(keys > thr, jnp.logical_and(keys == thr, c * KEY_CHUNK + kpos < limit))
            bias_ref[0, 0, c] = jnp.where(keep, 0.0, MASK_NEG).astype(bias_ref.dtype)
            return carry
        lax.fori_loop(0, n_c, write_bias, 0)

    def write_masked(c, carry):
        bias_ref[0, 0, c] = jnp.full((KEY_CHUNK, QUERY_BLOCK), MASK_NEG, bias_ref.dtype)
        return carry

    lax.fori_loop(n_c, n_chunks, write_masked, 0)


def _dsa_select(qi, wit, ki, batch, seq):
    nqb = seq // QUERY_BLOCK
    nkc = seq // KEY_CHUNK
    return pl.pallas_call(
        functools.partial(_dsa_select_kernel, n_chunks=nkc),
        grid=(batch, nqb),
        in_specs=[pl.BlockSpec((QUERY_BLOCK, IDX_HEADS * HEAD_DIM), lambda b, q: (b * nqb + q, 0)),
                  pl.BlockSpec((HEAD_DIM, QUERY_BLOCK), lambda b, q: (0, b * nqb + q)),
                  pl.BlockSpec((1, seq, HEAD_DIM), lambda b, q: (b, 0, 0))],
        out_specs=pl.BlockSpec((1, 1, nkc, KEY_CHUNK, QUERY_BLOCK), lambda b, q: (b, q, 0, 0, 0)),
        out_shape=jax.ShapeDtypeStruct((batch, nqb, nkc, KEY_CHUNK, QUERY_BLOCK), BF16),
        scratch_shapes=[pltpu.VMEM((IDX_HEADS * QUERY_BLOCK, HEAD_DIM), BF16),
                        pltpu.VMEM((nkc + nkc % 2, KEY_CHUNK, QUERY_BLOCK), I32)],
        compiler_params=_params(2),
        name="dsa_select",
    )(qi, wit, ki.reshape(batch, seq, HEAD_DIM))


def _dsa_attn_kernel(qb_tab, c_tab, q_ref, k_ref, v_ref, bias_ref, o_ref, qa_ref, m_ref, l_ref, acc_ref):
    step = pl.program_id(1)
    qb = qb_tab[step]
    c = c_tab[step]
    last = (qb * QUERY_BLOCK + QUERY_BLOCK - 1) // ATT_KEY_CHUNK
    n_tiles = ATT_KEY_CHUNK // HEAD_DIM

    @pl.when(c == 0)
    def _():
        ri = lax.broadcasted_iota(I32, (QUERY_BLOCK, HEAD_DIM), 0)
        ci = lax.broadcasted_iota(I32, (QUERY_BLOCK, HEAD_DIM), 1)
        eye = jnp.where(ri == ci, 1.0, 0.0).astype(BF16)
        for g in range(DSA_KV_HEADS):
            for j in range(DSA_GROUP):
                h = g * DSA_GROUP + j
                rows = slice(j * QUERY_BLOCK, (j + 1) * QUERY_BLOCK)
                qa_ref[g, rows, :HEAD_DIM] = q_ref[:, h * HEAD_DIM:(h + 1) * HEAD_DIM]
                qa_ref[g, rows, HEAD_DIM:] = eye
        m_ref[...] = jnp.full(m_ref.shape, M_INIT, F32)
        l_ref[...] = jnp.zeros_like(l_ref)
        acc_ref[...] = jnp.zeros_like(acc_ref)

    bias_t = bias_ref[0, 0, 0]

    def scores(g):
        ka = jnp.concatenate([k_ref[:, g * HEAD_DIM:(g + 1) * HEAD_DIM], bias_t], axis=1)
        return _dot_nt(qa_ref[g], ka)

    pending = [scores(g) for g in range(SCORES_AHEAD)]
    for g in range(DSA_KV_HEADS):
        sl = slice(g * HEAD_DIM, (g + 1) * HEAD_DIM)
        s = pending.pop(0)
        if g + SCORES_AHEAD < DSA_KV_HEADS:
            pending.append(scores(g + SCORES_AHEAD))
        tiles = [s[:, j * HEAD_DIM:(j + 1) * HEAD_DIM] for j in range(n_tiles)]
        m_old = m_ref[g]
        m_new = jnp.maximum(m_old, jnp.max(functools.reduce(jnp.maximum, tiles), axis=1, keepdims=True))
        alpha = jnp.exp2(m_old - m_new)
        ps = [jnp.exp2(tl - m_new) for tl in tiles]
        l_ref[g] = alpha * l_ref[g] + jnp.sum(functools.reduce(jnp.add, ps), axis=1, keepdims=True)
        p = jnp.concatenate([tl.astype(BF16) for tl in ps], axis=1)
        acc_ref[g] = alpha * acc_ref[g] + jnp.dot(p, v_ref[:, sl], preferred_element_type=F32)
        m_ref[g] = m_new

    @pl.when(c == last)
    def _():
        for g in range(DSA_KV_HEADS):
            o = acc_ref[g] / l_ref[g]
            for j in range(DSA_GROUP):
                h = g * DSA_GROUP + j
                o_ref[:, h * HEAD_DIM:(h + 1) * HEAD_DIM] = (
                    o[j * QUERY_BLOCK:(j + 1) * QUERY_BLOCK].astype(o_ref.dtype))


def _dsa_attention(q, k, v, bias_t, batch, seq):
    nqb = seq // QUERY_BLOCK
    nkc = seq // ATT_KEY_CHUNK
    rows = DSA_GROUP * QUERY_BLOCK
    steps = [(qb, c) for qb in range(nqb)
             for c in range((qb * QUERY_BLOCK + QUERY_BLOCK - 1) // ATT_KEY_CHUNK + 1)]
    qb_tab = jnp.asarray(np.array([s[0] for s in steps], np.int32))
    c_tab = jnp.asarray(np.array([s[1] for s in steps], np.int32))
    qmap = lambda b, s, qt, ct: (b * nqb + qt[s], 0)
    kvmap = lambda b, s, qt, ct: (b * nkc + ct[s], 0)
    grid_spec = pltpu.PrefetchScalarGridSpec(
        num_scalar_prefetch=2,
        grid=(batch, len(steps)),
        in_specs=[pl.BlockSpec((QUERY_BLOCK, DSA_HEADS * HEAD_DIM), qmap),
                  pl.BlockSpec((ATT_KEY_CHUNK, DSA_KV_HEADS * HEAD_DIM), kvmap),
                  pl.BlockSpec((ATT_KEY_CHUNK, DSA_KV_HEADS * HEAD_DIM), kvmap),
                  pl.BlockSpec((1, 1, 1, ATT_KEY_CHUNK, QUERY_BLOCK),
                               lambda b, s, qt, ct: (b, qt[s], ct[s], 0, 0))],
        out_specs=pl.BlockSpec((QUERY_BLOCK, DSA_HEADS * HEAD_DIM), qmap),
        scratch_shapes=[pltpu.VMEM((DSA_KV_HEADS, rows, 2 * HEAD_DIM), BF16),
                        pltpu.VMEM((DSA_KV_HEADS, rows, HEAD_DIM), F32),
                        pltpu.VMEM((DSA_KV_HEADS, rows, HEAD_DIM), F32),
                        pltpu.VMEM((DSA_KV_HEADS, rows, HEAD_DIM), F32)])
    return pl.pallas_call(
        _dsa_attn_kernel,
        grid_spec=grid_spec,
        out_shape=jax.ShapeDtypeStruct((batch * seq, DSA_HEADS * HEAD_DIM), BF16),
        compiler_params=_params(2),
        name="dsa_attention",
    )(qb_tab, c_tab, q, k, v, bias_t.reshape(batch, nqb, nkc, ATT_KEY_CHUNK, QUERY_BLOCK))


def kernel(x, positions, l0_attn_w_in, l0_attn_w_out, l1_attn_w_in, l1_k_idx_gain, l1_k_idx_bias,
           l1_attn_w_out, l0_ln_mix_gain, l0_ln_mix_bias, l0_ffn_gate, l0_ffn_up, l0_ffn_down,
           l0_ln_ffn_gain, l0_ln_ffn_bias, l1_ln_mix_gain, l1_ln_mix_bias, l1_ffn_gate, l1_ffn_up,
           l1_ffn_down, l1_ln_ffn_gain, l1_ln_ffn_bias):
    batch, seq, d_model = x.shape
    m = batch * seq
    cosf, sinf = _rope_tables(positions)
    rope = (cosf, sinf)
    xf = x.reshape(m, d_model)
    x_by_dilation = _cast_residue_major(xf, [d for _, d in DIL_GROUPS])

    tn = PROJ_TN
    n_grp = len(DIL_GROUPS)
    grp_blocks = DIL_COLS // tn
    outs, lses = [], []
    for g, (_, d) in enumerate(DIL_GROUPS):
        col_block = lambda n, p0=0, g=g: ((p0 + n // grp_blocks) * n_grp + g) * grp_blocks + n % grp_blocks
        rope_g = tuple(_to_residue_major(t, d, batch, seq) for t in rope)
        x_g = x_by_dilation[g]
        qk = _matmul(x_g, l0_attn_w_in, col_block, 2 * DIL_COLS, BF16, tn=tn, rope=rope_g,
                     scale=ATTN_SCALE * LOG2E, scale_cols=DIL_COLS, name=f"l0_qk_proj_d{d}")
        v = _matmul(x_g, l0_attn_w_in, functools.partial(col_block, p0=2), DIL_COLS, BF16, tn=tn,
                    name=f"l0_v_proj_d{d}")
        o, lse = _dilated_attention(qk, v, d, batch, seq)
        outs.append(o)
        lses.append(lse)
    o0 = _dil_merge(outs, lses, [d for _, d in DIL_GROUPS])
    same = lambda n: n
    y = _matmul(o0, l0_attn_w_out, same, d_model, BF16, tn=tn, name="l0_out_proj")
    xf, xb = _res_ln(xf, y, l0_ln_mix_gain, l0_ln_mix_bias)
    h = _ffn_up(xb, l0_ffn_gate, l0_ffn_up)
    y = _matmul(h, l0_ffn_down.astype(BF16), same, d_model, BF16, tm=DOWN_TOKEN_TILE, name="l0_ffn_down")
    xf, xb = _res_ln(xf, y, l0_ln_ffn_gain, l0_ln_ffn_bias)

    b_q = DSA_HEADS * HEAD_DIM
    b_kv = DSA_KV_HEADS * HEAD_DIM
    b_qi = IDX_HEADS * HEAD_DIM
    o_v = b_q + b_kv
    o_qi = o_v + b_kv
    o_ki = o_qi + b_qi
    o_wi = o_ki + HEAD_DIM
    at = lambda col0: (lambda n: n + col0 // tn)
    w1_t = l1_attn_w_in.T
    proj1 = functools.partial(_matmul, xb, w1_t, tn=tn, w_transposed=True)
    q1 = proj1(at(0), b_q, BF16, rope=rope, scale=ATTN_SCALE * LOG2E, scale_cols=b_q, name="l1_q_proj")
    k1 = proj1(at(b_q), b_kv, BF16, rope=rope, name="l1_k_proj")
    v1 = proj1(at(o_v), b_kv, BF16, name="l1_v_proj")
    qi = proj1(at(o_qi), b_qi, BF16, rope=rope, name="l1_qidx_proj")
    w_ki = l1_attn_w_in[:, o_ki:o_wi]
    w_wi = jnp.pad(l1_attn_w_in[:, o_wi:], ((0, 0), (0, HEAD_DIM - IDX_HEADS)))
    ki, wit = _kiwi(xb, w_ki, w_wi, cosf, sinf, l1_k_idx_gain, l1_k_idx_bias)
    bias_t = _dsa_select(qi, wit, ki, batch, seq)
    o1 = _dsa_attention(q1, k1, v1, bias_t, batch, seq)
    y = _matmul(o1, l1_attn_w_out, same, d_model, BF16, tn=tn, name="l1_out_proj")
    xf, xb = _res_ln(xf, y, l1_ln_mix_gain, l1_ln_mix_bias)
    h = _ffn_up(xb, l1_ffn_gate, l1_ffn_up)
    y = _matmul(h, l1_ffn_down.astype(BF16), same, d_model, BF16, tm=DOWN_TOKEN_TILE, name="l1_ffn_down")
    (xf,) = _res_ln(xf, y, l1_ln_ffn_gain, l1_ln_ffn_bias, with_bf16=False)
    return xf.reshape(batch, seq, d_model)
```

```python
import functools
import math

import numpy as np
import jax
import jax.numpy as jnp
from jax import lax
from jax.experimental import pallas as pl
from jax.experimental.pallas import tpu as pltpu

F32 = jnp.float32
BF16 = jnp.bfloat16
I32 = jnp.int32

HEAD_DIM = 128
HALF = HEAD_DIM // 2
ROPE_THETA = 10000.0
LN_EPS = 1e-5
DEPTH = 2
DEEPNORM_ALPHA = (2 * DEPTH) ** 0.25
ATTN_SCALE = HEAD_DIM ** -0.5
LOG2E = math.log2(math.e)

DIL_GROUPS = ((128, 1), (512, 4), (2048, 16))
BAND = 128
DIL_HEADS = 16
DIL_COLS = DIL_HEADS * HEAD_DIM

DSA_HEADS = 32
DSA_KV_HEADS = 8
DSA_GROUP = DSA_HEADS // DSA_KV_HEADS
IDX_HEADS = 32
IDX_HEADS_PER_DOT = 4
TOPK = 256
QUERY_BLOCK = 128
KEY_CHUNK = 512
ATT_KEY_CHUNK = 1024
SCORES_AHEAD = 1

INT_MIN = np.int32(-2 ** 31)
MASK_NEG = -1e30
M_INIT = -1e29

VMEM_LIMIT = 58 * 1024 * 1024
VMEM_RESERVE = 2 * 1024 * 1024

TOKEN_TILE = 1024
PROJ_TN = 512
FFN_TN = 256
FFN_TOKEN_TILE = 2048
DOWN_TOKEN_TILE = 512
LN_TOKEN_TILE = 256
MERGE_HEADS = 4


def _params(n_axes):
    return pltpu.CompilerParams(dimension_semantics=("arbitrary",) * n_axes,
                                vmem_limit_bytes=VMEM_LIMIT)


def _dot_nt(a, b):
    return lax.dot_general(a, b, (((1,), (1,)), ((), ())), preferred_element_type=F32)


def _rope(a, c, s):
    return a * c + pltpu.roll(a, HALF, 1) * s


def _to_residue_major(t, d, batch, seq):
    if d == 1:
        return t
    c = t.shape[1]
    t = t.reshape(batch, seq // (BAND * d), BAND, d, c)
    return t.transpose(0, 1, 3, 2, 4).reshape(batch * seq, c)


def _cast_residue_major_kernel(x_ref, *o_refs, dilations):
    rows = x_ref.shape[0]
    for o_ref, d in zip(o_refs, dilations):
        if d == 1:
            o_ref[...] = x_ref[...].astype(BF16)
            continue
        span = BAND * d
        for base in range(0, rows, span):
            for r in range(d):
                o_ref[base + r * BAND:base + (r + 1) * BAND, :] = (
                    x_ref[pl.ds(base + r, BAND, stride=d), :].astype(BF16))


def _cast_residue_major(x, dilations):
    m, dm = x.shape
    cols = HEAD_DIM
    rows = BAND * max(dilations)
    rows *= 2 if m % (2 * rows) == 0 else 1
    assert m % rows == 0 and dm % cols == 0 and all(rows % (BAND * d) == 0 for d in dilations)
    blk = pl.BlockSpec((rows, cols), lambda i, j: (i, j))
    return pl.pallas_call(
        functools.partial(_cast_residue_major_kernel, dilations=tuple(dilations)),
        grid=(m // rows, dm // cols),
        in_specs=[blk],
        out_specs=[blk] * len(dilations),
        out_shape=[jax.ShapeDtypeStruct((m, dm), BF16)] * len(dilations),
        compiler_params=_params(2),
        name="cast_residue_major",
    )(x)


def _rope_table_kernel(pos_ref, inv_ref, cos_ref, sin_ref):
    ang = pos_ref[...].astype(F32) * inv_ref[...]
    lane = lax.broadcasted_iota(I32, ang.shape, 1)
    cos_ref[...] = jnp.cos(ang)
    sin_ref[...] = jnp.where(lane < HALF, -jnp.sin(ang), jnp.sin(ang))


def _rope_tables(positions):
    m = positions.size
    tm = min(m, TOKEN_TILE)
    inv = ROPE_THETA ** (-jnp.arange(0, HEAD_DIM, 2, dtype=F32) / HEAD_DIM)
    inv2 = jnp.concatenate([inv, inv]).reshape(1, HEAD_DIM)
    return pl.pallas_call(
        _rope_table_kernel,
        grid=(m // tm,),
        in_specs=[pl.BlockSpec((tm, 1), lambda i: (i, 0)),
                  pl.BlockSpec((1, HEAD_DIM), lambda i: (0, 0))],
        out_specs=[pl.BlockSpec((tm, HEAD_DIM), lambda i: (i, 0))] * 2,
        out_shape=[jax.ShapeDtypeStruct((m, HEAD_DIM), F32)] * 2,
        compiler_params=_params(1),
        name="rope_tables",
    )(positions.reshape(m, 1), inv2)


MXU_WIDTH = 256


def _mm_kernel(x_ref, w_ref, *rest, w_is_f32, w_transposed, has_rope, n_blocks, scale_blocks, scale):
    rest = list(rest)
    wbf_ref = rest.pop() if w_is_f32 else w_ref
    o_ref = rest.pop()
    if w_is_f32:
        @pl.when(pl.program_id(1) == 0)
        def _():
            w = w_ref[...]
            wbf_ref[...] = (w.T if w_transposed else w).astype(BF16)

    strips = [slice(j, j + MXU_WIDTH) for j in range(0, o_ref.shape[1], MXU_WIDTH)]
    if has_rope and scale_blocks:
        factor = scale if scale_blocks == n_blocks else jnp.where(pl.program_id(0) < scale_blocks, scale, 1.0)
    for r in range(0, x_ref.shape[0], TOKEN_TILE):
        rows = slice(r, min(r + TOKEN_TILE, x_ref.shape[0]))
        x = x_ref[rows, :]
        accs = [jnp.dot(x, wbf_ref[:, st], preferred_element_type=F32) for st in strips]
        if not has_rope:
            for st, acc in zip(strips, accs):
                o_ref[rows, st] = acc.astype(o_ref.dtype)
            continue
        cos_ref, sin_ref = rest
        c, s = cos_ref[rows, :], sin_ref[rows, :]
        if scale_blocks:
            c, s = c * factor, s * factor
        for st, acc in zip(strips, accs):
            for j in range(0, MXU_WIDTH, HEAD_DIM):
                o_ref[rows, st.start + j:st.start + j + HEAD_DIM] = (
                    _rope(acc[:, j:j + HEAD_DIM], c, s).astype(o_ref.dtype))


def _matmul(x, w, col_block, n_cols, out_dtype, *, tn=PROJ_TN, tm=TOKEN_TILE, rope=None, scale=1.0, scale_cols=0,
            w_transposed=False, name):
    m, k = x.shape
    w_is_f32 = w.dtype == F32

    def vmem_bytes(rows):
        w_bytes = (2 * 4 + 2) * k * tn if w_is_f32 else 2 * k * tn
        rope_bytes = 0 if rope is None else 2 * 2 * rows * HEAD_DIM * 4
        return 2 * rows * k * 2 + w_bytes + 2 * rows * tn * jnp.dtype(out_dtype).itemsize + rope_bytes

    tm = min(m, tm)
    if 2 * tm <= m and m % (2 * tm) == 0 and vmem_bytes(2 * tm) + VMEM_RESERVE <= VMEM_LIMIT:
        tm *= 2
    assert m % tm == 0 and n_cols % tn == 0 and tn % MXU_WIDTH == 0 and scale_cols % tn == 0
    n_blocks = n_cols // tn
    assert w_is_f32 or not w_transposed
    w_mode = {} if w_is_f32 else {"pipeline_mode": pl.Buffered(1)}
    in_specs = [pl.BlockSpec((tm, k), lambda n, i: (i, 0)),
                pl.BlockSpec((tn, k), lambda n, i: (col_block(n), 0)) if w_transposed else
                pl.BlockSpec((k, tn), lambda n, i: (0, col_block(n)), **w_mode)]
    args = [x, w]
    if rope is not None:
        in_specs += [pl.BlockSpec((tm, HEAD_DIM), lambda n, i: (i, 0))] * 2
        args += list(rope)
    return pl.pallas_call(
        functools.partial(_mm_kernel, w_is_f32=w_is_f32, w_transposed=w_transposed, has_rope=rope is not None,
                          n_blocks=n_blocks, scale_blocks=scale_cols // tn, scale=scale),
        grid=(n_blocks, m // tm),
        in_specs=in_specs,
        out_specs=pl.BlockSpec((tm, tn), lambda n, i: (i, n)),
        out_shape=jax.ShapeDtypeStruct((m, n_cols), out_dtype),
        scratch_shapes=[pltpu.VMEM((k, tn), BF16)] if w_is_f32 else [],
        compiler_params=_params(2),
        name=name,
    )(*args)


def _kiwi_kernel(x_ref, wk_ref, ww_ref, cos_ref, sin_ref, g_ref, b_ref, ki_ref, wit_ref, wkbf_ref, wwbf_ref):
    @pl.when(pl.program_id(0) == 0)
    def _():
        wkbf_ref[...] = wk_ref[...].astype(BF16)
        wwbf_ref[...] = ww_ref[...].astype(BF16)

    x = x_ref[...]
    a = jnp.dot(x, wkbf_ref[...], preferred_element_type=F32)
    mu = jnp.mean(a, axis=-1, keepdims=True)
    var = jnp.mean(jnp.square(a - mu), axis=-1, keepdims=True)
    y = (a - mu) * lax.rsqrt(var + LN_EPS) * g_ref[...] + b_ref[...]
    ki_ref[...] = _rope(y, cos_ref[...], sin_ref[...]).astype(ki_ref.dtype)
    wi = jnp.dot(x, wwbf_ref[...], preferred_element_type=F32) * (IDX_HEADS ** -0.5 * HEAD_DIM ** -0.5)
    wit_ref[...] = wi.T


def _kiwi(x, w_ki, w_wi, cosf, sinf, gain, bias):
    m, k = x.shape
    tm = min(m, TOKEN_TILE)
    row = lambda i: (i, 0)
    fixed = lambda i: (0, 0)
    return pl.pallas_call(
        _kiwi_kernel,
        grid=(m // tm,),
        in_specs=[pl.BlockSpec((tm, k), row), pl.BlockSpec((k, HEAD_DIM), fixed),
                  pl.BlockSpec((k, HEAD_DIM), fixed),
                  pl.BlockSpec((tm, HEAD_DIM), row), pl.BlockSpec((tm, HEAD_DIM), row),
                  pl.BlockSpec((1, HEAD_DIM), fixed), pl.BlockSpec((1, HEAD_DIM), fixed)],
        out_specs=[pl.BlockSpec((tm, HEAD_DIM), row), pl.BlockSpec((HEAD_DIM, tm), lambda i: (0, i))],
        out_shape=[jax.ShapeDtypeStruct((m, HEAD_DIM), BF16),
                   jax.ShapeDtypeStruct((HEAD_DIM, m), F32)],
        scratch_shapes=[pltpu.VMEM((k, HEAD_DIM), BF16)] * 2,
        compiler_params=_params(1),
        name="dsa_kiwi_proj",
    )(x, w_ki, w_wi, cosf, sinf, gain.reshape(1, HEAD_DIM), bias.reshape(1, HEAD_DIM))


def _ffn_up_kernel(x_ref, wg_ref, wu_ref, h_ref, wgbf_ref, wubf_ref):
    @pl.when(pl.program_id(1) == 0)
    def _():
        wgbf_ref[...] = wg_ref[...].astype(BF16)
        wubf_ref[...] = wu_ref[...].astype(BF16)

    for r in range(0, x_ref.shape[0], TOKEN_TILE):
        rows = slice(r, min(r + TOKEN_TILE, x_ref.shape[0]))
        x = x_ref[rows, :]
        g = jnp.dot(x, wgbf_ref[...], preferred_element_type=F32)
        u = jnp.dot(x, wubf_ref[...], preferred_element_type=F32)
        h_ref[rows, :] = (g * (1.0 / (1.0 + jnp.exp(-g))) * u).astype(h_ref.dtype)


def _ffn_up(x, w_gate, w_up, *, tn=FFN_TN):
    m, k = x.shape
    n = w_gate.shape[1]
    tm = min(m, FFN_TOKEN_TILE)
    assert n % tn == 0 and m % tm == 0
    wspec = pl.BlockSpec((k, tn), lambda j, i: (0, j))
    return pl.pallas_call(
        _ffn_up_kernel,
        grid=(n // tn, m // tm),
        in_specs=[pl.BlockSpec((tm, k), lambda j, i: (i, 0)), wspec, wspec],
        out_specs=pl.BlockSpec((tm, tn), lambda j, i: (i, j)),
        out_shape=jax.ShapeDtypeStruct((m, n), BF16),
        scratch_shapes=[pltpu.VMEM((k, tn), BF16)] * 2,
        compiler_params=_params(2),
        name="ffn_up",
    )(x, w_gate, w_up)


def _res_ln_kernel(x_ref, y_ref, g_ref, b_ref, o_ref, *maybe_obf_ref):
    z = DEEPNORM_ALPHA * x_ref[...] + y_ref[...].astype(F32)
    mu = jnp.mean(z, axis=-1, keepdims=True)
    d = z - mu
    var = jnp.mean(jnp.square(d), axis=-1, keepdims=True)
    out = d * lax.rsqrt(var + LN_EPS) * g_ref[...] + b_ref[...]
    o_ref[...] = out
    for obf_ref in maybe_obf_ref:
        obf_ref[...] = out.astype(BF16)


def _res_ln(x, y, gain, bias, *, with_bf16=True):
    m, d = x.shape
    tm = min(m, LN_TOKEN_TILE)
    row = lambda i: (i, 0)
    fixed = lambda i: (0, 0)
    n_out = 2 if with_bf16 else 1
    return pl.pallas_call(
        _res_ln_kernel,
        grid=(m // tm,),
        in_specs=[pl.BlockSpec((tm, d), row), pl.BlockSpec((tm, d), row),
                  pl.BlockSpec((1, d), fixed), pl.BlockSpec((1, d), fixed)],
        out_specs=[pl.BlockSpec((tm, d), row)] * n_out,
        out_shape=[jax.ShapeDtypeStruct((m, d), F32), jax.ShapeDtypeStruct((m, d), BF16)][:n_out],
        compiler_params=_params(1),
        name="residual_layernorm",
    )(x, y, gain.reshape(1, d), bias.reshape(1, d))


def _dil_attn_kernel(q_ref, kp_ref, kc_ref, vp_ref, vc_ref, o_ref, lse_ref):
    has_prev = pl.program_id(1) > 0
    qi = lax.broadcasted_iota(I32, (BAND, BAND), 0)
    kj = lax.broadcasted_iota(I32, (BAND, BAND), 1)
    mask_p = jnp.logical_and(kj >= qi, has_prev)
    mask_c = kj <= qi
    lse_ref[...] = jnp.zeros_like(lse_ref)
    heads = [slice(h * HEAD_DIM, (h + 1) * HEAD_DIM) for h in range(DIL_HEADS)]
    scores = [(_dot_nt(q_ref[:, sl], kp_ref[:, sl]), _dot_nt(q_ref[:, sl], kc_ref[:, sl])) for sl in heads]
    probs = []
    for h, (sp, sc) in enumerate(scores):
        sp = jnp.where(mask_p, sp, -jnp.inf)
        sc = jnp.where(mask_c, sc, -jnp.inf)
        m = jnp.max(jnp.maximum(sp, sc), axis=1, keepdims=True)
        pp = jnp.exp2(sp - m)
        pc = jnp.exp2(sc - m)
        l = jnp.sum(pp + pc, axis=1, keepdims=True)
        lse_ref[h // MERGE_HEADS, :, h % MERGE_HEADS:h % MERGE_HEADS + 1] = m + jnp.log2(l)
        probs.append((pp.astype(BF16), pc.astype(BF16), 1.0 / l))
    for sl, (pp, pc, inv_l) in zip(heads, probs):
        o = (jnp.dot(pp, vp_ref[:, sl], preferred_element_type=F32)
             + jnp.dot(pc, vc_ref[:, sl], preferred_element_type=F32))
        o_ref[:, sl] = (o * inv_l).astype(o_ref.dtype)


def _dilated_attention(qk, v, dilation, batch, seq):
    d = dilation
    n_chunks = seq // (BAND * d)
    cur = lambda col: (lambda b, ch, r: ((b * n_chunks + ch) * d + r, col))
    prev = lambda col: (lambda b, ch, r: ((b * n_chunks + jnp.maximum(ch - 1, 0)) * d + r, col))
    blk = (BAND, DIL_COLS)
    planes = DIL_HEADS // MERGE_HEADS
    return pl.pallas_call(
        _dil_attn_kernel,
        grid=(batch, n_chunks, d),
        in_specs=[pl.BlockSpec(blk, cur(0)), pl.BlockSpec(blk, prev(1)), pl.BlockSpec(blk, cur(1)),
                  pl.BlockSpec(blk, prev(0)), pl.BlockSpec(blk, cur(0))],
        out_specs=[pl.BlockSpec(blk, cur(0)),
                   pl.BlockSpec((planes, BAND, HEAD_DIM), lambda b, ch, r: (0,) + cur(0)(b, ch, r))],
        out_shape=[jax.ShapeDtypeStruct((batch * seq, DIL_COLS), BF16),
                   jax.ShapeDtypeStruct((planes, batch * seq, HEAD_DIM), F32)],
        compiler_params=_params(3),
        name=f"dilated_attention_d{d}",
    )(qk, qk, qk, v, v)


def _dil_merge_kernel(*refs, dilations):
    n = len(dilations)
    o_refs, l_refs, out_ref = refs[:n], refs[n:2 * n], refs[2 * n]
    scratch = dict(zip([d for d in dilations if d > 1], refs[2 * n + 1:]))
    rows = out_ref.shape[0]

    def token_order(tile, d):
        if d == 1:
            return tile
        buf = scratch[d]
        span = BAND * d
        for base in range(0, rows, span):
            for r in range(d):
                buf[pl.ds(base + r, BAND, stride=d), :] = tile[base + r * BAND:base + (r + 1) * BAND, :]
        return buf[...]

    lses = [token_order(l_ref[0], d) for l_ref, d in zip(l_refs, dilations)]
    mx = functools.reduce(jnp.maximum, lses)
    es = [jnp.exp2(l - mx) for l in lses]
    inv = 1.0 / functools.reduce(jnp.add, es)
    ws = [e * inv for e in es]
    for j in range(MERGE_HEADS):
        sl = slice(j * HEAD_DIM, (j + 1) * HEAD_DIM)
        acc = None
        for o_ref, w, d in zip(o_refs, ws, dilations):
            term = w[:, j:j + 1] * token_order(o_ref[:, sl].astype(F32), d)
            acc = term if acc is None else acc + term
        out_ref[:, sl] = acc.astype(out_ref.dtype)


def _dil_merge(outs, lses, dilations):
    m = outs[0].shape[0]
    rows = BAND * max(dilations)
    cols = MERGE_HEADS * HEAD_DIM
    assert m % rows == 0 and all(rows % (BAND * d) == 0 for d in dilations)
    oblk = pl.BlockSpec((rows, cols), lambda i, hb: (i, hb))
    lblk = pl.BlockSpec((1, rows, HEAD_DIM), lambda i, hb: (hb, i, 0))
    n = len(dilations)
    return pl.pallas_call(
        functools.partial(_dil_merge_kernel, dilations=tuple(dilations)),
        grid=(m // rows, DIL_HEADS // MERGE_HEADS),
        in_specs=[oblk] * n + [lblk] * n,
        out_specs=oblk,
        out_shape=jax.ShapeDtypeStruct((m, DIL_COLS), BF16),
        scratch_shapes=[pltpu.VMEM((rows, HEAD_DIM), F32) for d in dilations if d > 1],
        compiler_params=_params(2),
        name="dilated_merge",
    )(*outs, *lses)


def _sortable(x):
    b = lax.bitcast_convert_type(x, I32)
    return b ^ ((b >> 31) & np.int32(0x7FFFFFFF))


def _dsa_select_kernel(qi_ref, wit_ref, ki_ref, bias_ref, qs_ref, keys_ref, *, n_chunks):
    qb = pl.program_id(1)
    n_c = (qb * QUERY_BLOCK + QUERY_BLOCK - 1) // KEY_CHUNK + 1
    for h in range(IDX_HEADS):
        qs_ref[h * QUERY_BLOCK:(h + 1) * QUERY_BLOCK, :] = qi_ref[:, h * HEAD_DIM:(h + 1) * HEAD_DIM]
    w = wit_ref[...]
    kpos = lax.broadcasted_iota(I32, (KEY_CHUNK, QUERY_BLOCK), 0)
    t = qb * QUERY_BLOCK + lax.broadcasted_iota(I32, (KEY_CHUNK, QUERY_BLOCK), 1)
    rows_per_dot = IDX_HEADS_PER_DOT * QUERY_BLOCK

    def score_chunk(c):
        kic = ki_ref[0, pl.ds(pl.multiple_of(c * KEY_CHUNK, KEY_CHUNK), KEY_CHUNK), :]
        sc = jnp.zeros((KEY_CHUNK, QUERY_BLOCK), F32)
        for hg in range(IDX_HEADS // IDX_HEADS_PER_DOT):
            r = _dot_nt(kic, qs_ref[hg * rows_per_dot:(hg + 1) * rows_per_dot, :])
            for j in range(IDX_HEADS_PER_DOT):
                h = hg * IDX_HEADS_PER_DOT + j
                sc = sc + jnp.maximum(r[:, j * QUERY_BLOCK:(j + 1) * QUERY_BLOCK], 0.0) * w[h:h + 1, :]
        causal = c * KEY_CHUNK + kpos <= t
        keys_ref[c] = jnp.where(causal, _sortable(sc), INT_MIN)

    def score_pair(c2, carry):
        score_chunk(2 * c2)
        score_chunk(2 * c2 + 1)
        return carry

    lax.fori_loop(0, n_c // 2, score_pair, 0)

    @pl.when(n_c % 2 == 1)
    def _():
        score_chunk(n_c - 1)
        keys_ref[n_c] = jnp.full((KEY_CHUNK, QUERY_BLOCK), INT_MIN, I32)

    n_pairs = (n_c + 1) // 2

    def sublane_counts(hit):
        return jnp.sum(jnp.where(hit, 1, 0).astype(I32).reshape(KEY_CHUNK // 8, 8, QUERY_BLOCK), axis=0)

    def count_ge(cand):
        def body(c2, acc):
            return (acc + sublane_counts(keys_ref[2 * c2] >= cand)
                    + sublane_counts(keys_ref[2 * c2 + 1] >= cand))
        acc = lax.fori_loop(0, n_pairs, body, jnp.zeros((8, QUERY_BLOCK), I32))
        return jnp.sum(acc, axis=0, keepdims=True)

    zero = jnp.zeros((1, QUERY_BLOCK), I32)
    n_zero = count_ge(zero)
    thr = jnp.where(n_zero >= TOPK, zero, jnp.full((1, QUERY_BLOCK), INT_MIN, I32))
    n_ge = jnp.where(n_zero >= TOPK, n_zero, zero)

    def bit_step(i, carry):
        thr, n_ge = carry
        cand = thr | jnp.left_shift(jnp.int32(1), 30 - i)
        n_cand = count_ge(cand)
        take = n_cand >= TOPK
        return jnp.where(take, cand, thr), jnp.where(take, n_cand, n_ge)

    thr, n_ge = lax.fori_loop(0, 31, bit_step, (thr, n_ge))
    thr = jnp.maximum(thr, INT_MIN + 1)
    has_ties = jnp.max(n_ge) > TOPK

    @pl.when(jnp.logical_not(has_ties))
    def _():
        def write_bias(c, carry):
            bias_ref[0, 0, c] = jnp.where(keys_ref[c] >= thr, 0.0, MASK_NEG).astype(bias_ref.dtype)
            return carry
        lax.fori_loop(0, n_c, write_bias, 0)

    @pl.when(has_ties)
    def _():
        room = TOPK - count_ge(thr + 1)

        def count_tied_below(limit):
            def body(c, acc):
                tied = jnp.logical_and(keys_ref[c] == thr, c * KEY_CHUNK + kpos < limit)
                x = jnp.where(tied, 1, 0).astype(I32)
                return acc + jnp.sum(x.reshape(KEY_CHUNK // 8, 8, QUERY_BLOCK), axis=0)
            acc = lax.fori_loop(0, n_c, body, jnp.zeros((8, QUERY_BLOCK), I32))
            return jnp.sum(acc, axis=0, keepdims=True)

        n_bits = (n_chunks * KEY_CHUNK).bit_length()

        def limit_step(i, limit):
            cand = limit | jnp.left_shift(jnp.int32(1), n_bits - 1 - i)
            return jnp.where(count_tied_below(cand) <= room, cand, limit)

        limit = lax.fori_loop(0, n_bits, limit_step, jnp.zeros((1, QUERY_BLOCK), I32))

        def write_bias(c, carry):
            keys = keys_ref[c]
            keep = jnp.logical_or(keys > thr, jnp.logical_and(keys == thr, c * KEY_CHUNK + kpos < limit))
            bias_ref[0, 0, c] = jnp.where(keep, 0.0, MASK_NEG).astype(bias_ref.dtype)
            return carry
        lax.fori_loop(0, n_c, write_bias, 0)

    def write_masked(c, carry):
        bias_ref[0, 0, c] = jnp.full((KEY_CHUNK, QUERY_BLOCK), MASK_NEG, bias_ref.dtype)
        return carry

    lax.fori_loop(n_c, n_chunks, write_masked, 0)


def _dsa_select(qi, wit, ki, batch, seq):
    nqb = seq // QUERY_BLOCK
    nkc = seq // KEY_CHUNK
    return pl.pallas_call(
        functools.partial(_dsa_select_kernel, n_chunks=nkc),
        grid=(batch, nqb),
        in_specs=[pl.BlockSpec((QUERY_BLOCK, IDX_HEADS * HEAD_DIM), lambda b, q: (b * nqb + q, 0)),
                  pl.BlockSpec((HEAD_DIM, QUERY_BLOCK), lambda b, q: (0, b * nqb + q)),
                  pl.BlockSpec((1, seq, HEAD_DIM), lambda b, q: (b, 0, 0))],
        out_specs=pl.BlockSpec((1, 1, nkc, KEY_CHUNK, QUERY_BLOCK), lambda b, q: (b, q, 0, 0, 0)),
        out_shape=jax.ShapeDtypeStruct((batch, nqb, nkc, KEY_CHUNK, QUERY_BLOCK), BF16),
        scratch_shapes=[pltpu.VMEM((IDX_HEADS * QUERY_BLOCK, HEAD_DIM), BF16),
                        pltpu.VMEM((nkc + nkc % 2, KEY_CHUNK, QUERY_BLOCK), I32)],
        compiler_params=_params(2),
        name="dsa_select",
    )(qi, wit, ki.reshape(batch, seq, HEAD_DIM))


def _dsa_attn_kernel(qb_tab, c_tab, q_ref, k_ref, v_ref, bias_ref, o_ref, qa_ref, m_ref, l_ref, acc_ref):
    step = pl.program_id(1)
    qb = qb_tab[step]
    c = c_tab[step]
    last = (qb * QUERY_BLOCK + QUERY_BLOCK - 1) // ATT_KEY_CHUNK
    n_tiles = ATT_KEY_CHUNK // HEAD_DIM

    @pl.when(c == 0)
    def _():
        ri = lax.broadcasted_iota(I32, (QUERY_BLOCK, HEAD_DIM), 0)
        ci = lax.broadcasted_iota(I32, (QUERY_BLOCK, HEAD_DIM), 1)
        eye = jnp.where(ri == ci, 1.0, 0.0).astype(BF16)
        for g in range(DSA_KV_HEADS):
            for j in range(DSA_GROUP):
                h = g * DSA_GROUP + j
                rows = slice(j * QUERY_BLOCK, (j + 1) * QUERY_BLOCK)
                qa_ref[g, rows, :HEAD_DIM] = q_ref[:, h * HEAD_DIM:(h + 1) * HEAD_DIM]
                qa_ref[g, rows, HEAD_DIM:] = eye
        m_ref[...] = jnp.full(m_ref.shape, M_INIT, F32)
        l_ref[...] = jnp.zeros_like(l_ref)
        acc_ref[...] = jnp.zeros_like(acc_ref)

    bias_t = bias_ref[0, 0, 0]

    def scores(g):
        ka = jnp.concatenate([k_ref[:, g * HEAD_DIM:(g + 1) * HEAD_DIM], bias_t], axis=1)
        return _dot_nt(qa_ref[g], ka)

    pending = [scores(g) for g in range(SCORES_AHEAD)]
    for g in range(DSA_KV_HEADS):
        sl = slice(g * HEAD_DIM, (g + 1) * HEAD_DIM)
        s = pending.pop(0)
        if g + SCORES_AHEAD < DSA_KV_HEADS:
            pending.append(scores(g + SCORES_AHEAD))
        tiles = [s[:, j * HEAD_DIM:(j + 1) * HEAD_DIM] for j in range(n_tiles)]
        m_old = m_ref[g]
        m_new = jnp.maximum(m_old, jnp.max(functools.reduce(jnp.maximum, tiles), axis=1, keepdims=True))
        alpha = jnp.exp2(m_old - m_new)
        ps = [jnp.exp2(tl - m_new) for tl in tiles]
        l_ref[g] = alpha * l_ref[g] + jnp.sum(functools.reduce(jnp.add, ps), axis=1, keepdims=True)
        p = jnp.concatenate([tl.astype(BF16) for tl in ps], axis=1)
        acc_ref[g] = alpha * acc_ref[g] + jnp.dot(p, v_ref[:, sl], preferred_element_type=F32)
        m_ref[g] = m_new

    @pl.when(c == last)
    def _():
        for g in range(DSA_KV_HEADS):
            o = acc_ref[g] / l_ref[g]
            for j in range(DSA_GROUP):
                h = g * DSA_GROUP + j
                o_ref[:, h * HEAD_DIM:(h + 1) * HEAD_DIM] = (
                    o[j * QUERY_BLOCK:(j + 1) * QUERY_BLOCK].astype(o_ref.dtype))


def _dsa_attention(q, k, v, bias_t, batch, seq):
    nqb = seq // QUERY_BLOCK
    nkc = seq // ATT_KEY_CHUNK
    rows = DSA_GROUP * QUERY_BLOCK
    steps = [(qb, c) for qb in range(nqb)
             for c in range((qb * QUERY_BLOCK + QUERY_BLOCK - 1) // ATT_KEY_CHUNK + 1)]
    qb_tab = jnp.asarray(np.array([s[0] for s in steps], np.int32))
    c_tab = jnp.asarray(np.array([s[1] for s in steps], np.int32))
    qmap = lambda b, s, qt, ct: (b * nqb + qt[s], 0)
    kvmap = lambda b, s, qt, ct: (b * nkc + ct[s], 0)
    grid_spec = pltpu.PrefetchScalarGridSpec(
        num_scalar_prefetch=2,
        grid=(batch, len(steps)),
        in_specs=[pl.BlockSpec((QUERY_BLOCK, DSA_HEADS * HEAD_DIM), qmap),
                  pl.BlockSpec((ATT_KEY_CHUNK, DSA_KV_HEADS * HEAD_DIM), kvmap),
                  pl.BlockSpec((ATT_KEY_CHUNK, DSA_KV_HEADS * HEAD_DIM), kvmap),
                  pl.BlockSpec((1, 1, 1, ATT_KEY_CHUNK, QUERY_BLOCK),
                               lambda b, s, qt, ct: (b, qt[s], ct[s], 0, 0))],
        out_specs=pl.BlockSpec((QUERY_BLOCK, DSA_HEADS * HEAD_DIM), qmap),
        scratch_shapes=[pltpu.VMEM((DSA_KV_HEADS, rows, 2 * HEAD_DIM), BF16),
                        pltpu.VMEM((DSA_KV_HEADS, rows, HEAD_DIM), F32),
                        pltpu.VMEM((DSA_KV_HEADS, rows, HEAD_DIM), F32),
                        pltpu.VMEM((DSA_KV_HEADS, rows, HEAD_DIM), F32)])
    return pl.pallas_call(
        _dsa_attn_kernel,
        grid_spec=grid_spec,
        out_shape=jax.ShapeDtypeStruct((batch * seq, DSA_HEADS * HEAD_DIM), BF16),
        compiler_params=_params(2),
        name="dsa_attention",
    )(qb_tab, c_tab, q, k, v, bias_t.reshape(batch, nqb, nkc, ATT_KEY_CHUNK, QUERY_BLOCK))


def kernel(x, positions, l0_attn_w_in, l0_attn_w_out, l1_attn_w_in, l1_k_idx_gain, l1_k_idx_bias,
           l1_attn_w_out, l0_ln_mix_gain, l0_ln_mix_bias, l0_ffn_gate, l0_ffn_up, l0_ffn_down,
           l0_ln_ffn_gain, l0_ln_ffn_bias, l1_ln_mix_gain, l1_ln_mix_bias, l1_ffn_gate, l1_ffn_up,
           l1_ffn_down, l1_ln_ffn_gain, l1_ln_ffn_bias):
    batch, seq, d_model = x.shape
    m = batch * seq
    cosf, sinf = _rope_tables(positions)
    rope = (cosf, sinf)
    xf = x.reshape(m, d_model)
    x_by_dilation = _cast_residue_major(xf, [d for _, d in DIL_GROUPS])

    tn = PROJ_TN
    n_grp = len(DIL_GROUPS)
    grp_blocks = DIL_COLS // tn
    outs, lses = [], []
    for g, (_, d) in enumerate(DIL_GROUPS):
        col_block = lambda n, p0=0, g=g: ((p0 + n // grp_blocks) * n_grp + g) * grp_blocks + n % grp_blocks
        rope_g = tuple(_to_residue_major(t, d, batch, seq) for t in rope)
        x_g = x_by_dilation[g]
        qk = _matmul(x_g, l0_attn_w_in, col_block, 2 * DIL_COLS, BF16, tn=tn, rope=rope_g,
                     scale=ATTN_SCALE * LOG2E, scale_cols=DIL_COLS, name=f"l0_qk_proj_d{d}")
        v = _matmul(x_g, l0_attn_w_in, functools.partial(col_block, p0=2), DIL_COLS, BF16, tn=tn,
                    name=f"l0_v_proj_d{d}")
        o, lse = _dilated_attention(qk, v, d, batch, seq)
        outs.append(o)
        lses.append(lse)
    o0 = _dil_merge(outs, lses, [d for _, d in DIL_GROUPS])
    same = lambda n: n
    y = _matmul(o0, l0_attn_w_out, same, d_model, BF16, tn=tn, name="l0_out_proj")
    xf, xb = _res_ln(xf, y, l0_ln_mix_gain, l0_ln_mix_bias)
    h = _ffn_up(xb, l0_ffn_gate, l0_ffn_up)
    y = _matmul(h, l0_ffn_down.astype(BF16), same, d_model, BF16, tm=DOWN_TOKEN_TILE, name="l0_ffn_down")
    xf, xb = _res_ln(xf, y, l0_ln_ffn_gain, l0_ln_ffn_bias)

    b_q = DSA_HEADS * HEAD_DIM
    b_kv = DSA_KV_HEADS * HEAD_DIM
    b_qi = IDX_HEADS * HEAD_DIM
    o_v = b_q + b_kv
    o_qi = o_v + b_kv
    o_ki = o_qi + b_qi
    o_wi = o_ki + HEAD_DIM
    at = lambda col0: (lambda n: n + col0 // tn)
    w1_t = l1_attn_w_in.T
    proj1 = functools.partial(_matmul, xb, w1_t, tn=tn, w_transposed=True)
    q1 = proj1(at(0), b_q, BF16, rope=rope, scale=ATTN_SCALE * LOG2E, scale_cols=b_q, name="l1_q_proj")
    k1 = proj1(at(b_q), b_kv, BF16, rope=rope, name="l1_k_proj")
    v1 = proj1(at(o_v), b_kv, BF16, name="l1_v_proj")
    qi = proj1(at(o_qi), b_qi, BF16, rope=rope, name="l1_qidx_proj")
    w_ki = l1_attn_w_in[:, o_ki:o_wi]
    w_wi = jnp.pad(l1_attn_w_in[:, o_wi:], ((0, 0), (0, HEAD_DIM - IDX_HEADS)))
    ki, wit = _kiwi(xb, w_ki, w_wi, cosf, sinf, l1_k_idx_gain, l1_k_idx_bias)
    bias_t = _dsa_select(qi, wit, ki, batch, seq)
    o1 = _dsa_attention(q1, k1, v1, bias_t, batch, seq)
    y = _matmul(o1, l1_attn_w_out, same, d_model, BF16, tn=tn, name="l1_out_proj")
    xf, xb = _res_ln(xf, y, l1_ln_mix_gain, l1_ln_mix_bias)
    h = _ffn_up(xb, l1_ffn_gate, l1_ffn_up)
    y = _matmul(h, l1_ffn_down.astype(BF16), same, d_model, BF16, tm=DOWN_TOKEN_TILE, name="l1_ffn_down")
    (xf,) = _res_ln(xf, y, l1_ln_ffn_gain, l1_ln_ffn_bias, with_bf16=False)
    return xf.reshape(batch, seq, d_model)
```

```python
import functools
import math

import numpy as np
import jax
import jax.numpy as jnp
from jax import lax
from jax.experimental import pallas as pl
from jax.experimental.pallas import tpu as pltpu

F32 = jnp.float32
BF16 = jnp.bfloat16
I32 = jnp.int32

HEAD_DIM = 128
HALF = HEAD_DIM // 2
ROPE_THETA = 10000.0
LN_EPS = 1e-5
DEPTH = 2
DEEPNORM_ALPHA = (2 * DEPTH) ** 0.25
ATTN_SCALE = HEAD_DIM ** -0.5
LOG2E = math.log2(math.e)

DIL_GROUPS = ((128, 1), (512, 4), (2048, 16))
BAND = 128
DIL_HEADS = 16
DIL_COLS = DIL_HEADS * HEAD_DIM

DSA_HEADS = 32
DSA_KV_HEADS = 8
DSA_GROUP = DSA_HEADS // DSA_KV_HEADS
IDX_HEADS = 32
IDX_HEADS_PER_DOT = 4
TOPK = 256
QUERY_BLOCK = 128
KEY_CHUNK = 512
ATT_KEY_CHUNK = 1024
SCORES_AHEAD = 1

INT_MIN = np.int32(-2 ** 31)
MASK_NEG = -1e30
M_INIT = -1e29

VMEM_LIMIT = 58 * 1024 * 1024
VMEM_RESERVE = 2 * 1024 * 1024
VMEM_MAX = 62 * 1024 * 1024

TOKEN_TILE = 1024
PROJ_TN = 512
FFN_TN = 256
FFN_TOKEN_TILE = 2048
DOWN_TOKEN_TILE = 512
LN_TOKEN_TILE = 256
MERGE_HEADS = 4


def _params(n_axes, vmem_limit=VMEM_LIMIT):
    return pltpu.CompilerParams(dimension_semantics=("arbitrary",) * n_axes,
                                vmem_limit_bytes=vmem_limit)


def _dot_nt(a, b):
    return lax.dot_general(a, b, (((1,), (1,)), ((), ())), preferred_element_type=F32)


def _rope(a, c, s):
    return a * c + pltpu.roll(a, HALF, 1) * s


def _to_residue_major(t, d, batch, seq):
    if d == 1:
        return t
    c = t.shape[1]
    t = t.reshape(batch, seq // (BAND * d), BAND, d, c)
    return t.transpose(0, 1, 3, 2, 4).reshape(batch * seq, c)


def _cast_residue_major_kernel(x_ref, *o_refs, dilations):
    rows = x_ref.shape[0]
    for o_ref, d in zip(o_refs, dilations):
        if d == 1:
            o_ref[...] = x_ref[...].astype(BF16)
            continue
        span = BAND * d
        for base in range(0, rows, span):
            for r in range(d):
                o_ref[base + r * BAND:base + (r + 1) * BAND, :] = (
                    x_ref[pl.ds(base + r, BAND, stride=d), :].astype(BF16))


def _cast_residue_major(x, dilations):
    m, dm = x.shape
    cols = HEAD_DIM
    rows = BAND * max(dilations)
    rows *= 2 if m % (2 * rows) == 0 else 1
    assert m % rows == 0 and dm % cols == 0 and all(rows % (BAND * d) == 0 for d in dilations)
    blk = pl.BlockSpec((rows, cols), lambda i, j: (i, j))
    return pl.pallas_call(
        functools.partial(_cast_residue_major_kernel, dilations=tuple(dilations)),
        grid=(m // rows, dm // cols),
        in_specs=[blk],
        out_specs=[blk] * len(dilations),
        out_shape=[jax.ShapeDtypeStruct((m, dm), BF16)] * len(dilations),
        compiler_params=_params(2),
        name="cast_residue_major",
    )(x)


def _rope_table_kernel(pos_ref, inv_ref, cos_ref, sin_ref):
    ang = pos_ref[...].astype(F32) * inv_ref[...]
    lane = lax.broadcasted_iota(I32, ang.shape, 1)
    cos_ref[...] = jnp.cos(ang)
    sin_ref[...] = jnp.where(lane < HALF, -jnp.sin(ang), jnp.sin(ang))


def _rope_tables(positions):
    m = positions.size
    tm = min(m, TOKEN_TILE)
    inv = ROPE_THETA ** (-jnp.arange(0, HEAD_DIM, 2, dtype=F32) / HEAD_DIM)
    inv2 = jnp.concatenate([inv, inv]).reshape(1, HEAD_DIM)
    return pl.pallas_call(
        _rope_table_kernel,
        grid=(m // tm,),
        in_specs=[pl.BlockSpec((tm, 1), lambda i: (i, 0)),
                  pl.BlockSpec((1, HEAD_DIM), lambda i: (0, 0))],
        out_specs=[pl.BlockSpec((tm, HEAD_DIM), lambda i: (i, 0))] * 2,
        out_shape=[jax.ShapeDtypeStruct((m, HEAD_DIM), F32)] * 2,
        compiler_params=_params(1),
        name="rope_tables",
    )(positions.reshape(m, 1), inv2)


MXU_WIDTH = 256


def _mm_kernel(x_ref, w_ref, *rest, w_is_f32, w_transposed, has_rope, n_blocks, scale_blocks, scale):
    rest = list(rest)
    wbf_ref = rest.pop() if w_is_f32 else w_ref
    o_ref = rest.pop()
    if w_is_f32:
        @pl.when(pl.program_id(1) == 0)
        def _():
            w = w_ref[...]
            wbf_ref[...] = (w.T if w_transposed else w).astype(BF16)

    strips = [slice(j, j + MXU_WIDTH) for j in range(0, o_ref.shape[1], MXU_WIDTH)]
    if has_rope and scale_blocks:
        factor = scale if scale_blocks == n_blocks else jnp.where(pl.program_id(0) < scale_blocks, scale, 1.0)
    for r in range(0, x_ref.shape[0], TOKEN_TILE):
        rows = slice(r, min(r + TOKEN_TILE, x_ref.shape[0]))
        x = x_ref[rows, :]
        accs = [jnp.dot(x, wbf_ref[:, st], preferred_element_type=F32) for st in strips]
        if not has_rope:
            for st, acc in zip(strips, accs):
                o_ref[rows, st] = acc.astype(o_ref.dtype)
            continue
        cos_ref, sin_ref = rest
        c, s = cos_ref[rows, :], sin_ref[rows, :]
        if scale_blocks:
            c, s = c * factor, s * factor
        for st, acc in zip(strips, accs):
            for j in range(0, MXU_WIDTH, HEAD_DIM):
                o_ref[rows, st.start + j:st.start + j + HEAD_DIM] = (
                    _rope(acc[:, j:j + HEAD_DIM], c, s).astype(o_ref.dtype))


def _matmul(x, w, col_block, n_cols, out_dtype, *, tn=PROJ_TN, tm=TOKEN_TILE, rope=None, scale=1.0, scale_cols=0,
            w_transposed=False, name):
    m, k = x.shape
    w_is_f32 = w.dtype == F32

    def vmem_bytes(rows):
        w_bytes = (2 * 4 + 2) * k * tn if w_is_f32 else 2 * k * tn
        rope_bytes = 0 if rope is None else 2 * 2 * rows * HEAD_DIM * 4
        return 2 * rows * k * 2 + w_bytes + 2 * rows * tn * jnp.dtype(out_dtype).itemsize + rope_bytes

    tm = min(m, tm)
    vmem_limit = VMEM_LIMIT
    if 2 * tm <= m and m % (2 * tm) == 0 and vmem_bytes(2 * tm) + VMEM_RESERVE <= VMEM_MAX:
        tm *= 2
        vmem_limit = max(VMEM_LIMIT, vmem_bytes(tm) + VMEM_RESERVE)
    assert m % tm == 0 and n_cols % tn == 0 and tn % MXU_WIDTH == 0 and scale_cols % tn == 0
    n_blocks = n_cols // tn
    assert w_is_f32 or not w_transposed
    w_mode = {} if w_is_f32 else {"pipeline_mode": pl.Buffered(1)}
    in_specs = [pl.BlockSpec((tm, k), lambda n, i: (i, 0)),
                pl.BlockSpec((tn, k), lambda n, i: (col_block(n), 0)) if w_transposed else
                pl.BlockSpec((k, tn), lambda n, i: (0, col_block(n)), **w_mode)]
    args = [x, w]
    if rope is not None:
        in_specs += [pl.BlockSpec((tm, HEAD_DIM), lambda n, i: (i, 0))] * 2
        args += list(rope)
    return pl.pallas_call(
        functools.partial(_mm_kernel, w_is_f32=w_is_f32, w_transposed=w_transposed, has_rope=rope is not None,
                          n_blocks=n_blocks, scale_blocks=scale_cols // tn, scale=scale),
        grid=(n_blocks, m // tm),
        in_specs=in_specs,
        out_specs=pl.BlockSpec((tm, tn), lambda n, i: (i, n)),
        out_shape=jax.ShapeDtypeStruct((m, n_cols), out_dtype),
        scratch_shapes=[pltpu.VMEM((k, tn), BF16)] if w_is_f32 else [],
        compiler_params=_params(2, vmem_limit),
        name=name,
    )(*args)


def _kiwi_kernel(x_ref, wk_ref, ww_ref, cos_ref, sin_ref, g_ref, b_ref, ki_ref, wit_ref, wkbf_ref, wwbf_ref):
    @pl.when(pl.program_id(0) == 0)
    def _():
        wkbf_ref[...] = wk_ref[...].astype(BF16)
        wwbf_ref[...] = ww_ref[...].astype(BF16)

    x = x_ref[...]
    a = jnp.dot(x, wkbf_ref[...], preferred_element_type=F32)
    mu = jnp.mean(a, axis=-1, keepdims=True)
    var = jnp.mean(jnp.square(a - mu), axis=-1, keepdims=True)
    y = (a - mu) * lax.rsqrt(var + LN_EPS) * g_ref[...] + b_ref[...]
    ki_ref[...] = _rope(y, cos_ref[...], sin_ref[...]).astype(ki_ref.dtype)
    wi = jnp.dot(x, wwbf_ref[...], preferred_element_type=F32) * (IDX_HEADS ** -0.5 * HEAD_DIM ** -0.5)
    wit_ref[...] = wi.T


def _kiwi(x, w_ki, w_wi, cosf, sinf, gain, bias):
    m, k = x.shape
    tm = min(m, TOKEN_TILE)
    row = lambda i: (i, 0)
    fixed = lambda i: (0, 0)
    return pl.pallas_call(
        _kiwi_kernel,
        grid=(m // tm,),
        in_specs=[pl.BlockSpec((tm, k), row), pl.BlockSpec((k, HEAD_DIM), fixed),
                  pl.BlockSpec((k, HEAD_DIM), fixed),
                  pl.BlockSpec((tm, HEAD_DIM), row), pl.BlockSpec((tm, HEAD_DIM), row),
                  pl.BlockSpec((1, HEAD_DIM), fixed), pl.BlockSpec((1, HEAD_DIM), fixed)],
        out_specs=[pl.BlockSpec((tm, HEAD_DIM), row), pl.BlockSpec((HEAD_DIM, tm), lambda i: (0, i))],
        out_shape=[jax.ShapeDtypeStruct((m, HEAD_DIM), BF16),
                   jax.ShapeDtypeStruct((HEAD_DIM, m), F32)],
        scratch_shapes=[pltpu.VMEM((k, HEAD_DIM), BF16)] * 2,
        compiler_params=_params(1),
        name="dsa_kiwi_proj",
    )(x, w_ki, w_wi, cosf, sinf, gain.reshape(1, HEAD_DIM), bias.reshape(1, HEAD_DIM))


def _ffn_up_kernel(x_ref, wg_ref, wu_ref, h_ref, wgbf_ref, wubf_ref):
    @pl.when(pl.program_id(1) == 0)
    def _():
        wgbf_ref[...] = wg_ref[...].astype(BF16)
        wubf_ref[...] = wu_ref[...].astype(BF16)

    for r in range(0, x_ref.shape[0], TOKEN_TILE):
        rows = slice(r, min(r + TOKEN_TILE, x_ref.shape[0]))
        x = x_ref[rows, :]
        g = jnp.dot(x, wgbf_ref[...], preferred_element_type=F32)
        u = jnp.dot(x, wubf_ref[...], preferred_element_type=F32)
        h_ref[rows, :] = (g * (1.0 / (1.0 + jnp.exp(-g))) * u).astype(h_ref.dtype)


def _ffn_up(x, w_gate, w_up, *, tn=FFN_TN):
    m, k = x.shape
    n = w_gate.shape[1]
    tm = min(m, FFN_TOKEN_TILE)
    assert n % tn == 0 and m % tm == 0
    wspec = pl.BlockSpec((k, tn), lambda j, i: (0, j))
    return pl.pallas_call(
        _ffn_up_kernel,
        grid=(n // tn, m // tm),
        in_specs=[pl.BlockSpec((tm, k), lambda j, i: (i, 0)), wspec, wspec],
        out_specs=pl.BlockSpec((tm, tn), lambda j, i: (i, j)),
        out_shape=jax.ShapeDtypeStruct((m, n), BF16),
        scratch_shapes=[pltpu.VMEM((k, tn), BF16)] * 2,
        compiler_params=_params(2),
        name="ffn_up",
    )(x, w_gate, w_up)


def _res_ln_kernel(x_ref, y_ref, g_ref, b_ref, o_ref, *maybe_obf_ref):
    z = DEEPNORM_ALPHA * x_ref[...] + y_ref[...].astype(F32)
    mu = jnp.mean(z, axis=-1, keepdims=True)
    d = z - mu
    var = jnp.mean(jnp.square(d), axis=-1, keepdims=True)
    out = d * lax.rsqrt(var + LN_EPS) * g_ref[...] + b_ref[...]
    o_ref[...] = out
    for obf_ref in maybe_obf_ref:
        obf_ref[...] = out.astype(BF16)


def _res_ln(x, y, gain, bias, *, with_bf16=True):
    m, d = x.shape
    tm = min(m, LN_TOKEN_TILE)
    row = lambda i: (i, 0)
    fixed = lambda i: (0, 0)
    n_out = 2 if with_bf16 else 1
    return pl.pallas_call(
        _res_ln_kernel,
        grid=(m // tm,),
        in_specs=[pl.BlockSpec((tm, d), row), pl.BlockSpec((tm, d), row),
                  pl.BlockSpec((1, d), fixed), pl.BlockSpec((1, d), fixed)],
        out_specs=[pl.BlockSpec((tm, d), row)] * n_out,
        out_shape=[jax.ShapeDtypeStruct((m, d), F32), jax.ShapeDtypeStruct((m, d), BF16)][:n_out],
        compiler_params=_params(1),
        name="residual_layernorm",
    )(x, y, gain.reshape(1, d), bias.reshape(1, d))


def _dil_attn_kernel(q_ref, kp_ref, kc_ref, vp_ref, vc_ref, o_ref, lse_ref):
    has_prev = pl.program_id(1) > 0
    qi = lax.broadcasted_iota(I32, (BAND, BAND), 0)
    kj = lax.broadcasted_iota(I32, (BAND, BAND), 1)
    mask_p = jnp.logical_and(kj >= qi, has_prev)
    mask_c = kj <= qi
    lse_ref[...] = jnp.zeros_like(lse_ref)
    heads = [slice(h * HEAD_DIM, (h + 1) * HEAD_DIM) for h in range(DIL_HEADS)]
    scores = [(_dot_nt(q_ref[:, sl], kp_ref[:, sl]), _dot_nt(q_ref[:, sl], kc_ref[:, sl])) for sl in heads]
    probs = []
    for h, (sp, sc) in enumerate(scores):
        sp = jnp.where(mask_p, sp, -jnp.inf)
        sc = jnp.where(mask_c, sc, -jnp.inf)
        m = jnp.max(jnp.maximum(sp, sc), axis=1, keepdims=True)
        pp = jnp.exp2(sp - m)
        pc = jnp.exp2(sc - m)
        l = jnp.sum(pp + pc, axis=1, keepdims=True)
        lse_ref[h // MERGE_HEADS, :, h % MERGE_HEADS:h % MERGE_HEADS + 1] = m + jnp.log2(l)
        probs.append((pp.astype(BF16), pc.astype(BF16), 1.0 / l))
    for sl, (pp, pc, inv_l) in zip(heads, probs):
        o = (jnp.dot(pp, vp_ref[:, sl], preferred_element_type=F32)
             + jnp.dot(pc, vc_ref[:, sl], preferred_element_type=F32))
        o_ref[:, sl] = (o * inv_l).astype(o_ref.dtype)


def _dilated_attention(qk, v, dilation, batch, seq):
    d = dilation
    n_chunks = seq // (BAND * d)
    cur = lambda col: (lambda b, ch, r: ((b * n_chunks + ch) * d + r, col))
    prev = lambda col: (lambda b, ch, r: ((b * n_chunks + jnp.maximum(ch - 1, 0)) * d + r, col))
    blk = (BAND, DIL_COLS)
    planes = DIL_HEADS // MERGE_HEADS
    return pl.pallas_call(
        _dil_attn_kernel,
        grid=(batch, n_chunks, d),
        in_specs=[pl.BlockSpec(blk, cur(0)), pl.BlockSpec(blk, prev(1)), pl.BlockSpec(blk, cur(1)),
                  pl.BlockSpec(blk, prev(0)), pl.BlockSpec(blk, cur(0))],
        out_specs=[pl.BlockSpec(blk, cur(0)),
                   pl.BlockSpec((planes, BAND, HEAD_DIM), lambda b, ch, r: (0,) + cur(0)(b, ch, r))],
        out_shape=[jax.ShapeDtypeStruct((batch * seq, DIL_COLS), BF16),
                   jax.ShapeDtypeStruct((planes, batch * seq, HEAD_DIM), F32)],
        compiler_params=_params(3),
        name=f"dilated_attention_d{d}",
    )(qk, qk, qk, v, v)


def _dil_merge_kernel(*refs, dilations):
    n = len(dilations)
    o_refs, l_refs, out_ref = refs[:n], refs[n:2 * n], refs[2 * n]
    scratch = dict(zip([d for d in dilations if d > 1], refs[2 * n + 1:]))
    rows = out_ref.shape[0]

    def token_order(tile, d):
        if d == 1:
            return tile
        buf = scratch[d]
        span = BAND * d
        for base in range(0, rows, span):
            for r in range(d):
                buf[pl.ds(base + r, BAND, stride=d), :] = tile[base + r * BAND:base + (r + 1) * BAND, :]
        return buf[...]

    lses = [token_order(l_ref[0], d) for l_ref, d in zip(l_refs, dilations)]
    mx = functools.reduce(jnp.maximum, lses)
    es = [jnp.exp2(l - mx) for l in lses]
    inv = 1.0 / functools.reduce(jnp.add, es)
    ws = [e * inv for e in es]
    for j in range(MERGE_HEADS):
        sl = slice(j * HEAD_DIM, (j + 1) * HEAD_DIM)
        acc = None
        for o_ref, w, d in zip(o_refs, ws, dilations):
            term = w[:, j:j + 1] * token_order(o_ref[:, sl].astype(F32), d)
            acc = term if acc is None else acc + term
        out_ref[:, sl] = acc.astype(out_ref.dtype)


def _dil_merge(outs, lses, dilations):
    m = outs[0].shape[0]
    rows = BAND * max(dilations)
    cols = MERGE_HEADS * HEAD_DIM
    assert m % rows == 0 and all(rows % (BAND * d) == 0 for d in dilations)
    oblk = pl.BlockSpec((rows, cols), lambda i, hb: (i, hb))
    lblk = pl.BlockSpec((1, rows, HEAD_DIM), lambda i, hb: (hb, i, 0))
    n = len(dilations)
    return pl.pallas_call(
        functools.partial(_dil_merge_kernel, dilations=tuple(dilations)),
        grid=(m // rows, DIL_HEADS // MERGE_HEADS),
        in_specs=[oblk] * n + [lblk] * n,
        out_specs=oblk,
        out_shape=jax.ShapeDtypeStruct((m, DIL_COLS), BF16),
        scratch_shapes=[pltpu.VMEM((rows, HEAD_DIM), F32) for d in dilations if d > 1],
        compiler_params=_params(2),
        name="dilated_merge",
    )(*outs, *lses)


def _sortable(x):
    b = lax.bitcast_convert_type(x, I32)
    return b ^ ((b >> 31) & np.int32(0x7FFFFFFF))


def _dsa_select_kernel(qi_ref, wit_ref, ki_ref, bias_ref, qs_ref, keys_ref, *, n_chunks):
    qb = pl.program_id(1)
    n_c = (qb * QUERY_BLOCK + QUERY_BLOCK - 1) // KEY_CHUNK + 1
    for h in range(IDX_HEADS):
        qs_ref[h * QUERY_BLOCK:(h + 1) * QUERY_BLOCK, :] = qi_ref[:, h * HEAD_DIM:(h + 1) * HEAD_DIM]
    w = wit_ref[...]
    kpos = lax.broadcasted_iota(I32, (KEY_CHUNK, QUERY_BLOCK), 0)
    t = qb * QUERY_BLOCK + lax.broadcasted_iota(I32, (KEY_CHUNK, QUERY_BLOCK), 1)
    rows_per_dot = IDX_HEADS_PER_DOT * QUERY_BLOCK

    def score_chunk(c):
        kic = ki_ref[0, pl.ds(pl.multiple_of(c * KEY_CHUNK, KEY_CHUNK), KEY_CHUNK), :]
        sc = jnp.zeros((KEY_CHUNK, QUERY_BLOCK), F32)
        for hg in range(IDX_HEADS // IDX_HEADS_PER_DOT):
            r = _dot_nt(kic, qs_ref[hg * rows_per_dot:(hg + 1) * rows_per_dot, :])
            for j in range(IDX_HEADS_PER_DOT):
                h = hg * IDX_HEADS_PER_DOT + j
                sc = sc + jnp.maximum(r[:, j * QUERY_BLOCK:(j + 1) * QUERY_BLOCK], 0.0) * w[h:h + 1, :]
        causal = c * KEY_CHUNK + kpos <= t
        keys_ref[c] = jnp.where(causal, _sortable(sc), INT_MIN)

    def score_pair(c2, carry):
        score_chunk(2 * c2)
        score_chunk(2 * c2 + 1)
        return carry

    lax.fori_loop(0, n_c // 2, score_pair, 0)

    @pl.when(n_c % 2 == 1)
    def _():
        score_chunk(n_c - 1)
        keys_ref[n_c] = jnp.full((KEY_CHUNK, QUERY_BLOCK), INT_MIN, I32)

    n_pairs = (n_c + 1) // 2

    def sublane_counts(hit):
        return jnp.sum(jnp.where(hit, 1, 0).astype(I32).reshape(KEY_CHUNK // 8, 8, QUERY_BLOCK), axis=0)

    def count_ge(cand):
        def body(c2, acc):
            return (acc + sublane_counts(keys_ref[2 * c2] >= cand)
                    + sublane_counts(keys_ref[2 * c2 + 1] >= cand))
        acc = lax.fori_loop(0, n_pairs, body, jnp.zeros((8, QUERY_BLOCK), I32))
        return jnp.sum(acc, axis=0, keepdims=True)

    zero = jnp.zeros((1, QUERY_BLOCK), I32)
    n_zero = count_ge(zero)
    thr = jnp.where(n_zero >= TOPK, zero, jnp.full((1, QUERY_BLOCK), INT_MIN, I32))
    n_ge = jnp.where(n_zero >= TOPK, n_zero, zero)

    def bit_step(i, carry):
        thr, n_ge = carry
        cand = thr | jnp.left_shift(jnp.int32(1), 30 - i)
        n_cand = count_ge(cand)
        take = n_cand >= TOPK
        return jnp.where(take, cand, thr), jnp.where(take, n_cand, n_ge)

    thr, n_ge = lax.fori_loop(0, 31, bit_step, (thr, n_ge))
    thr = jnp.maximum(thr, INT_MIN + 1)
    has_ties = jnp.max(n_ge) > TOPK

    @pl.when(jnp.logical_not(has_ties))
    def _():
        def write_bias(c, carry):
            bias_ref[0, 0, c] = jnp.where(keys_ref[c] >= thr, 0.0, MASK_NEG).astype(bias_ref.dtype)
            return carry
        lax.fori_loop(0, n_c, write_bias, 0)

    @pl.when(has_ties)
    def _():
        room = TOPK - count_ge(thr + 1)

        def count_tied_below(limit):
            def body(c, acc):
                tied = jnp.logical_and(keys_ref[c] == thr, c * KEY_CHUNK + kpos < limit)
                x = jnp.where(tied, 1, 0).astype(I32)
                return acc + jnp.sum(x.reshape(KEY_CHUNK // 8, 8, QUERY_BLOCK), axis=0)
            acc = lax.fori_loop(0, n_c, body, jnp.zeros((8, QUERY_BLOCK), I32))
            return jnp.sum(acc, axis=0, keepdims=True)

        n_bits = (n_chunks * KEY_CHUNK).bit_length()

        def limit_step(i, limit):
            cand = limit | jnp.left_shift(jnp.int32(1), n_bits - 1 - i)
            return jnp.where(count_tied_below(cand) <= room, cand, limit)

        limit = lax.fori_loop(0, n_bits, limit_step, jnp.zeros((1, QUERY_BLOCK), I32))

        def write_bias(c, carry):
            keys = keys_ref[c]
            keep = jnp.logical_or(keys > thr, jnp.logical_and(keys == thr, c * KEY_CHUNK + kpos < limit))
            bias_ref[0, 0, c] = jnp.where(keep, 0.0, MASK_NEG).astype(bias_ref.dtype)
            return carry
        lax.fori_loop(0, n_c, write_bias, 0)

    def write_masked(c, carry):
        bias_ref[0, 0, c] = jnp.full((KEY_CHUNK, QUERY_BLOCK), MASK_NEG, bias_ref.dtype)
        return carry

    lax.fori_loop(n_c, n_chunks, write_masked, 0)


def _dsa_select(qi, wit, ki, batch, seq):
    nqb = seq // QUERY_BLOCK
    nkc = seq // KEY_CHUNK
    return pl.pallas_call(
        functools.partial(_dsa_select_kernel, n_chunks=nkc),
        grid=(batch, nqb),
        in_specs=[pl.BlockSpec((QUERY_BLOCK, IDX_HEADS * HEAD_DIM), lambda b, q: (b * nqb + q, 0)),
                  pl.BlockSpec((HEAD_DIM, QUERY_BLOCK), lambda b, q: (0, b * nqb + q)),
                  pl.BlockSpec((1, seq, HEAD_DIM), lambda b, q: (b, 0, 0))],
        out_specs=pl.BlockSpec((1, 1, nkc, KEY_CHUNK, QUERY_BLOCK), lambda b, q: (b, q, 0, 0, 0)),
        out_shape=jax.ShapeDtypeStruct((batch, nqb, nkc, KEY_CHUNK, QUERY_BLOCK), BF16),
        scratch_shapes=[pltpu.VMEM((IDX_HEADS * QUERY_BLOCK, HEAD_DIM), BF16),
                        pltpu.VMEM((nkc + nkc % 2, KEY_CHUNK, QUERY_BLOCK), I32)],
        compiler_params=_params(2),
        name="dsa_select",
    )(qi, wit, ki.reshape(batch, seq, HEAD_DIM))


def _dsa_attn_kernel(qb_tab, c_tab, q_ref, k_ref, v_ref, bias_ref, o_ref, qa_ref, m_ref, l_ref, acc_ref):
    step = pl.program_id(1)
    qb = qb_tab[step]
    c = c_tab[step]
    last = (qb * QUERY_BLOCK + QUERY_BLOCK - 1) // ATT_KEY_CHUNK
    n_tiles = ATT_KEY_CHUNK // HEAD_DIM

    @pl.when(c == 0)
    def _():
        ri = lax.broadcasted_iota(I32, (QUERY_BLOCK, HEAD_DIM), 0)
        ci = lax.broadcasted_iota(I32, (QUERY_BLOCK, HEAD_DIM), 1)
        eye = jnp.where(ri == ci, 1.0, 0.0).astype(BF16)
        for g in range(DSA_KV_HEADS):
            for j in range(DSA_GROUP):
                h = g * DSA_GROUP + j
                rows = slice(j * QUERY_BLOCK, (j + 1) * QUERY_BLOCK)
                qa_ref[g, rows, :HEAD_DIM] = q_ref[:, h * HEAD_DIM:(h + 1) * HEAD_DIM]
                qa_ref[g, rows, HEAD_DIM:] = eye
        m_ref[...] = jnp.full(m_ref.shape, M_INIT, F32)
        l_ref[...] = jnp.zeros_like(l_ref)
        acc_ref[...] = jnp.zeros_like(acc_ref)

    bias_t = bias_ref[0, 0, 0]

    def scores(g):
        ka = jnp.concatenate([k_ref[:, g * HEAD_DIM:(g + 1) * HEAD_DIM], bias_t], axis=1)
        return _dot_nt(qa_ref[g], ka)

    pending = [scores(g) for g in range(SCORES_AHEAD)]
    for g in range(DSA_KV_HEADS):
        sl = slice(g * HEAD_DIM, (g + 1) * HEAD_DIM)
        s = pending.pop(0)
        if g + SCORES_AHEAD < DSA_KV_HEADS:
            pending.append(scores(g + SCORES_AHEAD))
        tiles = [s[:, j * HEAD_DIM:(j + 1) * HEAD_DIM] for j in range(n_tiles)]
        m_old = m_ref[g]
        m_new = jnp.maximum(m_old, jnp.max(functools.reduce(jnp.maximum, tiles), axis=1, keepdims=True))
        alpha = jnp.exp2(m_old - m_new)
        ps = [jnp.exp2(tl - m_new) for tl in tiles]
        l_ref[g] = alpha * l_ref[g] + jnp.sum(functools.reduce(jnp.add, ps), axis=1, keepdims=True)
        p = jnp.concatenate([tl.astype(BF16) for tl in ps], axis=1)
        acc_ref[g] = alpha * acc_ref[g] + jnp.dot(p, v_ref[:, sl], preferred_element_type=F32)
        m_ref[g] = m_new

    @pl.when(c == last)
    def _():
        for g in range(DSA_KV_HEADS):
            o = acc_ref[g] / l_ref[g]
            for j in range(DSA_GROUP):
                h = g * DSA_GROUP + j
                o_ref[:, h * HEAD_DIM:(h + 1) * HEAD_DIM] = (
                    o[j * QUERY_BLOCK:(j + 1) * QUERY_BLOCK].astype(o_ref.dtype))


def _dsa_attention(q, k, v, bias_t, batch, seq):
    nqb = seq // QUERY_BLOCK
    nkc = seq // ATT_KEY_CHUNK
    rows = DSA_GROUP * QUERY_BLOCK
    steps = [(qb, c) for qb in range(nqb)
             for c in range((qb * QUERY_BLOCK + QUERY_BLOCK - 1) // ATT_KEY_CHUNK + 1)]
    qb_tab = jnp.asarray(np.array([s[0] for s in steps], np.int32))
    c_tab = jnp.asarray(np.array([s[1] for s in steps], np.int32))
    qmap = lambda b, s, qt, ct: (b * nqb + qt[s], 0)
    kvmap = lambda b, s, qt, ct: (b * nkc + ct[s], 0)
    grid_spec = pltpu.PrefetchScalarGridSpec(
        num_scalar_prefetch=2,
        grid=(batch, len(steps)),
        in_specs=[pl.BlockSpec((QUERY_BLOCK, DSA_HEADS * HEAD_DIM), qmap),
                  pl.BlockSpec((ATT_KEY_CHUNK, DSA_KV_HEADS * HEAD_DIM), kvmap),
                  pl.BlockSpec((ATT_KEY_CHUNK, DSA_KV_HEADS * HEAD_DIM), kvmap),
                  pl.BlockSpec((1, 1, 1, ATT_KEY_CHUNK, QUERY_BLOCK),
                               lambda b, s, qt, ct: (b, qt[s], ct[s], 0, 0))],
        out_specs=pl.BlockSpec((QUERY_BLOCK, DSA_HEADS * HEAD_DIM), qmap),
        scratch_shapes=[pltpu.VMEM((DSA_KV_HEADS, rows, 2 * HEAD_DIM), BF16),
                        pltpu.VMEM((DSA_KV_HEADS, rows, HEAD_DIM), F32),
                        pltpu.VMEM((DSA_KV_HEADS, rows, HEAD_DIM), F32),
                        pltpu.VMEM((DSA_KV_HEADS, rows, HEAD_DIM), F32)])
    return pl.pallas_call(
        _dsa_attn_kernel,
        grid_spec=grid_spec,
        out_shape=jax.ShapeDtypeStruct((batch * seq, DSA_HEADS * HEAD_DIM), BF16),
        compiler_params=_params(2),
        name="dsa_attention",
    )(qb_tab, c_tab, q, k, v, bias_t.reshape(batch, nqb, nkc, ATT_KEY_CHUNK, QUERY_BLOCK))


def kernel(x, positions, l0_attn_w_in, l0_attn_w_out, l1_attn_w_in, l1_k_idx_gain, l1_k_idx_bias,
           l1_attn_w_out, l0_ln_mix_gain, l0_ln_mix_bias, l0_ffn_gate, l0_ffn_up, l0_ffn_down,
           l0_ln_ffn_gain, l0_ln_ffn_bias, l1_ln_mix_gain, l1_ln_mix_bias, l1_ffn_gate, l1_ffn_up,
           l1_ffn_down, l1_ln_ffn_gain, l1_ln_ffn_bias):
    batch, seq, d_model = x.shape
    m = batch * seq
    cosf, sinf = _rope_tables(positions)
    rope = (cosf, sinf)
    xf = x.reshape(m, d_model)
    x_by_dilation = _cast_residue_major(xf, [d for _, d in DIL_GROUPS])

    tn = PROJ_TN
    n_grp = len(DIL_GROUPS)
    grp_blocks = DIL_COLS // tn
    outs, lses = [], []
    for g, (_, d) in enumerate(DIL_GROUPS):
        col_block = lambda n, p0=0, g=g: ((p0 + n // grp_blocks) * n_grp + g) * grp_blocks + n % grp_blocks
        rope_g = tuple(_to_residue_major(t, d, batch, seq) for t in rope)
        x_g = x_by_dilation[g]
        qk = _matmul(x_g, l0_attn_w_in, col_block, 2 * DIL_COLS, BF16, tn=tn, rope=rope_g,
                     scale=ATTN_SCALE * LOG2E, scale_cols=DIL_COLS, name=f"l0_qk_proj_d{d}")
        v = _matmul(x_g, l0_attn_w_in, functools.partial(col_block, p0=2), DIL_COLS, BF16, tn=tn,
                    name=f"l0_v_proj_d{d}")
        o, lse = _dilated_attention(qk, v, d, batch, seq)
        outs.append(o)
        lses.append(lse)
    o0 = _dil_merge(outs, lses, [d for _, d in DIL_GROUPS])
    same = lambda n: n
    y = _matmul(o0, l0_attn_w_out, same, d_model, BF16, tn=tn, name="l0_out_proj")
    xf, xb = _res_ln(xf, y, l0_ln_mix_gain, l0_ln_mix_bias)
    h = _ffn_up(xb, l0_ffn_gate, l0_ffn_up)
    y = _matmul(h, l0_ffn_down.astype(BF16), same, d_model, BF16, tm=DOWN_TOKEN_TILE, name="l0_ffn_down")
    xf, xb = _res_ln(xf, y, l0_ln_ffn_gain, l0_ln_ffn_bias)

    b_q = DSA_HEADS * HEAD_DIM
    b_kv = DSA_KV_HEADS * HEAD_DIM
    b_qi = IDX_HEADS * HEAD_DIM
    o_v = b_q + b_kv
    o_qi = o_v + b_kv
    o_ki = o_qi + b_qi
    o_wi = o_ki + HEAD_DIM
    at = lambda col0: (lambda n: n + col0 // tn)
    w1_t = l1_attn_w_in.T
    proj1 = functools.partial(_matmul, xb, w1_t, tn=tn, w_transposed=True)
    q1 = proj1(at(0), b_q, BF16, rope=rope, scale=ATTN_SCALE * LOG2E, scale_cols=b_q, name="l1_q_proj")
    k1 = proj1(at(b_q), b_kv, BF16, rope=rope, name="l1_k_proj")
    v1 = proj1(at(o_v), b_kv, BF16, name="l1_v_proj")
    qi = proj1(at(o_qi), b_qi, BF16, rope=rope, name="l1_qidx_proj")
    w_ki = l1_attn_w_in[:, o_ki:o_wi]
    w_wi = jnp.pad(l1_attn_w_in[:, o_wi:], ((0, 0), (0, HEAD_DIM - IDX_HEADS)))
    ki, wit = _kiwi(xb, w_ki, w_wi, cosf, sinf, l1_k_idx_gain, l1_k_idx_bias)
    bias_t = _dsa_select(qi, wit, ki, batch, seq)
    o1 = _dsa_attention(q1, k1, v1, bias_t, batch, seq)
    y = _matmul(o1, l1_attn_w_out, same, d_model, BF16, tn=tn, name="l1_out_proj")
    xf, xb = _res_ln(xf, y, l1_ln_mix_gain, l1_ln_mix_bias)
    h = _ffn_up(xb, l1_ffn_gate, l1_ffn_up)
    y = _matmul(h, l1_ffn_down.astype(BF16), same, d_model, BF16, tm=DOWN_TOKEN_TILE, name="l1_ffn_down")
    (xf,) = _res_ln(xf, y, l1_ln_ffn_gain, l1_ln_ffn_bias, with_bf16=False)
    return xf.reshape(batch, seq, d_model)
```

```python
import functools
import math

import numpy as np
import jax
import jax.numpy as jnp
from jax import lax
from jax.experimental import pallas as pl
from jax.experimental.pallas import tpu as pltpu

F32 = jnp.float32
BF16 = jnp.bfloat16
I32 = jnp.int32

HEAD_DIM = 128
HALF = HEAD_DIM // 2
ROPE_THETA = 10000.0
LN_EPS = 1e-5
DEPTH = 2
DEEPNORM_ALPHA = (2 * DEPTH) ** 0.25
ATTN_SCALE = HEAD_DIM ** -0.5
LOG2E = math.log2(math.e)

DIL_GROUPS = ((128, 1), (512, 4), (2048, 16))
BAND = 128
DIL_HEADS = 16
DIL_COLS = DIL_HEADS * HEAD_DIM

DSA_HEADS = 32
DSA_KV_HEADS = 8
DSA_GROUP = DSA_HEADS // DSA_KV_HEADS
IDX_HEADS = 32
IDX_HEADS_PER_DOT = 4
TOPK = 256
QUERY_BLOCK = 128
KEY_CHUNK = 512
ATT_KEY_CHUNK = 1024
SCORES_AHEAD = 1

INT_MIN = np.int32(-2 ** 31)
MASK_NEG = -1e30
M_INIT = -1e29

VMEM_LIMIT = 58 * 1024 * 1024
VMEM_RESERVE = 2 * 1024 * 1024
VMEM_MAX = 62 * 1024 * 1024

TOKEN_TILE = 1024
PROJ_TN = 512
FFN_TN = 256
FFN_TOKEN_TILE = 2048
DOWN_TOKEN_TILE = 512
LN_TOKEN_TILE = 256
MERGE_HEADS = 4


def _params(n_axes, vmem_limit=VMEM_LIMIT):
    return pltpu.CompilerParams(dimension_semantics=("arbitrary",) * n_axes,
                                vmem_limit_bytes=vmem_limit)


def _dot_nt(a, b):
    return lax.dot_general(a, b, (((1,), (1,)), ((), ())), preferred_element_type=F32)


def _rope(a, c, s):
    return a * c + pltpu.roll(a, HALF, 1) * s


def _to_residue_major(t, d, batch, seq):
    if d == 1:
        return t
    c = t.shape[1]
    t = t.reshape(batch, seq // (BAND * d), BAND, d, c)
    return t.transpose(0, 1, 3, 2, 4).reshape(batch * seq, c)


def _cast_residue_major_kernel(x_ref, *o_refs, dilations):
    rows = x_ref.shape[0]
    for o_ref, d in zip(o_refs, dilations):
        if d == 1:
            o_ref[...] = x_ref[...].astype(BF16)
            continue
        span = BAND * d
        for base in range(0, rows, span):
            for r in range(d):
                o_ref[base + r * BAND:base + (r + 1) * BAND, :] = (
                    x_ref[pl.ds(base + r, BAND, stride=d), :].astype(BF16))


def _cast_residue_major(x, dilations):
    m, dm = x.shape
    cols = HEAD_DIM
    rows = BAND * max(dilations)
    rows *= 2 if m % (2 * rows) == 0 else 1
    assert m % rows == 0 and dm % cols == 0 and all(rows % (BAND * d) == 0 for d in dilations)
    blk = pl.BlockSpec((rows, cols), lambda i, j: (i, j))
    return pl.pallas_call(
        functools.partial(_cast_residue_major_kernel, dilations=tuple(dilations)),
        grid=(m // rows, dm // cols),
        in_specs=[blk],
        out_specs=[blk] * len(dilations),
        out_shape=[jax.ShapeDtypeStruct((m, dm), BF16)] * len(dilations),
        compiler_params=_params(2),
        name="cast_residue_major",
    )(x)


def _rope_table_kernel(pos_ref, inv_ref, cos_ref, sin_ref):
    ang = pos_ref[...].astype(F32) * inv_ref[...]
    lane = lax.broadcasted_iota(I32, ang.shape, 1)
    cos_ref[...] = jnp.cos(ang)
    sin_ref[...] = jnp.where(lane < HALF, -jnp.sin(ang), jnp.sin(ang))


def _rope_tables(positions):
    m = positions.size
    tm = min(m, TOKEN_TILE)
    inv = ROPE_THETA ** (-jnp.arange(0, HEAD_DIM, 2, dtype=F32) / HEAD_DIM)
    inv2 = jnp.concatenate([inv, inv]).reshape(1, HEAD_DIM)
    return pl.pallas_call(
        _rope_table_kernel,
        grid=(m // tm,),
        in_specs=[pl.BlockSpec((tm, 1), lambda i: (i, 0)),
                  pl.BlockSpec((1, HEAD_DIM), lambda i: (0, 0))],
        out_specs=[pl.BlockSpec((tm, HEAD_DIM), lambda i: (i, 0))] * 2,
        out_shape=[jax.ShapeDtypeStruct((m, HEAD_DIM), F32)] * 2,
        compiler_params=_params(1),
        name="rope_tables",
    )(positions.reshape(m, 1), inv2)


MXU_WIDTH = 256


def _mm_kernel(x_ref, w_ref, *rest, w_is_f32, w_transposed, has_rope, n_blocks, scale_blocks, scale):
    rest = list(rest)
    wbf_ref = rest.pop() if w_is_f32 else w_ref
    o_ref = rest.pop()
    if w_is_f32:
        @pl.when(pl.program_id(1) == 0)
        def _():
            w = w_ref[...]
            wbf_ref[...] = (w.T if w_transposed else w).astype(BF16)

    strips = [slice(j, j + MXU_WIDTH) for j in range(0, o_ref.shape[1], MXU_WIDTH)]
    if has_rope and scale_blocks:
        factor = scale if scale_blocks == n_blocks else jnp.where(pl.program_id(0) < scale_blocks, scale, 1.0)
    for r in range(0, x_ref.shape[0], TOKEN_TILE):
        rows = slice(r, min(r + TOKEN_TILE, x_ref.shape[0]))
        x = x_ref[rows, :]
        accs = [jnp.dot(x, wbf_ref[:, st], preferred_element_type=F32) for st in strips]
        if not has_rope:
            for st, acc in zip(strips, accs):
                o_ref[rows, st] = acc.astype(o_ref.dtype)
            continue
        cos_ref, sin_ref = rest
        c, s = cos_ref[rows, :], sin_ref[rows, :]
        if scale_blocks:
            c, s = c * factor, s * factor
        for st, acc in zip(strips, accs):
            for j in range(0, MXU_WIDTH, HEAD_DIM):
                o_ref[rows, st.start + j:st.start + j + HEAD_DIM] = (
                    _rope(acc[:, j:j + HEAD_DIM], c, s).astype(o_ref.dtype))


def _matmul(x, w, col_block, n_cols, out_dtype, *, tn=PROJ_TN, tm=TOKEN_TILE, rope=None, scale=1.0, scale_cols=0,
            w_transposed=False, name):
    m, k = x.shape
    w_is_f32 = w.dtype == F32

    def vmem_bytes(rows):
        w_bytes = (2 * 4 + 2) * k * tn if w_is_f32 else 2 * k * tn
        rope_bytes = 0 if rope is None else 2 * 2 * rows * HEAD_DIM * 4
        return 2 * rows * k * 2 + w_bytes + 2 * rows * tn * jnp.dtype(out_dtype).itemsize + rope_bytes

    tm = min(m, tm)
    vmem_limit = VMEM_LIMIT
    if 2 * tm <= m and m % (2 * tm) == 0 and vmem_bytes(2 * tm) + VMEM_RESERVE <= VMEM_MAX:
        tm *= 2
        vmem_limit = max(VMEM_LIMIT, vmem_bytes(tm) + VMEM_RESERVE)
    assert m % tm == 0 and n_cols % tn == 0 and tn % MXU_WIDTH == 0 and scale_cols % tn == 0
    n_blocks = n_cols // tn
    assert w_is_f32 or not w_transposed
    w_mode = {} if w_is_f32 else {"pipeline_mode": pl.Buffered(1)}
    in_specs = [pl.BlockSpec((tm, k), lambda n, i: (i, 0)),
                pl.BlockSpec((tn, k), lambda n, i: (col_block(n), 0)) if w_transposed else
                pl.BlockSpec((k, tn), lambda n, i: (0, col_block(n)), **w_mode)]
    args = [x, w]
    if rope is not None:
        in_specs += [pl.BlockSpec((tm, HEAD_DIM), lambda n, i: (i, 0))] * 2
        args += list(rope)
    return pl.pallas_call(
        functools.partial(_mm_kernel, w_is_f32=w_is_f32, w_transposed=w_transposed, has_rope=rope is not None,
                          n_blocks=n_blocks, scale_blocks=scale_cols // tn, scale=scale),
        grid=(n_blocks, m // tm),
        in_specs=in_specs,
        out_specs=pl.BlockSpec((tm, tn), lambda n, i: (i, n)),
        out_shape=jax.ShapeDtypeStruct((m, n_cols), out_dtype),
        scratch_shapes=[pltpu.VMEM((k, tn), BF16)] if w_is_f32 else [],
        compiler_params=_params(2, vmem_limit),
        name=name,
    )(*args)


def _kiwi_kernel(x_ref, wk_ref, ww_ref, cos_ref, sin_ref, g_ref, b_ref, ki_ref, wit_ref, wkbf_ref, wwbf_ref):
    @pl.when(pl.program_id(0) == 0)
    def _():
        wkbf_ref[...] = wk_ref[...].astype(BF16)
        wwbf_ref[...] = ww_ref[...].astype(BF16)

    x = x_ref[...]
    a = jnp.dot(x, wkbf_ref[...], preferred_element_type=F32)
    mu = jnp.mean(a, axis=-1, keepdims=True)
    var = jnp.mean(jnp.square(a - mu), axis=-1, keepdims=True)
    y = (a - mu) * lax.rsqrt(var + LN_EPS) * g_ref[...] + b_ref[...]
    ki_ref[...] = _rope(y, cos_ref[...], sin_ref[...]).astype(ki_ref.dtype)
    wi = jnp.dot(x, wwbf_ref[...], preferred_element_type=F32) * (IDX_HEADS ** -0.5 * HEAD_DIM ** -0.5)
    wit_ref[...] = wi.T


def _kiwi(x, w_ki, w_wi, cosf, sinf, gain, bias):
    m, k = x.shape
    tm = min(m, TOKEN_TILE)
    row = lambda i: (i, 0)
    fixed = lambda i: (0, 0)
    return pl.pallas_call(
        _kiwi_kernel,
        grid=(m // tm,),
        in_specs=[pl.BlockSpec((tm, k), row), pl.BlockSpec((k, HEAD_DIM), fixed),
                  pl.BlockSpec((k, HEAD_DIM), fixed),
                  pl.BlockSpec((tm, HEAD_DIM), row), pl.BlockSpec((tm, HEAD_DIM), row),
                  pl.BlockSpec((1, HEAD_DIM), fixed), pl.BlockSpec((1, HEAD_DIM), fixed)],
        out_specs=[pl.BlockSpec((tm, HEAD_DIM), row), pl.BlockSpec((HEAD_DIM, tm), lambda i: (0, i))],
        out_shape=[jax.ShapeDtypeStruct((m, HEAD_DIM), BF16),
                   jax.ShapeDtypeStruct((HEAD_DIM, m), F32)],
        scratch_shapes=[pltpu.VMEM((k, HEAD_DIM), BF16)] * 2,
        compiler_params=_params(1),
        name="dsa_kiwi_proj",
    )(x, w_ki, w_wi, cosf, sinf, gain.reshape(1, HEAD_DIM), bias.reshape(1, HEAD_DIM))


def _ffn_up_kernel(x_ref, wg_ref, wu_ref, h_ref, wgbf_ref, wubf_ref):
    @pl.when(pl.program_id(1) == 0)
    def _():
        wgbf_ref[...] = wg_ref[...].astype(BF16)
        wubf_ref[...] = wu_ref[...].astype(BF16)

    for r in range(0, x_ref.shape[0], TOKEN_TILE):
        rows = slice(r, min(r + TOKEN_TILE, x_ref.shape[0]))
        x = x_ref[rows, :]
        g = jnp.dot(x, wgbf_ref[...], preferred_element_type=F32)
        u = jnp.dot(x, wubf_ref[...], preferred_element_type=F32)
        h_ref[rows, :] = (g * (1.0 / (1.0 + jnp.exp(-g))) * u).astype(h_ref.dtype)


def _ffn_up(x, w_gate, w_up, *, tn=FFN_TN):
    m, k = x.shape
    n = w_gate.shape[1]
    tm = min(m, FFN_TOKEN_TILE)
    assert n % tn == 0 and m % tm == 0
    wspec = pl.BlockSpec((k, tn), lambda j, i: (0, j))
    return pl.pallas_call(
        _ffn_up_kernel,
        grid=(n // tn, m // tm),
        in_specs=[pl.BlockSpec((tm, k), lambda j, i: (i, 0)), wspec, wspec],
        out_specs=pl.BlockSpec((tm, tn), lambda j, i: (i, j)),
        out_shape=jax.ShapeDtypeStruct((m, n), BF16),
        scratch_shapes=[pltpu.VMEM((k, tn), BF16)] * 2,
        compiler_params=_params(2),
        name="ffn_up",
    )(x, w_gate, w_up)


def _res_ln_kernel(x_ref, y_ref, g_ref, b_ref, o_ref, *maybe_obf_ref):
    z = DEEPNORM_ALPHA * x_ref[...] + y_ref[...].astype(F32)
    mu = jnp.mean(z, axis=-1, keepdims=True)
    d = z - mu
    var = jnp.mean(jnp.square(d), axis=-1, keepdims=True)
    out = d * lax.rsqrt(var + LN_EPS) * g_ref[...] + b_ref[...]
    o_ref[...] = out
    for obf_ref in maybe_obf_ref:
        obf_ref[...] = out.astype(BF16)


def _res_ln(x, y, gain, bias, *, with_bf16=True):
    m, d = x.shape
    tm = min(m, LN_TOKEN_TILE)
    row = lambda i: (i, 0)
    fixed = lambda i: (0, 0)
    n_out = 2 if with_bf16 else 1
    return pl.pallas_call(
        _res_ln_kernel,
        grid=(m // tm,),
        in_specs=[pl.BlockSpec((tm, d), row), pl.BlockSpec((tm, d), row),
                  pl.BlockSpec((1, d), fixed), pl.BlockSpec((1, d), fixed)],
        out_specs=[pl.BlockSpec((tm, d), row)] * n_out,
        out_shape=[jax.ShapeDtypeStruct((m, d), F32), jax.ShapeDtypeStruct((m, d), BF16)][:n_out],
        compiler_params=_params(1),
        name="residual_layernorm",
    )(x, y, gain.reshape(1, d), bias.reshape(1, d))


def _dil_attn_kernel(q_ref, kp_ref, kc_ref, vp_ref, vc_ref, o_ref, lse_ref):
    has_prev = pl.program_id(1) > 0
    qi = lax.broadcasted_iota(I32, (BAND, BAND), 0)
    kj = lax.broadcasted_iota(I32, (BAND, BAND), 1)
    mask_p = jnp.logical_and(kj >= qi, has_prev)
    mask_c = kj <= qi
    lse_ref[...] = jnp.zeros_like(lse_ref)
    heads = [slice(h * HEAD_DIM, (h + 1) * HEAD_DIM) for h in range(DIL_HEADS)]
    scores = [(_dot_nt(q_ref[:, sl], kp_ref[:, sl]), _dot_nt(q_ref[:, sl], kc_ref[:, sl])) for sl in heads]
    probs = []
    for h, (sp, sc) in enumerate(scores):
        sp = jnp.where(mask_p, sp, -jnp.inf)
        sc = jnp.where(mask_c, sc, -jnp.inf)
        m = jnp.max(jnp.maximum(sp, sc), axis=1, keepdims=True)
        pp = jnp.exp2(sp - m)
        pc = jnp.exp2(sc - m)
        l = jnp.sum(pp + pc, axis=1, keepdims=True)
        lse_ref[h // MERGE_HEADS, :, h % MERGE_HEADS:h % MERGE_HEADS + 1] = m + jnp.log2(l)
        probs.append((pp.astype(BF16), pc.astype(BF16), 1.0 / l))
    for sl, (pp, pc, inv_l) in zip(heads, probs):
        o = (jnp.dot(pp, vp_ref[:, sl], preferred_element_type=F32)
             + jnp.dot(pc, vc_ref[:, sl], preferred_element_type=F32))
        o_ref[:, sl] = (o * inv_l).astype(o_ref.dtype)


def _dilated_attention(qk, v, dilation, batch, seq):
    d = dilation
    n_chunks = seq // (BAND * d)
    cur = lambda col: (lambda b, ch, r: ((b * n_chunks + ch) * d + r, col))
    prev = lambda col: (lambda b, ch, r: ((b * n_chunks + jnp.maximum(ch - 1, 0)) * d + r, col))
    blk = (BAND, DIL_COLS)
    planes = DIL_HEADS // MERGE_HEADS
    return pl.pallas_call(
        _dil_attn_kernel,
        grid=(batch, n_chunks, d),
        in_specs=[pl.BlockSpec(blk, cur(0)), pl.BlockSpec(blk, prev(1)), pl.BlockSpec(blk, cur(1)),
                  pl.BlockSpec(blk, prev(0)), pl.BlockSpec(blk, cur(0))],
        out_specs=[pl.BlockSpec(blk, cur(0)),
                   pl.BlockSpec((planes, BAND, HEAD_DIM), lambda b, ch, r: (0,) + cur(0)(b, ch, r))],
        out_shape=[jax.ShapeDtypeStruct((batch * seq, DIL_COLS), BF16),
                   jax.ShapeDtypeStruct((planes, batch * seq, HEAD_DIM), F32)],
        compiler_params=_params(3),
        name=f"dilated_attention_d{d}",
    )(qk, qk, qk, v, v)


def _dil_merge_kernel(*refs, dilations):
    n = len(dilations)
    o_refs, l_refs, out_ref = refs[:n], refs[n:2 * n], refs[2 * n]
    scratch = dict(zip([d for d in dilations if d > 1], refs[2 * n + 1:]))
    rows = out_ref.shape[0]

    def token_order(tile, d):
        if d == 1:
            return tile
        buf = scratch[d]
        span = BAND * d
        for base in range(0, rows, span):
            for r in range(d):
                buf[pl.ds(base + r, BAND, stride=d), :] = tile[base + r * BAND:base + (r + 1) * BAND, :]
        return buf[...]

    lses = [token_order(l_ref[0], d) for l_ref, d in zip(l_refs, dilations)]
    mx = functools.reduce(jnp.maximum, lses)
    es = [jnp.exp2(l - mx) for l in lses]
    inv = 1.0 / functools.reduce(jnp.add, es)
    ws = [e * inv for e in es]
    for j in range(MERGE_HEADS):
        sl = slice(j * HEAD_DIM, (j + 1) * HEAD_DIM)
        acc = None
        for o_ref, w, d in zip(o_refs, ws, dilations):
            term = w[:, j:j + 1] * token_order(o_ref[:, sl].astype(F32), d)
            acc = term if acc is None else acc + term
        out_ref[:, sl] = acc.astype(out_ref.dtype)


def _dil_merge(outs, lses, dilations):
    m = outs[0].shape[0]
    rows = BAND * max(dilations)
    cols = MERGE_HEADS * HEAD_DIM
    assert m % rows == 0 and all(rows % (BAND * d) == 0 for d in dilations)
    oblk = pl.BlockSpec((rows, cols), lambda i, hb: (i, hb))
    lblk = pl.BlockSpec((1, rows, HEAD_DIM), lambda i, hb: (hb, i, 0))
    n = len(dilations)
    return pl.pallas_call(
        functools.partial(_dil_merge_kernel, dilations=tuple(dilations)),
        grid=(m // rows, DIL_HEADS // MERGE_HEADS),
        in_specs=[oblk] * n + [lblk] * n,
        out_specs=oblk,
        out_shape=jax.ShapeDtypeStruct((m, DIL_COLS), BF16),
        scratch_shapes=[pltpu.VMEM((rows, HEAD_DIM), F32) for d in dilations if d > 1],
        compiler_params=_params(2),
        name="dilated_merge",
    )(*outs, *lses)


def _sortable(x):
    b = lax.bitcast_convert_type(x, I32)
    return b ^ ((b >> 31) & np.int32(0x7FFFFFFF))


def _dsa_select_kernel(qi_ref, wit_ref, ki_ref, bias_ref, qs_ref, keys_ref, *, n_chunks):
    qb = pl.program_id(1)
    n_c = (qb * QUERY_BLOCK + QUERY_BLOCK - 1) // KEY_CHUNK + 1
    for h in range(IDX_HEADS):
        qs_ref[h * QUERY_BLOCK:(h + 1) * QUERY_BLOCK, :] = qi_ref[:, h * HEAD_DIM:(h + 1) * HEAD_DIM]
    w = wit_ref[...]
    kpos = lax.broadcasted_iota(I32, (KEY_CHUNK, QUERY_BLOCK), 0)
    t = qb * QUERY_BLOCK + lax.broadcasted_iota(I32, (KEY_CHUNK, QUERY_BLOCK), 1)
    rows_per_dot = IDX_HEADS_PER_DOT * QUERY_BLOCK

    def score_chunk(c):
        kic = ki_ref[0, pl.ds(pl.multiple_of(c * KEY_CHUNK, KEY_CHUNK), KEY_CHUNK), :]
        sc = jnp.zeros((KEY_CHUNK, QUERY_BLOCK), F32)
        for hg in range(IDX_HEADS // IDX_HEADS_PER_DOT):
            r = _dot_nt(kic, qs_ref[hg * rows_per_dot:(hg + 1) * rows_per_dot, :])
            for j in range(IDX_HEADS_PER_DOT):
                h = hg * IDX_HEADS_PER_DOT + j
                sc = sc + jnp.maximum(r[:, j * QUERY_BLOCK:(j + 1) * QUERY_BLOCK], 0.0) * w[h:h + 1, :]
        causal = c * KEY_CHUNK + kpos <= t
        keys_ref[c] = jnp.where(causal, _sortable(sc), INT_MIN)

    def score_quad(c4, carry):
        for j in range(4):
            score_chunk(4 * c4 + j)
        return carry

    lax.fori_loop(0, n_c // 4, score_quad, 0)

    @pl.when(n_c % 4 >= 2)
    def _():
        score_chunk((n_c // 4) * 4)
        score_chunk((n_c // 4) * 4 + 1)

    @pl.when(n_c % 2 == 1)
    def _():
        score_chunk(n_c - 1)
        keys_ref[n_c] = jnp.full((KEY_CHUNK, QUERY_BLOCK), INT_MIN, I32)

    n_pairs = (n_c + 1) // 2

    def sublane_counts(hit):
        return jnp.sum(jnp.where(hit, 1, 0).astype(I32).reshape(KEY_CHUNK // 8, 8, QUERY_BLOCK), axis=0)

    def count_ge(cand):
        def body(c2, acc):
            return (acc + sublane_counts(keys_ref[2 * c2] >= cand)
                    + sublane_counts(keys_ref[2 * c2 + 1] >= cand))
        acc = lax.fori_loop(0, n_pairs, body, jnp.zeros((8, QUERY_BLOCK), I32))
        return jnp.sum(acc, axis=0, keepdims=True)

    zero = jnp.zeros((1, QUERY_BLOCK), I32)
    n_zero = count_ge(zero)
    thr = jnp.where(n_zero >= TOPK, zero, jnp.full((1, QUERY_BLOCK), INT_MIN, I32))
    n_ge = jnp.where(n_zero >= TOPK, n_zero, zero)

    def bit_step(i, carry):
        thr, n_ge = carry
        cand = thr | jnp.left_shift(jnp.int32(1), 30 - i)
        n_cand = count_ge(cand)
        take = n_cand >= TOPK
        return jnp.where(take, cand, thr), jnp.where(take, n_cand, n_ge)

    thr, n_ge = lax.fori_loop(0, 31, bit_step, (thr, n_ge))
    thr = jnp.maximum(thr, INT_MIN + 1)
    has_ties = jnp.max(n_ge) > TOPK

    @pl.when(jnp.logical_not(has_ties))
    def _():
        def write_bias(c, carry):
            bias_ref[0, 0, c] = jnp.where(keys_ref[c] >= thr, 0.0, MASK_NEG).astype(bias_ref.dtype)
            return carry
        lax.fori_loop(0, n_c, write_bias, 0)

    @pl.when(has_ties)
    def _():
        room = TOPK - count_ge(thr + 1)

        def count_tied_below(limit):
            def body(c, acc):
                tied = jnp.logical_and(keys_ref[c] == thr, c * KEY_CHUNK + kpos < limit)
                x = jnp.where(tied, 1, 0).astype(I32)
                return acc + jnp.sum(x.reshape(KEY_CHUNK // 8, 8, QUERY_BLOCK), axis=0)
            acc = lax.fori_loop(0, n_c, body, jnp.zeros((8, QUERY_BLOCK), I32))
            return jnp.sum(acc, axis=0, keepdims=True)

        n_bits = (n_chunks * KEY_CHUNK).bit_length()

        def limit_step(i, limit):
            cand = limit | jnp.left_shift(jnp.int32(1), n_bits - 1 - i)
            return jnp.where(count_tied_below(cand) <= room, cand, limit)

        limit = lax.fori_loop(0, n_bits, limit_step, jnp.zeros((1, QUERY_BLOCK), I32))

        def write_bias(c, carry):
            keys = keys_ref[c]
            keep = jnp.logical_or(keys > thr, jnp.logical_and(keys == thr, c * KEY_CHUNK + kpos < limit))
            bias_ref[0, 0, c] = jnp.where(keep, 0.0, MASK_NEG).astype(bias_ref.dtype)
            return carry
        lax.fori_loop(0, n_c, write_bias, 0)

    def write_masked(c, carry):
        bias_ref[0, 0, c] = jnp.full((KEY_CHUNK, QUERY_BLOCK), MASK_NEG, bias_ref.dtype)
        return carry

    lax.fori_loop(n_c, n_chunks, write_masked, 0)


def _dsa_select(qi, wit, ki, batch, seq):
    nqb = seq // QUERY_BLOCK
    nkc = seq // KEY_CHUNK
    return pl.pallas_call(
        functools.partial(_dsa_select_kernel, n_chunks=nkc),
        grid=(batch, nqb),
        in_specs=[pl.BlockSpec((QUERY_BLOCK, IDX_HEADS * HEAD_DIM), lambda b, q: (b * nqb + q, 0)),
                  pl.BlockSpec((HEAD_DIM, QUERY_BLOCK), lambda b, q: (0, b * nqb + q)),
                  pl.BlockSpec((1, seq, HEAD_DIM), lambda b, q: (b, 0, 0))],
        out_specs=pl.BlockSpec((1, 1, nkc, KEY_CHUNK, QUERY_BLOCK), lambda b, q: (b, q, 0, 0, 0)),
        out_shape=jax.ShapeDtypeStruct((batch, nqb, nkc, KEY_CHUNK, QUERY_BLOCK), BF16),
        scratch_shapes=[pltpu.VMEM((IDX_HEADS * QUERY_BLOCK, HEAD_DIM), BF16),
                        pltpu.VMEM((nkc + nkc % 2, KEY_CHUNK, QUERY_BLOCK), I32)],
        compiler_params=_params(2),
        name="dsa_select",
    )(qi, wit, ki.reshape(batch, seq, HEAD_DIM))


def _dsa_attn_kernel(qb_tab, c_tab, q_ref, k_ref, v_ref, bias_ref, o_ref, qa_ref, m_ref, l_ref, acc_ref):
    step = pl.program_id(1)
    qb = qb_tab[step]
    c = c_tab[step]
    last = (qb * QUERY_BLOCK + QUERY_BLOCK - 1) // ATT_KEY_CHUNK
    n_tiles = ATT_KEY_CHUNK // HEAD_DIM

    @pl.when(c == 0)
    def _():
        ri = lax.broadcasted_iota(I32, (QUERY_BLOCK, HEAD_DIM), 0)
        ci = lax.broadcasted_iota(I32, (QUERY_BLOCK, HEAD_DIM), 1)
        eye = jnp.where(ri == ci, 1.0, 0.0).astype(BF16)
        for g in range(DSA_KV_HEADS):
            for j in range(DSA_GROUP):
                h = g * DSA_GROUP + j
                rows = slice(j * QUERY_BLOCK, (j + 1) * QUERY_BLOCK)
                qa_ref[g, rows, :HEAD_DIM] = q_ref[:, h * HEAD_DIM:(h + 1) * HEAD_DIM]
                qa_ref[g, rows, HEAD_DIM:] = eye
        m_ref[...] = jnp.full(m_ref.shape, M_INIT, F32)
        l_ref[...] = jnp.zeros_like(l_ref)
        acc_ref[...] = jnp.zeros_like(acc_ref)

    bias_t = bias_ref[0, 0, 0]

    def scores(g):
        ka = jnp.concatenate([k_ref[:, g * HEAD_DIM:(g + 1) * HEAD_DIM], bias_t], axis=1)
        return _dot_nt(qa_ref[g], ka)

    pending = [scores(g) for g in range(SCORES_AHEAD)]
    for g in range(DSA_KV_HEADS):
        sl = slice(g * HEAD_DIM, (g + 1) * HEAD_DIM)
        s = pending.pop(0)
        if g + SCORES_AHEAD < DSA_KV_HEADS:
            pending.append(scores(g + SCORES_AHEAD))
        tiles = [s[:, j * HEAD_DIM:(j + 1) * HEAD_DIM] for j in range(n_tiles)]
        m_old = m_ref[g]
        m_new = jnp.maximum(m_old, jnp.max(functools.reduce(jnp.maximum, tiles), axis=1, keepdims=True))
        alpha = jnp.exp2(m_old - m_new)
        ps = [jnp.exp2(tl - m_new) for tl in tiles]
        l_ref[g] = alpha * l_ref[g] + jnp.sum(functools.reduce(jnp.add, ps), axis=1, keepdims=True)
        p = jnp.concatenate([tl.astype(BF16) for tl in ps], axis=1)
        acc_ref[g] = alpha * acc_ref[g] + jnp.dot(p, v_ref[:, sl], preferred_element_type=F32)
        m_ref[g] = m_new

    @pl.when(c == last)
    def _():
        for g in range(DSA_KV_HEADS):
            o = acc_ref[g] / l_ref[g]
            for j in range(DSA_GROUP):
                h = g * DSA_GROUP + j
                o_ref[:, h * HEAD_DIM:(h + 1) * HEAD_DIM] = (
                    o[j * QUERY_BLOCK:(j + 1) * QUERY_BLOCK].astype(o_ref.dtype))


def _dsa_attention(q, k, v, bias_t, batch, seq):
    nqb = seq // QUERY_BLOCK
    nkc = seq // ATT_KEY_CHUNK
    rows = DSA_GROUP * QUERY_BLOCK
    steps = [(qb, c) for qb in range(nqb)
             for c in range((qb * QUERY_BLOCK + QUERY_BLOCK - 1) // ATT_KEY_CHUNK + 1)]
    qb_tab = jnp.asarray(np.array([s[0] for s in steps], np.int32))
    c_tab = jnp.asarray(np.array([s[1] for s in steps], np.int32))
    qmap = lambda b, s, qt, ct: (b * nqb + qt[s], 0)
    kvmap = lambda b, s, qt, ct: (b * nkc + ct[s], 0)
    grid_spec = pltpu.PrefetchScalarGridSpec(
        num_scalar_prefetch=2,
        grid=(batch, len(steps)),
        in_specs=[pl.BlockSpec((QUERY_BLOCK, DSA_HEADS * HEAD_DIM), qmap),
                  pl.BlockSpec((ATT_KEY_CHUNK, DSA_KV_HEADS * HEAD_DIM), kvmap),
                  pl.BlockSpec((ATT_KEY_CHUNK, DSA_KV_HEADS * HEAD_DIM), kvmap),
                  pl.BlockSpec((1, 1, 1, ATT_KEY_CHUNK, QUERY_BLOCK),
                               lambda b, s, qt, ct: (b, qt[s], ct[s], 0, 0))],
        out_specs=pl.BlockSpec((QUERY_BLOCK, DSA_HEADS * HEAD_DIM), qmap),
        scratch_shapes=[pltpu.VMEM((DSA_KV_HEADS, rows, 2 * HEAD_DIM), BF16),
                        pltpu.VMEM((DSA_KV_HEADS, rows, HEAD_DIM), F32),
                        pltpu.VMEM((DSA_KV_HEADS, rows, HEAD_DIM), F32),
                        pltpu.VMEM((DSA_KV_HEADS, rows, HEAD_DIM), F32)])
    return pl.pallas_call(
        _dsa_attn_kernel,
        grid_spec=grid_spec,
        out_shape=jax.ShapeDtypeStruct((batch * seq, DSA_HEADS * HEAD_DIM), BF16),
        compiler_params=_params(2),
        name="dsa_attention",
    )(qb_tab, c_tab, q, k, v, bias_t.reshape(batch, nqb, nkc, ATT_KEY_CHUNK, QUERY_BLOCK))


def kernel(x, positions, l0_attn_w_in, l0_attn_w_out, l1_attn_w_in, l1_k_idx_gain, l1_k_idx_bias,
           l1_attn_w_out, l0_ln_mix_gain, l0_ln_mix_bias, l0_ffn_gate, l0_ffn_up, l0_ffn_down,
           l0_ln_ffn_gain, l0_ln_ffn_bias, l1_ln_mix_gain, l1_ln_mix_bias, l1_ffn_gate, l1_ffn_up,
           l1_ffn_down, l1_ln_ffn_gain, l1_ln_ffn_bias):
    batch, seq, d_model = x.shape
    m = batch * seq
    cosf, sinf = _rope_tables(positions)
    rope = (cosf, sinf)
    xf = x.reshape(m, d_model)
    x_by_dilation = _cast_residue_major(xf, [d for _, d in DIL_GROUPS])

    tn = PROJ_TN
    n_grp = len(DIL_GROUPS)
    grp_blocks = DIL_COLS // tn
    outs, lses = [], []
    for g, (_, d) in enumerate(DIL_GROUPS):
        col_block = lambda n, p0=0, g=g: ((p0 + n // grp_blocks) * n_grp + g) * grp_blocks + n % grp_blocks
        rope_g = tuple(_to_residue_major(t, d, batch, seq) for t in rope)
        x_g = x_by_dilation[g]
        qk = _matmul(x_g, l0_attn_w_in, col_block, 2 * DIL_COLS, BF16, tn=tn, rope=rope_g,
                     scale=ATTN_SCALE * LOG2E, scale_cols=DIL_COLS, name=f"l0_qk_proj_d{d}")
        v = _matmul(x_g, l0_attn_w_in, functools.partial(col_block, p0=2), DIL_COLS, BF16, tn=tn,
                    name=f"l0_v_proj_d{d}")
        o, lse = _dilated_attention(qk, v, d, batch, seq)
        outs.append(o)
        lses.append(lse)
    o0 = _dil_merge(outs, lses, [d for _, d in DIL_GROUPS])
    same = lambda n: n
    y = _matmul(o0, l0_attn_w_out, same, d_model, BF16, tn=tn, name="l0_out_proj")
    xf, xb = _res_ln(xf, y, l0_ln_mix_gain, l0_ln_mix_bias)
    h = _ffn_up(xb, l0_ffn_gate, l0_ffn_up)
    y = _matmul(h, l0_ffn_down.astype(BF16), same, d_model, BF16, tm=DOWN_TOKEN_TILE, name="l0_ffn_down")
    xf, xb = _res_ln(xf, y, l0_ln_ffn_gain, l0_ln_ffn_bias)

    b_q = DSA_HEADS * HEAD_DIM
    b_kv = DSA_KV_HEADS * HEAD_DIM
    b_qi = IDX_HEADS * HEAD_DIM
    o_v = b_q + b_kv
    o_qi = o_v + b_kv
    o_ki = o_qi + b_qi
    o_wi = o_ki + HEAD_DIM
    at = lambda col0: (lambda n: n + col0 // tn)
    w1_t = l1_attn_w_in.T
    proj1 = functools.partial(_matmul, xb, w1_t, tn=tn, w_transposed=True)
    q1 = proj1(at(0), b_q, BF16, rope=rope, scale=ATTN_SCALE * LOG2E, scale_cols=b_q, name="l1_q_proj")
    k1 = proj1(at(b_q), b_kv, BF16, rope=rope, name="l1_k_proj")
    v1 = proj1(at(o_v), b_kv, BF16, name="l1_v_proj")
    qi = proj1(at(o_qi), b_qi, BF16, rope=rope, name="l1_qidx_proj")
    w_ki = l1_attn_w_in[:, o_ki:o_wi]
    w_wi = jnp.pad(l1_attn_w_in[:, o_wi:], ((0, 0), (0, HEAD_DIM - IDX_HEADS)))
    ki, wit = _kiwi(xb, w_ki, w_wi, cosf, sinf, l1_k_idx_gain, l1_k_idx_bias)
    bias_t = _dsa_select(qi, wit, ki, batch, seq)
    o1 = _dsa_attention(q1, k1, v1, bias_t, batch, seq)
    y = _matmul(o1, l1_attn_w_out, same, d_model, BF16, tn=tn, name="l1_out_proj")
    xf, xb = _res_ln(xf, y, l1_ln_mix_gain, l1_ln_mix_bias)
    h = _ffn_up(xb, l1_ffn_gate, l1_ffn_up)
    y = _matmul(h, l1_ffn_down.astype(BF16), same, d_model, BF16, tm=DOWN_TOKEN_TILE, name="l1_ffn_down")
    (xf,) = _res_ln(xf, y, l1_ln_ffn_gain, l1_ln_ffn_bias, with_bf16=False)
    return xf.reshape(batch, seq, d_model)
```
